```python
import math
import jax, jax.numpy as jnp
from jax import lax
import numpy as np

D_MODEL = 2048
BATCH = 4
SEQ = 2048
DEPTH = 1

HEAD_DIM = 128
SB_HEADS = 8
DIL_GROUPS = ((128, 1), (512, 4), (2048, 16))
DIL_HEADS_PER_GROUP = 4
N_DIL_GROUPS = len(DIL_GROUPS)
DIL_HEADS = DIL_HEADS_PER_GROUP * N_DIL_GROUPS
BLOCK = 128
SB_W = SB_HEADS * HEAD_DIM
DIL_W = DIL_HEADS * HEAD_DIM
DIL_OUT = DIL_HEADS_PER_GROUP * HEAD_DIM
N_BRANCHES = 2
IN_COLS = 3 * SB_W + 3 * DIL_W + N_BRANCHES * D_MODEL
D_FF = -(-8 * D_MODEL // (3 * 256)) * 256
N_MOD = 6
EPS = 1e-6

kernel_name = "hybrid_stickbreak_dilated_alibi_swiglu_adaln"


def rmsnorm(x, g):
    xf = x.astype(jnp.float32)
    y = xf * lax.rsqrt(jnp.mean(xf * xf, axis=-1, keepdims=True) + EPS)
    return (y * g.astype(jnp.float32)).astype(x.dtype)


def alibi_slopes():
    return 2.0 ** (-8.0 * (jnp.arange(DIL_HEADS, dtype=jnp.float32) + 1.0) / DIL_HEADS)


def stick_breaking_attention(q, k, v):
    B, S, H, Dh = q.shape
    nb = S // BLOCK
    scale = Dh ** -0.5
    qb = q.reshape(B, nb, BLOCK, H, Dh).transpose(1, 0, 2, 3, 4)
    key_pos = jnp.arange(S)

    def one_block(args):
        qi, blk = args
        z = jnp.einsum('bqhd,bkhd->bhqk', qi, k, preferred_element_type=jnp.float32) * scale
        q_pos = blk * BLOCK + jnp.arange(BLOCK)
        mask = key_pos[None, :] < q_pos[:, None]
        log_1m = jnp.where(mask, -jax.nn.softplus(z), 0.0)
        suffix = lax.cumsum(log_1m, axis=3, reverse=True) - log_1m
        a = jnp.where(mask, jnp.exp(jax.nn.log_sigmoid(z) + suffix), 0.0)
        return jnp.einsum('bhqk,bkhd->bqhd', a.astype(v.dtype), v)

    out = lax.map(one_block, (qb, jnp.arange(nb)))
    return out.transpose(1, 0, 2, 3, 4).reshape(B, S, H, Dh)


def dilated_window_attention(q, k, v, slopes, window, dilation):
    B, S, H, Dh = q.shape
    span = window // dilation
    L = S // dilation
    Lp = -(-L // BLOCK) * BLOCK
    nb = Lp // BLOCK
    scale = Dh ** -0.5

    def to_sub(t):
        t = t.reshape(B, L, dilation, H, Dh).transpose(0, 2, 1, 3, 4)
        t = jnp.pad(t, ((0, 0), (0, 0), (0, Lp - L), (0, 0), (0, 0)))
        return t.reshape(B, dilation, nb, BLOCK, H, Dh)

    def with_prev(t):
        prev = jnp.pad(t, ((0, 0), (0, 0), (1, 0), (0, 0), (0, 0), (0, 0)))[:, :, :-1]
        return jnp.concatenate([prev, t], axis=3)

    qs = to_sub(q)
    kk = with_prev(to_sub(k))
    vv = with_prev(to_sub(v))
    s = jnp.einsum('brnqhd,brnkhd->brnhqk', qs, kk, preferred_element_type=jnp.float32) * scale
    a_idx = jnp.arange(BLOCK)
    b_idx = jnp.arange(2 * BLOCK)
    rel = a_idx[:, None] + BLOCK - b_idx[None, :]
    blk = jnp.arange(nb)
    valid = ((rel >= 0) & (rel <= span))[None] & ((blk[:, None, None] > 0) | (b_idx[None, None, :] >= BLOCK))
    bias = -slopes[:, None, None] * (dilation * rel).astype(jnp.float32)[None]
    s = jnp.where(valid[None, None, :, None], s + bias[None, None, None], -jnp.inf)
    m = jnp.max(s, axis=-1, keepdims=True)
    p = jnp.exp(s - m)
    l = jnp.sum(p, axis=-1, keepdims=True)
    o = jnp.einsum('brnhqk,brnkhd->brnqhd', (p / l).astype(v.dtype), vv)
    lse = (m + jnp.log(l))[..., 0]
    o = o.reshape(B, dilation, Lp, H, Dh)[:, :, :L].transpose(0, 2, 1, 3, 4).reshape(B, S, H, Dh)
    lse = lse.transpose(0, 1, 2, 4, 3).reshape(B, dilation, Lp, H)[:, :, :L]
    lse = lse.transpose(0, 2, 1, 3).reshape(B, S, H)
    return o, lse


def setup_inputs(seed: int = 0) -> dict:
    key = jax.random.key(seed)
    ks = jax.random.split(key, 16)
    f32 = jnp.float32
    nrm = lambda k, shape, fan_in, gain=1.0: jax.random.normal(k, shape, f32) * (gain * fan_in ** -0.5)
    return {
        "x": jax.random.normal(ks[0], (BATCH, SEQ, D_MODEL), f32),
        "c": jax.random.normal(ks[1], (BATCH, D_MODEL), f32),
        "w_ada": nrm(ks[2], (DEPTH, D_MODEL, N_MOD * D_MODEL), D_MODEL, 0.5),
        "b_ada": 0.01 * jax.random.normal(ks[3], (DEPTH, N_MOD * D_MODEL), f32),
        "g_norm1": 1.0 + 0.01 * jax.random.normal(ks[4], (DEPTH, D_MODEL), f32),
        "g_norm2": 1.0 + 0.01 * jax.random.normal(ks[5], (DEPTH, D_MODEL), f32),
        "g_final": 1.0 + 0.01 * jax.random.normal(ks[6], (D_MODEL,), f32),
        "w_in": nrm(ks[7], (DEPTH, D_MODEL, IN_COLS), D_MODEL),
        "b_gate": 0.01 * jax.random.normal(ks[8], (DEPTH, N_BRANCHES * D_MODEL), f32),
        "w_proj_sb": nrm(ks[9], (DEPTH, SB_W, D_MODEL), SB_W),
        "w_proj_dil": nrm(ks[10], (DEPTH, DIL_OUT, D_MODEL), DIL_OUT),
        "w_out": nrm(ks[11], (DEPTH, D_MODEL, D_MODEL), D_MODEL),
        "w_ffn_gate": nrm(ks[12], (DEPTH, D_MODEL, D_FF), D_MODEL),
        "w_ffn_up": nrm(ks[13], (DEPTH, D_MODEL, D_FF), D_MODEL),
        "w_ffn_down": nrm(ks[14], (DEPTH, D_FF, D_MODEL), D_FF),
    }


def reference(x, c, w_ada, b_ada, g_norm1, g_norm2, g_final, w_in, b_gate, w_proj_sb, w_proj_dil,
              w_out, w_ffn_gate, w_ffn_up, w_ffn_down):
    B, S, D = x.shape
    slopes = alibi_slopes().reshape(N_DIL_GROUPS, DIL_HEADS_PER_GROUP)
    split_idx = [int(i) for i in np.cumsum([SB_W] * 3 + [DIL_W] * 3)]
    for l in range(DEPTH):
        mod = jax.nn.silu(c) @ w_ada[l] + b_ada[l]
        shift1, scale1, gate1, shift2, scale2, gate2 = [t[:, None, :] for t in jnp.split(mod, N_MOD, axis=-1)]

        h = rmsnorm(x, g_norm1[l]) * (1.0 + scale1) + shift1
        proj = h @ w_in[l]
        q_sb, k_sb, v_sb, q_dl, k_dl, v_dl, gate_pre = jnp.split(proj, split_idx, axis=-1)

        sb_shape = (B, S, SB_HEADS, HEAD_DIM)
        o_sb = stick_breaking_attention(q_sb.reshape(sb_shape), k_sb.reshape(sb_shape), v_sb.reshape(sb_shape))
        y_sb = o_sb.reshape(B, S, SB_W) @ w_proj_sb[l]

        dl_shape = (B, S, N_DIL_GROUPS, DIL_HEADS_PER_GROUP, HEAD_DIM)
        q_dl, k_dl, v_dl = q_dl.reshape(dl_shape), k_dl.reshape(dl_shape), v_dl.reshape(dl_shape)
        outs, lses = [], []
        for g, (window, dilation) in enumerate(DIL_GROUPS):
            o_g, lse_g = dilated_window_attention(q_dl[:, :, g], k_dl[:, :, g], v_dl[:, :, g],
                                                  slopes[g], window, dilation)
            outs.append(o_g)
            lses.append(lse_g)
        w_mix = jax.nn.softmax(jnp.stack(lses, axis=0), axis=0)
        o_dl = jnp.sum(w_mix[..., None].astype(x.dtype) * jnp.stack(outs, axis=0), axis=0)
        y_dl = o_dl.reshape(B, S, DIL_OUT) @ w_proj_dil[l]

        gates = jax.nn.sigmoid(gate_pre + b_gate[l])
        g_sb, g_dl = jnp.split(gates, N_BRANCHES, axis=-1)
        mixed = (g_sb * y_sb + g_dl * y_dl) @ w_out[l]
        x = x + gate1 * mixed

        h2 = rmsnorm(x, g_norm2[l]) * (1.0 + scale2) + shift2
        ffn = (jax.nn.silu(h2 @ w_ffn_gate[l]) * (h2 @ w_ffn_up[l])) @ w_ffn_down[l]
        x = x + gate2 * ffn
    return rmsnorm(x, g_final)
```

```python
import functools

import jax
import jax.numpy as jnp
from jax import lax
from jax.experimental import pallas as pl
from jax.experimental.pallas import tpu as pltpu

F32 = jnp.float32
BF16 = jnp.bfloat16

HEAD_DIM = 128
SB_HEADS = 8
DIL_GROUPS = ((128, 1), (512, 4), (2048, 16))
DIL_HEADS_PER_GROUP = 4
DIL_HEADS = DIL_HEADS_PER_GROUP * len(DIL_GROUPS)
BLOCK = 128
SB_W = SB_HEADS * HEAD_DIM
DIL_W = DIL_HEADS * HEAD_DIM
DIL_OUT = DIL_HEADS_PER_GROUP * HEAD_DIM
N_MOD = 6
EPS = 1e-6

MIB = 1024 * 1024
ADA_ROWS = 8


def _params(n_axes, vmem_mib):
    return pltpu.CompilerParams(dimension_semantics=("arbitrary",) * n_axes,
                                vmem_limit_bytes=vmem_mib * MIB)


def _dot(a, b):
    return jnp.dot(a, b, preferred_element_type=F32)


def _dot_nt(a, b):
    return lax.dot_general(a, b, (((1,), (1,)), ((), ())), preferred_element_type=F32)


def _sigmoid(x):
    return 1.0 / (1.0 + jnp.exp(-x))


def _rms(x):
    return x * lax.rsqrt(jnp.mean(x * x, axis=-1, keepdims=True) + EPS)


def _ada_kernel(c_ref, w_ref, b_ref, o_ref):
    c = c_ref[...]
    s = (c * _sigmoid(c)).astype(BF16)
    o_ref[...] = _dot(s, w_ref[...].astype(BF16)) + b_ref[...]


def _ada(c_pad, w, b, tn=1024):
    d, n = w.shape
    return pl.pallas_call(
        _ada_kernel,
        out_shape=jax.ShapeDtypeStruct((ADA_ROWS, n), F32),
        grid=(n // tn,),
        in_specs=[pl.BlockSpec((ADA_ROWS, d), lambda j: (0, 0)),
                  pl.BlockSpec((d, tn), lambda j: (0, j)),
                  pl.BlockSpec((1, tn), lambda j: (0, j))],
        out_specs=pl.BlockSpec((ADA_ROWS, tn), lambda j: (0, j)),
        compiler_params=_params(1, 40),
        name="ada",
    )(c_pad, w, b)


def _norm_mod_kernel(x_ref, g_ref, mod_ref, o_ref, *, shift_row, scale_row):
    y = _rms(x_ref[0]) * g_ref[...]
    scale = mod_ref[0, scale_row:scale_row + 1, :]
    shift = mod_ref[0, shift_row:shift_row + 1, :]
    o_ref[0] = (y * (1.0 + scale) + shift).astype(o_ref.dtype)


def _norm_mod(x, g, mod3, shift_row, scale_row, ts=512):
    b, s, d = x.shape
    return pl.pallas_call(
        functools.partial(_norm_mod_kernel, shift_row=shift_row, scale_row=scale_row),
        out_shape=jax.ShapeDtypeStruct((b, s, d), BF16),
        grid=(b, s // ts),
        in_specs=[pl.BlockSpec((1, ts, d), lambda i, j: (i, j, 0)),
                  pl.BlockSpec((1, d), lambda i, j: (0, 0)),
                  pl.BlockSpec((1, N_MOD, d), lambda i, j: (i, 0, 0))],
        out_specs=pl.BlockSpec((1, ts, d), lambda i, j: (i, j, 0)),
        compiler_params=_params(2, 32),
        name="norm_mod",
    )(x, g, mod3)


def _mm_kernel(a_ref, b_ref, o_ref):
    o_ref[...] = _dot(a_ref[...], b_ref[...]).astype(o_ref.dtype)


def _matmul(a, b, tm, tn, name):
    m, k = a.shape
    n = b.shape[1]
    return pl.pallas_call(
        _mm_kernel,
        out_shape=jax.ShapeDtypeStruct((m, n), BF16),
        grid=(n // tn, m // tm),
        in_specs=[pl.BlockSpec((tm, k), lambda j, i: (i, 0)),
                  pl.BlockSpec((k, tn), lambda j, i: (0, j))],
        out_specs=pl.BlockSpec((tm, tn), lambda j, i: (i, j)),
        compiler_params=_params(2, 48),
        name=name,
    )(a, b)


def _sb_kernel(q_ref, k_ref, v_ref, tri_ref, o_ref, *, tq, bk, scale):
    seq = q_ref.shape[1]
    n_q = seq // tq
    n_diag = tq // bk
    tri = tri_ref[...]

    def block(q, i, j, acc, run, masked):
        start = pl.multiple_of(j * bk, bk)
        k = k_ref[0, pl.ds(start, bk), :]
        v = v_ref[0, pl.ds(start, bk), :]
        z = _dot_nt(q, k) * scale
        sp = jnp.maximum(z, 0.0) + jnp.log(1.0 + jnp.exp(-jnp.abs(z)))
        if masked:
            q_pos = i * tq + lax.broadcasted_iota(jnp.int32, (tq, bk), 0)
            k_pos = j * bk + lax.broadcasted_iota(jnp.int32, (tq, bk), 1)
            keep = k_pos < q_pos
            sp = jnp.where(keep, sp, 0.0)
        hi = sp.astype(BF16)
        lo = (sp - hi.astype(F32)).astype(BF16)
        suffix = _dot(hi, tri) + _dot(lo, tri)
        a = jnp.exp(z - (run + suffix))
        if masked:
            a = jnp.where(keep, a, 0.0)
        acc = acc + _dot(a.astype(BF16), v)
        run = run + suffix[:, 0:1]
        return acc, run

    def q_body(i, carry):
        row0 = pl.multiple_of(i * tq, tq)
        q = q_ref[0, pl.ds(row0, tq), :]
        acc = jnp.zeros((tq, HEAD_DIM), F32)
        run = jnp.zeros((tq, 1), F32)
        for t in range(n_diag):
            acc, run = block(q, i, (i + 1) * n_diag - 1 - t, acc, run, True)
        n_full = i * n_diag

        def kv_body(t, c):
            return block(q, i, n_full - 1 - t, c[0], c[1], False)

        acc, run = lax.fori_loop(0, n_full, kv_body, (acc, run))
        o_ref[0, pl.ds(row0, tq), :] = acc.astype(o_ref.dtype)
        return carry

    lax.fori_loop(0, n_q, q_body, 0)


def _sb_attention(qkv, tri, tq, bk):
    b, s, _ = qkv.shape
    blk = (1, s, HEAD_DIM)
    return pl.pallas_call(
        functools.partial(_sb_kernel, tq=tq, bk=bk, scale=HEAD_DIM ** -0.5),
        out_shape=jax.ShapeDtypeStruct((b, s, SB_W), BF16),
        grid=(b, SB_HEADS),
        in_specs=[pl.BlockSpec(blk, lambda i, h: (i, 0, h)),
                  pl.BlockSpec(blk, lambda i, h: (i, 0, SB_HEADS + h)),
                  pl.BlockSpec(blk, lambda i, h: (i, 0, 2 * SB_HEADS + h)),
                  pl.BlockSpec((bk, bk), lambda i, h: (0, 0))],
        out_specs=pl.BlockSpec(blk, lambda i, h: (i, 0, h)),
        compiler_params=_params(2, 48),
        name="sb_attn",
    )(qkv, qkv, qkv, tri)


def _dil_kernel(*refs, n_sub, has_prev, dilation, slopes, scale):
    if has_prev:
        q_ref, kc_ref, kp_ref, vc_ref, vp_ref, o_ref, lse_ref = refs
    else:
        q_ref, kc_ref, vc_ref, o_ref, lse_ref = refs
    n_keys = 2 * BLOCK if has_prev else BLOCK
    a_idx = lax.broadcasted_iota(jnp.int32, (BLOCK, n_keys), 0)
    b_idx = lax.broadcasted_iota(jnp.int32, (BLOCK, n_keys), 1)
    rel = a_idx + (n_keys - BLOCK) - b_idx
    in_window = (rel >= 0) & (rel <= BLOCK)
    dist = (dilation * rel).astype(F32)
    first_step = jnp.zeros((BLOCK, n_keys), jnp.int32) + pl.program_id(2)

    for j in range(n_sub):
        rows = slice(j * BLOCK, (j + 1) * BLOCK)
        valid = in_window
        if has_prev and j == 0:
            valid = in_window & ((b_idx >= BLOCK) | (first_step > 0))
        for h in range(DIL_HEADS_PER_GROUP):
            cols = slice(h * HEAD_DIM, (h + 1) * HEAD_DIM)
            q = q_ref[0, rows, cols]
            kk = kc_ref[0, rows, cols]
            vv = vc_ref[0, rows, cols]
            if has_prev:
                if j == 0:
                    k_prev, v_prev = kp_ref[0, :, cols], vp_ref[0, :, cols]
                else:
                    prev_rows = slice((j - 1) * BLOCK, j * BLOCK)
                    k_prev, v_prev = kc_ref[0, prev_rows, cols], vc_ref[0, prev_rows, cols]
                kk = jnp.concatenate([k_prev, kk], axis=0)
                vv = jnp.concatenate([v_prev, vv], axis=0)
            s = _dot_nt(q, kk) * scale - slopes[h] * dist
            s = jnp.where(valid, s, -jnp.inf)
            m = jnp.max(s, axis=-1, keepdims=True)
            p = jnp.exp(s - m)
            l = jnp.sum(p, axis=-1, keepdims=True)
            o = _dot(p.astype(BF16), vv) / l
            o_ref[0, rows, cols] = o.astype(o_ref.dtype)
            lse_ref[0, rows, cols] = jnp.broadcast_to(m + jnp.log(l), (BLOCK, HEAD_DIM))


def _dil_attention(qkv, group, slopes, n_sub):
    window, dilation = DIL_GROUPS[group]
    b, s, width = qkv.shape
    sub_len = s // dilation
    n_blk = sub_len // BLOCK
    n_sub = min(n_sub, n_blk)
    has_prev = n_blk > 1
    assert window // dilation == BLOCK and sub_len % BLOCK == 0 and n_blk % n_sub == 0
    cols_per_token = width // DIL_OUT
    view = qkv.reshape(b, sub_len, dilation * width)
    q_col, k_col, v_col = (part * len(DIL_GROUPS) + group for part in range(3))
    tile = (1, n_sub * BLOCK, DIL_OUT)

    def cur(col):
        return pl.BlockSpec(tile, lambda i, r, n: (i, n, r * cols_per_token + col))

    def prev(col):
        return pl.BlockSpec((1, BLOCK, DIL_OUT),
                            lambda i, r, n: (i, jnp.maximum(n * n_sub - 1, 0), r * cols_per_token + col))

    if has_prev:
        in_specs = [cur(q_col), cur(k_col), prev(k_col), cur(v_col), prev(v_col)]
        args = (view,) * 5
    else:
        in_specs = [cur(q_col), cur(k_col), cur(v_col)]
        args = (view,) * 3
    out_spec = pl.BlockSpec(tile, lambda i, r, n: (i, n, r))
    o, lse = pl.pallas_call(
        functools.partial(_dil_kernel, n_sub=n_sub, has_prev=has_prev, dilation=dilation,
                          slopes=slopes, scale=HEAD_DIM ** -0.5),
        out_shape=(jax.ShapeDtypeStruct((b, sub_len, dilation * DIL_OUT), BF16),
                   jax.ShapeDtypeStruct((b, sub_len, dilation * DIL_OUT), F32)),
        grid=(b, dilation, n_blk // n_sub),
        in_specs=in_specs,
        out_specs=(out_spec, out_spec),
        compiler_params=_params(3, 32),
        name=f"dil_attn_d{dilation}",
    )(*args)
    return o.reshape(b, s, DIL_OUT), lse.reshape(b, s, DIL_OUT)


def _post_kernel(osb_ref, o0_ref, o1_ref, o2_ref, l0_ref, l1_ref, l2_ref, gp_ref, bg_ref, x_ref,
                 mod_ref, g2_ref, wsb_ref, wdl_ref, wout_ref, x1_ref, h2_ref):
    d = x_ref.shape[-1]
    l0, l1, l2 = l0_ref[...], l1_ref[...], l2_ref[...]
    top = jnp.maximum(jnp.maximum(l0, l1), l2)
    e0, e1, e2 = jnp.exp(l0 - top), jnp.exp(l1 - top), jnp.exp(l2 - top)
    o_dl = (e0 * o0_ref[...].astype(F32) + e1 * o1_ref[...].astype(F32)
            + e2 * o2_ref[...].astype(F32)) / (e0 + e1 + e2)
    y_sb = _dot(osb_ref[...], wsb_ref[...])
    y_dl = _dot(o_dl.astype(BF16), wdl_ref[...])
    gates = _sigmoid(gp_ref[...].astype(F32) + bg_ref[...])
    mixed = gates[:, :d] * y_sb + gates[:, d:] * y_dl
    t = _dot(mixed.astype(BF16), wout_ref[...])
    x1 = x_ref[...] + mod_ref[0, 2:3, :] * t
    x1_ref[...] = x1
    h2 = _rms(x1) * g2_ref[...] * (1.0 + mod_ref[0, 4:5, :]) + mod_ref[0, 3:4, :]
    h2_ref[...] = h2.astype(h2_ref.dtype)


def _post(o_sb, o_dl, lse, gate_pre, b_gate, x2d, mod3, g2, w_sb, w_dl, w_out, seq, tm=256):
    m, d = x2d.shape

    def row(width):
        return pl.BlockSpec((tm, width), lambda i: (i, 0))

    def whole(arr):
        return pl.BlockSpec(arr.shape, lambda i: (0,) * arr.ndim, pipeline_mode=pl.Buffered(1))

    return pl.pallas_call(
        _post_kernel,
        out_shape=(jax.ShapeDtypeStruct((m, d), F32), jax.ShapeDtypeStruct((m, d), BF16)),
        grid=(m // tm,),
        in_specs=[row(SB_W)] + [row(DIL_OUT)] * 6 + [row(2 * d), whole(b_gate), row(d),
                  pl.BlockSpec((1, N_MOD, d), lambda i: ((i * tm) // seq, 0, 0)),
                  whole(g2), whole(w_sb), whole(w_dl), whole(w_out)],
        out_specs=(row(d), row(d)),
        compiler_params=_params(1, 56),
        name="post",
    )(o_sb, *o_dl, *lse, gate_pre, b_gate, x2d, mod3, g2, w_sb, w_dl, w_out)


def _swiglu_kernel(h_ref, wg_ref, wu_ref, o_ref):
    h = h_ref[...]
    g = _dot(h, wg_ref[...])
    u = _dot(h, wu_ref[...])
    o_ref[...] = (g * _sigmoid(g) * u).astype(o_ref.dtype)


def _swiglu(h, wg, wu, tm=2048, tn=512):
    m, k = h.shape
    n = wg.shape[1]
    w_spec = pl.BlockSpec((k, tn), lambda j, i: (0, j))
    return pl.pallas_call(
        _swiglu_kernel,
        out_shape=jax.ShapeDtypeStruct((m, n), BF16),
        grid=(n // tn, m // tm),
        in_specs=[pl.BlockSpec((tm, k), lambda j, i: (i, 0)), w_spec, w_spec],
        out_specs=pl.BlockSpec((tm, tn), lambda j, i: (i, j)),
        compiler_params=_params(2, 56),
        name="swiglu",
    )(h, wg, wu)


def _down_kernel(a_ref, w_ref, x1_ref, mod_ref, gf_ref, o_ref, acc_ref, *, final_norm):
    k = pl.program_id(1)

    @pl.when(k == 0)
    def _():
        acc_ref[...] = jnp.zeros_like(acc_ref)

    acc_ref[...] += _dot(a_ref[...], w_ref[...])

    @pl.when(k == pl.num_programs(1) - 1)
    def _():
        x2 = x1_ref[...] + mod_ref[0, 5:6, :] * acc_ref[...]
        o_ref[...] = _rms(x2) * gf_ref[...] if final_norm else x2


def _down(a, w, x1, mod3, g_final, seq, final_norm, tm=512, tk=1408):
    m, f = a.shape
    d = w.shape[1]
    return pl.pallas_call(
        functools.partial(_down_kernel, final_norm=final_norm),
        out_shape=jax.ShapeDtypeStruct((m, d), F32),
        grid=(m // tm, f // tk),
        in_specs=[pl.BlockSpec((tm, tk), lambda i, k: (i, k)),
                  pl.BlockSpec((tk, d), lambda i, k: (k, 0)),
                  pl.BlockSpec((tm, d), lambda i, k: (i, 0)),
                  pl.BlockSpec((1, N_MOD, d), lambda i, k: ((i * tm) // seq, 0, 0)),
                  pl.BlockSpec((1, d), lambda i, k: (0, 0))],
        out_specs=pl.BlockSpec((tm, d), lambda i, k: (i, 0)),
        scratch_shapes=[pltpu.VMEM((tm, d), F32)],
        compiler_params=_params(2, 48),
        name="down",
    )(a, w, x1, mod3, g_final)


def _alibi_slopes():
    return tuple(2.0 ** (-8.0 * (i + 1.0) / DIL_HEADS) for i in range(DIL_HEADS))


def kernel(x, c, w_ada, b_ada, g_norm1, g_norm2, g_final, w_in, b_gate, w_proj_sb, w_proj_dil,
           w_out, w_ffn_gate, w_ffn_up, w_ffn_down):
    batch, seq, d = x.shape
    tokens = batch * seq
    slopes = _alibi_slopes()
    sb_tile = 256
    idx = jnp.arange(sb_tile)
    tri = (idx[:, None] >= idx[None, :]).astype(BF16)
    c_pad = jnp.pad(c, ((0, ADA_ROWS - batch), (0, 0)))
    qkv_sb_end = 3 * SB_W
    qkv_dl_end = qkv_sb_end + 3 * DIL_W

    for l in range(w_ada.shape[0]):
        mod3 = _ada(c_pad, w_ada[l], b_ada[l][None, :]).reshape(ADA_ROWS, N_MOD, d)

        h = _norm_mod(x, g_norm1[l][None, :], mod3, shift_row=0, scale_row=1).reshape(tokens, d)
        w_l = w_in[l]
        qkv_sb = _matmul(h, w_l[:, :qkv_sb_end].astype(BF16), 1024, 1024, "proj_sb")
        qkv_dl = _matmul(h, w_l[:, qkv_sb_end:qkv_dl_end].astype(BF16), 1024, 1536, "proj_dil")
        gate_pre = _matmul(h, w_l[:, qkv_dl_end:].astype(BF16), 1024, 1024, "proj_gate")

        o_sb = _sb_attention(qkv_sb.reshape(batch, seq, qkv_sb_end), tri, sb_tile, sb_tile)

        qkv_dl = qkv_dl.reshape(batch, seq, 3 * DIL_W)
        o_dl, lse = [], []
        for g in range(len(DIL_GROUPS)):
            o_g, lse_g = _dil_attention(qkv_dl, g, slopes[g * DIL_HEADS_PER_GROUP:(g + 1) * DIL_HEADS_PER_GROUP],
                                        n_sub=4)
            o_dl.append(o_g.reshape(tokens, DIL_OUT))
            lse.append(lse_g.reshape(tokens, DIL_OUT))

        x1, h2 = _post(o_sb.reshape(tokens, SB_W), o_dl, lse, gate_pre, b_gate[l][None, :],
                       x.reshape(tokens, d), mod3, g_norm2[l][None, :], w_proj_sb[l].astype(BF16),
                       w_proj_dil[l].astype(BF16), w_out[l].astype(BF16), seq)

        a = _swiglu(h2, w_ffn_gate[l].astype(BF16), w_ffn_up[l].astype(BF16))
        x = _down(a, w_ffn_down[l].astype(BF16), x1, mod3, g_final[None, :], seq,
                  final_norm=(l == w_ada.shape[0] - 1)).reshape(batch, seq, d)
    return x
```

```python
import functools

import jax
import jax.numpy as jnp
from jax import lax
from jax.experimental import pallas as pl
from jax.experimental.pallas import tpu as pltpu

F32 = jnp.float32
BF16 = jnp.bfloat16

HEAD_DIM = 128
SB_HEADS = 8
DIL_GROUPS = ((128, 1), (512, 4), (2048, 16))
DIL_HEADS_PER_GROUP = 4
DIL_HEADS = DIL_HEADS_PER_GROUP * len(DIL_GROUPS)
BLOCK = 128
SB_W = SB_HEADS * HEAD_DIM
DIL_W = DIL_HEADS * HEAD_DIM
DIL_OUT = DIL_HEADS_PER_GROUP * HEAD_DIM
N_MOD = 6
EPS = 1e-6

LANES = 128
MIB = 1024 * 1024
ADA_ROWS = 8
W_COL = 512


def _params(n_axes, vmem_mib):
    return pltpu.CompilerParams(dimension_semantics=("arbitrary",) * n_axes,
                                vmem_limit_bytes=vmem_mib * MIB)


def _dot(a, b):
    return jnp.dot(a, b, preferred_element_type=F32)


def _dot_nt(a, b):
    return lax.dot_general(a, b, (((1,), (1,)), ((), ())), preferred_element_type=F32)


def _sigmoid(x):
    return 1.0 / (1.0 + jnp.exp(-x))


def _rms(x):
    return x * lax.rsqrt(jnp.mean(x * x, axis=-1, keepdims=True) + EPS)


def _ada_kernel(c_ref, w_ref, b_ref, o_ref):
    c = c_ref[...]
    s = (c * _sigmoid(c)).astype(BF16)
    o_ref[...] = _dot(s, w_ref[...].astype(BF16)) + b_ref[...]


def _ada(c_pad, w, b, tn=1024):
    d, n = w.shape
    return pl.pallas_call(
        _ada_kernel,
        out_shape=jax.ShapeDtypeStruct((ADA_ROWS, n), F32),
        grid=(n // tn,),
        in_specs=[pl.BlockSpec((ADA_ROWS, d), lambda j: (0, 0)),
                  pl.BlockSpec((d, tn), lambda j: (0, j)),
                  pl.BlockSpec((1, tn), lambda j: (0, j))],
        out_specs=pl.BlockSpec((ADA_ROWS, tn), lambda j: (0, j)),
        compiler_params=_params(1, 40),
        name="ada",
    )(c_pad, w, b)


def _norm1_kernel(x_ref, g_ref, mod_ref, o_ref, *rest, dils):
    perm_refs, slab_ref = rest[:-1], rest[-1]
    ts, d_model = x_ref.shape[1], x_ref.shape[2]
    y = _rms(x_ref[0]) * g_ref[...]
    h = y * (1.0 + mod_ref[0, 1:2, :]) + mod_ref[0, 0:1, :]
    o_ref[0] = h.astype(o_ref.dtype)
    n_slab = d_model // LANES
    for s in range(n_slab):
        slab_ref[s] = h[:, s * LANES:(s + 1) * LANES]
    for dil, p_ref in zip(dils, perm_refs):
        for r in range(dil):
            for s in range(n_slab):
                rows = slab_ref[s, pl.ds(r, ts // dil, stride=dil), :]
                p_ref[0, r, :, s * LANES:(s + 1) * LANES] = rows.astype(p_ref.dtype)


def _norm1(x, g, mod3, dils, ts=512):
    b, s, d = x.shape
    out_shape = [jax.ShapeDtypeStruct((b, s, d), BF16)]
    out_specs = [pl.BlockSpec((1, ts, d), lambda i, j: (i, j, 0))]
    for dil in dils:
        out_shape.append(jax.ShapeDtypeStruct((b, dil, s // dil, d), BF16))
        out_specs.append(pl.BlockSpec((1, dil, ts // dil, d), lambda i, j: (i, 0, j, 0)))
    return pl.pallas_call(
        functools.partial(_norm1_kernel, dils=dils),
        out_shape=out_shape,
        grid=(b, s // ts),
        in_specs=[pl.BlockSpec((1, ts, d), lambda i, j: (i, j, 0)),
                  pl.BlockSpec((1, d), lambda i, j: (0, 0)),
                  pl.BlockSpec((1, N_MOD, d), lambda i, j: (i, 0, 0))],
        out_specs=out_specs,
        scratch_shapes=[pltpu.VMEM((d // LANES, ts, LANES), F32)],
        compiler_params=_params(2, 40),
        name="norm1",
    )(x, g, mod3)


def _proj_kernel(*refs, n_rhs):
    h_ref, w_refs, o_ref, wb_ref = refs[0], refs[1:1 + n_rhs], refs[1 + n_rhs], refs[2 + n_rhs]

    @pl.when(pl.program_id(1) == 0)
    def _():
        for r, w_ref in enumerate(w_refs):
            wb_ref[:, r * W_COL:(r + 1) * W_COL] = w_ref[...].astype(BF16)

    o_ref[...] = _dot(h_ref[...], wb_ref[...]).astype(o_ref.dtype)


def _proj(h, w_in, layer, col_fn, n_tiles, n_rhs, name, tm=1024):
    m, k = h.shape
    tn = n_rhs * W_COL
    mode = pl.Buffered(1) if n_tiles == 1 else None

    def w_spec(r):
        return pl.BlockSpec((None, k, W_COL), lambda j, i: (layer, 0, col_fn(j, r)), pipeline_mode=mode)

    return pl.pallas_call(
        functools.partial(_proj_kernel, n_rhs=n_rhs),
        out_shape=jax.ShapeDtypeStruct((m, n_tiles * tn), BF16),
        grid=(n_tiles, m // tm),
        in_specs=[pl.BlockSpec((tm, k), lambda j, i: (i, 0))] + [w_spec(r) for r in range(n_rhs)],
        out_specs=pl.BlockSpec((tm, tn), lambda j, i: (i, j)),
        scratch_shapes=[pltpu.VMEM((k, tn), BF16)],
        compiler_params=_params(2, 48),
        name=name,
    )(h, *([w_in] * n_rhs))


def _sb_kernel(q_ref, k_ref, v_ref, tri_ref, o_ref, *, tq, bk, scale):
    seq = q_ref.shape[1]
    tri = tri_ref[...]
    row = lax.broadcasted_iota(jnp.int32, (tq, bk), 0)
    col = lax.broadcasted_iota(jnp.int32, (tq, bk), 1)
    for i in range(seq // tq):
        n_keys = (i + 1) * tq
        q = q_ref[0, i * tq:(i + 1) * tq, :]
        z_all = _dot_nt(q, k_ref[0, :n_keys, :]) * scale
        run = jnp.zeros((tq, 1), F32)
        weights = [None] * (n_keys // bk)
        for jb in reversed(range(n_keys // bk)):
            z = z_all[:, jb * bk:(jb + 1) * bk]
            sp = jnp.maximum(z, 0.0) + jnp.log(1.0 + jnp.exp(-jnp.abs(z)))
            on_diagonal = (jb + 1) * bk > i * tq
            if on_diagonal:
                keep = col + (jb * bk - i * tq) < row
                sp = jnp.where(keep, sp, 0.0)
            hi = sp.astype(BF16)
            lo = (sp - hi.astype(F32)).astype(BF16)
            suffix = _dot(hi, tri) + _dot(lo, tri)
            a = jnp.exp(z - (run + suffix))
            if on_diagonal:
                a = jnp.where(keep, a, 0.0)
            weights[jb] = a.astype(BF16)
            run = run + suffix[:, 0:1]
        a_all = weights[0] if len(weights) == 1 else jnp.concatenate(weights, axis=1)
        o_ref[0, i * tq:(i + 1) * tq, :] = _dot(a_all, v_ref[0, :n_keys, :]).astype(o_ref.dtype)


def _sb_attention(qkv, tri, tq, bk):
    b, s, _ = qkv.shape
    blk = (1, s, HEAD_DIM)
    return pl.pallas_call(
        functools.partial(_sb_kernel, tq=tq, bk=bk, scale=HEAD_DIM ** -0.5),
        out_shape=jax.ShapeDtypeStruct((b, s, SB_W), BF16),
        grid=(b, SB_HEADS),
        in_specs=[pl.BlockSpec(blk, lambda i, h: (i, 0, h)),
                  pl.BlockSpec(blk, lambda i, h: (i, 0, SB_HEADS + h)),
                  pl.BlockSpec(blk, lambda i, h: (i, 0, 2 * SB_HEADS + h)),
                  pl.BlockSpec((bk, bk), lambda i, h: (0, 0))],
        out_specs=pl.BlockSpec(blk, lambda i, h: (i, 0, h)),
        compiler_params=_params(2, 48),
        name="sb_attn",
    )(qkv, qkv, qkv, tri)


def _dil_kernel(*refs, n_sub, has_prev, dilation, slopes, scale):
    if has_prev:
        q_ref, kc_ref, kp_ref, vc_ref, vp_ref, o_ref, lse_ref, o_slab, lse_slab = refs
    else:
        q_ref, kc_ref, vc_ref, o_ref, lse_ref, o_slab, lse_slab = refs
    n_keys = 2 * BLOCK if has_prev else BLOCK
    a_idx = lax.broadcasted_iota(jnp.int32, (BLOCK, n_keys), 0)
    b_idx = lax.broadcasted_iota(jnp.int32, (BLOCK, n_keys), 1)
    rel = a_idx + (n_keys - BLOCK) - b_idx
    in_window = (rel >= 0) & (rel <= BLOCK)
    dist = (dilation * rel).astype(F32)
    step = jnp.zeros((BLOCK, n_keys), jnp.int32) + pl.program_id(1)

    for r in range(dilation):
        for j in range(n_sub):
            rows = slice(j * BLOCK, (j + 1) * BLOCK)
            valid = in_window
            if has_prev and j == 0:
                valid = in_window & ((b_idx >= BLOCK) | (step > 0))
            out_rows = pl.ds(j * BLOCK * dilation + r, BLOCK, stride=dilation)
            for h in range(DIL_HEADS_PER_GROUP):
                cols = slice(h * HEAD_DIM, (h + 1) * HEAD_DIM)
                q = q_ref[0, r, rows, cols]
                kk = kc_ref[0, r, rows, cols]
                vv = vc_ref[0, r, rows, cols]
                if has_prev:
                    if j == 0:
                        k_prev, v_prev = kp_ref[0, r, :, cols], vp_ref[0, r, :, cols]
                    else:
                        prev_rows = slice((j - 1) * BLOCK, j * BLOCK)
                        k_prev, v_prev = kc_ref[0, r, prev_rows, cols], vc_ref[0, r, prev_rows, cols]
                    kk = jnp.concatenate([k_prev, kk], axis=0)
                    vv = jnp.concatenate([v_prev, vv], axis=0)
                s = _dot_nt(q, kk) * scale - slopes[h] * dist
                s = jnp.where(valid, s, -jnp.inf)
                m = jnp.max(s, axis=-1, keepdims=True)
                p = jnp.exp(s - m)
                l = jnp.sum(p, axis=-1, keepdims=True)
                o_slab[h, out_rows, :] = _dot(p.astype(BF16), vv) / l
                lse_slab[h, out_rows, :] = jnp.broadcast_to(m + jnp.log(l), (BLOCK, HEAD_DIM))

    for h in range(DIL_HEADS_PER_GROUP):
        cols = slice(h * HEAD_DIM, (h + 1) * HEAD_DIM)
        o_ref[0, :, cols] = o_slab[h].astype(o_ref.dtype)
        lse_ref[0, :, cols] = lse_slab[h]


def _dil_attention(qkv, batch, group, slopes, n_sub):
    window, dilation = DIL_GROUPS[group]
    s = qkv.shape[0] // batch
    sub_len = s // dilation
    n_blk = sub_len // BLOCK
    n_sub = min(n_sub, n_blk)
    has_prev = n_blk > 1
    assert window // dilation == BLOCK and sub_len % BLOCK == 0 and n_blk % n_sub == 0
    view = qkv.reshape(batch, dilation, sub_len, 3 * DIL_OUT)
    rows = n_sub * BLOCK

    def cur(col):
        return pl.BlockSpec((1, dilation, rows, DIL_OUT), lambda i, n: (i, 0, n, col))

    def prev(col):
        return pl.BlockSpec((1, dilation, BLOCK, DIL_OUT),
                            lambda i, n: (i, 0, jnp.maximum(n * n_sub - 1, 0), col))

    if has_prev:
        in_specs = [cur(0), cur(1), prev(1), cur(2), prev(2)]
    else:
        in_specs = [cur(0), cur(1), cur(2)]
    out_spec = pl.BlockSpec((1, dilation * rows, DIL_OUT), lambda i, n: (i, n, 0))
    slab = pltpu.VMEM((DIL_HEADS_PER_GROUP, dilation * rows, HEAD_DIM), F32)
    return pl.pallas_call(
        functools.partial(_dil_kernel, n_sub=n_sub, has_prev=has_prev, dilation=dilation,
                          slopes=slopes, scale=HEAD_DIM ** -0.5),
        out_shape=(jax.ShapeDtypeStruct((batch, s, DIL_OUT), BF16),
                   jax.ShapeDtypeStruct((batch, s, DIL_OUT), F32)),
        grid=(batch, n_blk // n_sub),
        in_specs=in_specs,
        out_specs=(out_spec, out_spec),
        scratch_shapes=[slab, slab],
        compiler_params=_params(2, 40),
        name=f"dil_attn_d{dilation}",
    )(*([view] * len(in_specs)))


def _post_kernel(osb_ref, o0_ref, o1_ref, o2_ref, l0_ref, l1_ref, l2_ref, gp_ref, bg_ref, x_ref,
                 mod_ref, g2_ref, wsb_ref, wdl_ref, wout_ref, x1_ref, h2_ref):
    d = x_ref.shape[-1]
    l0, l1, l2 = l0_ref[...], l1_ref[...], l2_ref[...]
    top = jnp.maximum(jnp.maximum(l0, l1), l2)
    e0, e1, e2 = jnp.exp(l0 - top), jnp.exp(l1 - top), jnp.exp(l2 - top)
    o_dl = (e0 * o0_ref[...].astype(F32) + e1 * o1_ref[...].astype(F32)
            + e2 * o2_ref[...].astype(F32)) / (e0 + e1 + e2)
    y_sb = _dot(osb_ref[...], wsb_ref[...])
    y_dl = _dot(o_dl.astype(BF16), wdl_ref[...])
    gates = _sigmoid(gp_ref[...].astype(F32) + bg_ref[...])
    mixed = gates[:, :d] * y_sb + gates[:, d:] * y_dl
    t = _dot(mixed.astype(BF16), wout_ref[...])
    x1 = x_ref[...] + mod_ref[0, 2:3, :] * t
    x1_ref[...] = x1
    h2 = _rms(x1) * g2_ref[...] * (1.0 + mod_ref[0, 4:5, :]) + mod_ref[0, 3:4, :]
    h2_ref[...] = h2.astype(h2_ref.dtype)


def _post(o_sb, o_dl, lse, gate_pre, b_gate, x2d, mod3, g2, w_sb, w_dl, w_out, seq, tm=256):
    m, d = x2d.shape

    def row(width):
        return pl.BlockSpec((tm, width), lambda i: (i, 0))

    def whole(arr):
        return pl.BlockSpec(arr.shape, lambda i: (0,) * arr.ndim, pipeline_mode=pl.Buffered(1))

    return pl.pallas_call(
        _post_kernel,
        out_shape=(jax.ShapeDtypeStruct((m, d), F32), jax.ShapeDtypeStruct((m, d), BF16)),
        grid=(m // tm,),
        in_specs=[row(SB_W)] + [row(DIL_OUT)] * 6 + [row(2 * d), whole(b_gate), row(d),
                  pl.BlockSpec((1, N_MOD, d), lambda i: ((i * tm) // seq, 0, 0)),
                  whole(g2), whole(w_sb), whole(w_dl), whole(w_out)],
        out_specs=(row(d), row(d)),
        compiler_params=_params(1, 56),
        name="post",
    )(o_sb, *o_dl, *lse, gate_pre, b_gate, x2d, mod3, g2, w_sb, w_dl, w_out)


def _swiglu_kernel(h_ref, wg_ref, wu_ref, o_ref, wgb_ref, wub_ref):
    @pl.when(pl.program_id(1) == 0)
    def _():
        wgb_ref[...] = wg_ref[...].astype(BF16)
        wub_ref[...] = wu_ref[...].astype(BF16)

    h = h_ref[...]
    g = _dot(h, wgb_ref[...])
    u = _dot(h, wub_ref[...])
    o_ref[...] = (g * _sigmoid(g) * u).astype(o_ref.dtype)


def _swiglu(h, wg, wu, layer, tm=2048, tn=512):
    m, k = h.shape
    n = wg.shape[-1]
    w_spec = pl.BlockSpec((None, k, tn), lambda j, i: (layer, 0, j))
    return pl.pallas_call(
        _swiglu_kernel,
        out_shape=jax.ShapeDtypeStruct((m, n), BF16),
        grid=(n // tn, m // tm),
        in_specs=[pl.BlockSpec((tm, k), lambda j, i: (i, 0)), w_spec, w_spec],
        out_specs=pl.BlockSpec((tm, tn), lambda j, i: (i, j)),
        scratch_shapes=[pltpu.VMEM((k, tn), BF16), pltpu.VMEM((k, tn), BF16)],
        compiler_params=_params(2, 60),
        name="swiglu",
    )(h, wg, wu)


def _down_kernel(a_ref, w_ref, x1_ref, mod_ref, gf_ref, o_ref, acc_ref, *, final_norm):
    k = pl.program_id(1)

    @pl.when(k == 0)
    def _():
        acc_ref[...] = jnp.zeros_like(acc_ref)

    acc_ref[...] += _dot(a_ref[...], w_ref[...])

    @pl.when(k == pl.num_programs(1) - 1)
    def _():
        x2 = x1_ref[...] + mod_ref[0, 5:6, :] * acc_ref[...]
        o_ref[...] = _rms(x2) * gf_ref[...] if final_norm else x2


def _down(a, w, x1, mod3, g_final, seq, final_norm, tm=512, tk=1408):
    m, f = a.shape
    d = w.shape[1]
    return pl.pallas_call(
        functools.partial(_down_kernel, final_norm=final_norm),
        out_shape=jax.ShapeDtypeStruct((m, d), F32),
        grid=(m // tm, f // tk),
        in_specs=[pl.BlockSpec((tm, tk), lambda i, k: (i, k)),
                  pl.BlockSpec((tk, d), lambda i, k: (k, 0)),
                  pl.BlockSpec((tm, d), lambda i, k: (i, 0)),
                  pl.BlockSpec((1, N_MOD, d), lambda i, k: ((i * tm) // seq, 0, 0)),
                  pl.BlockSpec((1, d), lambda i, k: (0, 0))],
        out_specs=pl.BlockSpec((tm, d), lambda i, k: (i, 0)),
        scratch_shapes=[pltpu.VMEM((tm, d), F32)],
        compiler_params=_params(2, 48),
        name="down",
    )(a, w, x1, mod3, g_final)


def _alibi_slopes():
    return tuple(2.0 ** (-8.0 * (i + 1.0) / DIL_HEADS) for i in range(DIL_HEADS))


def kernel(x, c, w_ada, b_ada, g_norm1, g_norm2, g_final, w_in, b_gate, w_proj_sb, w_proj_dil,
           w_out, w_ffn_gate, w_ffn_up, w_ffn_down):
    batch, seq, d = x.shape
    tokens = batch * seq
    n_layers = w_ada.shape[0]
    slopes = _alibi_slopes()
    n_groups = len(DIL_GROUPS)
    sb_tile = 256
    idx = jnp.arange(sb_tile)
    tri = (idx[:, None] >= idx[None, :]).astype(BF16)
    c_pad = jnp.pad(c, ((0, ADA_ROWS - batch), (0, 0)))
    perm_dils = tuple(dil for _, dil in DIL_GROUPS if dil > 1)
    dl_col0 = 3 * SB_W // W_COL
    gate_col0 = dl_col0 + 3 * DIL_W // W_COL
    blocks_per_group = DIL_OUT // W_COL
    assert blocks_per_group == 1

    for l in range(n_layers):
        mod3 = _ada(c_pad, w_ada[l], b_ada[l][None, :]).reshape(ADA_ROWS, N_MOD, d)

        h_all = _norm1(x, g_norm1[l][None, :], mod3, perm_dils)
        h = h_all[0].reshape(tokens, d)
        h_perm = dict(zip(perm_dils, (hp.reshape(tokens, d) for hp in h_all[1:])))

        qkv_sb = _proj(h, w_in, l, lambda j, r: 2 * j + r, 3 * SB_W // (2 * W_COL), 2, "proj_sb")
        gate_pre = _proj(h, w_in, l, lambda j, r: gate_col0 + 2 * j + r, 2 * d // (2 * W_COL), 2, "proj_gate")

        o_sb = _sb_attention(qkv_sb.reshape(batch, seq, 3 * SB_W), tri, sb_tile, sb_tile)

        o_dl, lse = [], []
        for g, (_, dil) in enumerate(DIL_GROUPS):
            qkv_g = _proj(h_perm.get(dil, h), w_in, l,
                          lambda j, r, g=g: dl_col0 + r * n_groups + g, 1, 3, f"proj_dil_d{dil}")
            o_g, lse_g = _dil_attention(qkv_g, batch, g,
                                        slopes[g * DIL_HEADS_PER_GROUP:(g + 1) * DIL_HEADS_PER_GROUP], n_sub=4)
            o_dl.append(o_g.reshape(tokens, DIL_OUT))
            lse.append(lse_g.reshape(tokens, DIL_OUT))

        x1, h2 = _post(o_sb.reshape(tokens, SB_W), o_dl, lse, gate_pre, b_gate[l][None, :],
                       x.reshape(tokens, d), mod3, g_norm2[l][None, :], w_proj_sb[l].astype(BF16),
                       w_proj_dil[l].astype(BF16), w_out[l].astype(BF16), seq)

        a = _swiglu(h2, w_ffn_gate, w_ffn_up, l)
        x = _down(a, w_ffn_down[l].astype(BF16), x1, mod3, g_final[None, :], seq,
                  final_norm=(l == n_layers - 1)).reshape(batch, seq, d)
    return x
```

```python
import functools

import jax
import jax.numpy as jnp
from jax import lax
from jax.experimental import pallas as pl
from jax.experimental.pallas import tpu as pltpu

F32 = jnp.float32
BF16 = jnp.bfloat16

HEAD_DIM = 128
SB_HEADS = 8
DIL_GROUPS = ((128, 1), (512, 4), (2048, 16))
DIL_HEADS_PER_GROUP = 4
DIL_HEADS = DIL_HEADS_PER_GROUP * len(DIL_GROUPS)
BLOCK = 128
SB_W = SB_HEADS * HEAD_DIM
DIL_W = DIL_HEADS * HEAD_DIM
DIL_OUT = DIL_HEADS_PER_GROUP * HEAD_DIM
N_MOD = 6
EPS = 1e-6

LANES = 128
MIB = 1024 * 1024
ADA_ROWS = 8
W_COL = 512
SUB_ROWS = 256
DIL_CHAINS = 8


def _params(n_axes, vmem_mib):
    return pltpu.CompilerParams(dimension_semantics=("arbitrary",) * n_axes,
                                vmem_limit_bytes=vmem_mib * MIB)


def _dot(a, b):
    return jnp.dot(a, b, preferred_element_type=F32)


def _dot_nt(a, b):
    return lax.dot_general(a, b, (((1,), (1,)), ((), ())), preferred_element_type=F32)


def _sigmoid(x):
    return 1.0 / (1.0 + jnp.exp(-x))


def _rms(x):
    return x * lax.rsqrt(jnp.mean(x * x, axis=-1, keepdims=True) + EPS)


def _row_chunks(rows):
    return [slice(r, r + SUB_ROWS) for r in range(0, rows, SUB_ROWS)]


def _ada_kernel(c_ref, w_ref, b_ref, o_ref):
    c = c_ref[...]
    s = (c * _sigmoid(c)).astype(BF16)
    o_ref[...] = _dot(s, w_ref[...].astype(BF16)) + b_ref[...]


def _ada(c_pad, w, b, tn=1024):
    d, n = w.shape
    return pl.pallas_call(
        _ada_kernel,
        out_shape=jax.ShapeDtypeStruct((ADA_ROWS, n), F32),
        grid=(n // tn,),
        in_specs=[pl.BlockSpec((ADA_ROWS, d), lambda j: (0, 0)),
                  pl.BlockSpec((d, tn), lambda j: (0, j)),
                  pl.BlockSpec((1, tn), lambda j: (0, j))],
        out_specs=pl.BlockSpec((ADA_ROWS, tn), lambda j: (0, j)),
        compiler_params=_params(1, 40),
        name="ada",
    )(c_pad, w, b)


def _norm1_kernel(x_ref, g_ref, mod_ref, o_ref, *rest, dils):
    perm_refs, slab_ref = rest[:-1], rest[-1]
    ts, d_model = x_ref.shape[1], x_ref.shape[2]
    y = _rms(x_ref[0]) * g_ref[...]
    h = y * (1.0 + mod_ref[0, 1:2, :]) + mod_ref[0, 0:1, :]
    o_ref[0] = h.astype(o_ref.dtype)
    n_slab = d_model // LANES
    for s in range(n_slab):
        slab_ref[s] = h[:, s * LANES:(s + 1) * LANES]
    for dil, p_ref in zip(dils, perm_refs):
        for r in range(dil):
            for s in range(n_slab):
                rows = slab_ref[s, pl.ds(r, ts // dil, stride=dil), :]
                p_ref[0, r, :, s * LANES:(s + 1) * LANES] = rows.astype(p_ref.dtype)


def _norm1(x, g, mod3, dils, ts=512):
    b, s, d = x.shape
    out_shape = [jax.ShapeDtypeStruct((b, s, d), BF16)]
    out_specs = [pl.BlockSpec((1, ts, d), lambda i, j: (i, j, 0))]
    for dil in dils:
        out_shape.append(jax.ShapeDtypeStruct((b, dil, s // dil, d), BF16))
        out_specs.append(pl.BlockSpec((1, dil, ts // dil, d), lambda i, j: (i, 0, j, 0)))
    return pl.pallas_call(
        functools.partial(_norm1_kernel, dils=dils),
        out_shape=out_shape,
        grid=(b, s // ts),
        in_specs=[pl.BlockSpec((1, ts, d), lambda i, j: (i, j, 0)),
                  pl.BlockSpec((1, d), lambda i, j: (0, 0)),
                  pl.BlockSpec((1, N_MOD, d), lambda i, j: (i, 0, 0))],
        out_specs=out_specs,
        scratch_shapes=[pltpu.VMEM((d // LANES, ts, LANES), F32)],
        compiler_params=_params(2, 40),
        name="norm1",
    )(x, g, mod3)


def _proj_kernel(*refs, n_rhs):
    h_ref, w_refs, o_ref, wb_ref = refs[0], refs[1:1 + n_rhs], refs[1 + n_rhs], refs[2 + n_rhs]

    @pl.when(pl.program_id(1) == 0)
    def _():
        for r, w_ref in enumerate(w_refs):
            wb_ref[:, r * W_COL:(r + 1) * W_COL] = w_ref[...].astype(BF16)

    o_ref[...] = _dot(h_ref[...], wb_ref[...]).astype(o_ref.dtype)


def _proj(h, w_in, layer, col_fn, n_tiles, n_rhs, name, tm=1024):
    m, k = h.shape
    tn = n_rhs * W_COL
    mode = pl.Buffered(1) if n_tiles == 1 else None

    def w_spec(r):
        return pl.BlockSpec((None, k, W_COL), lambda j, i: (layer, 0, col_fn(j, r)), pipeline_mode=mode)

    return pl.pallas_call(
        functools.partial(_proj_kernel, n_rhs=n_rhs),
        out_shape=jax.ShapeDtypeStruct((m, n_tiles * tn), BF16),
        grid=(n_tiles, m // tm),
        in_specs=[pl.BlockSpec((tm, k), lambda j, i: (i, 0))] + [w_spec(r) for r in range(n_rhs)],
        out_specs=pl.BlockSpec((tm, tn), lambda j, i: (i, j)),
        scratch_shapes=[pltpu.VMEM((k, tn), BF16)],
        compiler_params=_params(2, 56),
        name=name,
    )(h, *([w_in] * n_rhs))


def _sb_kernel(q_ref, k_ref, v_ref, tri_ref, o_ref, *, tq, bk, scale):
    seq = q_ref.shape[1]
    tri = tri_ref[...]
    row = lax.broadcasted_iota(jnp.int32, (tq, bk), 0)
    col = lax.broadcasted_iota(jnp.int32, (tq, bk), 1)
    for i in range(seq // tq):
        n_keys = (i + 1) * tq
        q = q_ref[0, i * tq:(i + 1) * tq, :]
        z_all = _dot_nt(q, k_ref[0, :n_keys, :]) * scale
        run = jnp.zeros((tq, 1), F32)
        weights = [None] * (n_keys // bk)
        for jb in reversed(range(n_keys // bk)):
            z = z_all[:, jb * bk:(jb + 1) * bk]
            sp = jnp.maximum(z, 0.0) + jnp.log(1.0 + jnp.exp(-jnp.abs(z)))
            on_diagonal = (jb + 1) * bk > i * tq
            if on_diagonal:
                keep = col + (jb * bk - i * tq) < row
                sp = jnp.where(keep, sp, 0.0)
            hi = sp.astype(BF16)
            lo = (sp - hi.astype(F32)).astype(BF16)
            suffix = _dot(hi, tri) + _dot(lo, tri)
            a = jnp.exp(z - (run + suffix))
            if on_diagonal:
                a = jnp.where(keep, a, 0.0)
            weights[jb] = a.astype(BF16)
            run = run + suffix[:, 0:1]
        a_all = weights[0] if len(weights) == 1 else jnp.concatenate(weights, axis=1)
        o_ref[0, i * tq:(i + 1) * tq, :] = _dot(a_all, v_ref[0, :n_keys, :]).astype(o_ref.dtype)


def _sb_attention(qkv, tri, tq, bk):
    b, s, _ = qkv.shape
    blk = (1, s, HEAD_DIM)
    return pl.pallas_call(
        functools.partial(_sb_kernel, tq=tq, bk=bk, scale=HEAD_DIM ** -0.5),
        out_shape=jax.ShapeDtypeStruct((b, s, SB_W), BF16),
        grid=(b, SB_HEADS),
        in_specs=[pl.BlockSpec(blk, lambda i, h: (i, 0, h)),
                  pl.BlockSpec(blk, lambda i, h: (i, 0, SB_HEADS + h)),
                  pl.BlockSpec(blk, lambda i, h: (i, 0, 2 * SB_HEADS + h)),
                  pl.BlockSpec((bk, bk), lambda i, h: (0, 0))],
        out_specs=pl.BlockSpec(blk, lambda i, h: (i, 0, h)),
        compiler_params=_params(2, 48),
        name="sb_attn",
    )(qkv, qkv, qkv, tri)


def _dil_kernel(*refs, n_sub, has_prev, dilation, slopes, scale, n_other):
    refs = list(refs)
    q_ref, kc_ref = refs.pop(0), refs.pop(0)
    kp_ref = refs.pop(0) if has_prev else None
    vc_ref = refs.pop(0)
    vp_ref = refs.pop(0) if has_prev else None
    others = [(refs.pop(0), refs.pop(0)) for _ in range(n_other)]
    o_ref = refs.pop(0)
    lse_ref = None if n_other else refs.pop(0)
    o_slab, lse_slab = refs

    n_keys = 2 * BLOCK if has_prev else BLOCK
    a_idx = lax.broadcasted_iota(jnp.int32, (BLOCK, n_keys), 0)
    b_idx = lax.broadcasted_iota(jnp.int32, (BLOCK, n_keys), 1)
    rel = a_idx + (n_keys - BLOCK) - b_idx
    in_window = (rel >= 0) & (rel <= BLOCK)
    dist = (dilation * rel).astype(F32)
    step = jnp.zeros((BLOCK, n_keys), jnp.int32) + pl.program_id(1)
    valid_first = in_window & ((b_idx >= BLOCK) | (step > 0)) if has_prev else in_window

    def operands(r, j, h):
        rows = slice(j * BLOCK, (j + 1) * BLOCK)
        cols = slice(h * HEAD_DIM, (h + 1) * HEAD_DIM)
        q, kk, vv = q_ref[0, r, rows, cols], kc_ref[0, r, rows, cols], vc_ref[0, r, rows, cols]
        if has_prev:
            if j == 0:
                k_prev, v_prev = kp_ref[0, r, :, cols], vp_ref[0, r, :, cols]
            else:
                prev_rows = slice((j - 1) * BLOCK, j * BLOCK)
                k_prev, v_prev = kc_ref[0, r, prev_rows, cols], vc_ref[0, r, prev_rows, cols]
            kk = jnp.concatenate([k_prev, kk], axis=0)
            vv = jnp.concatenate([v_prev, vv], axis=0)
        return q, kk, vv

    blocks = [(r, j, h) for r in range(dilation) for j in range(n_sub) for h in range(DIL_HEADS_PER_GROUP)]
    for c0 in range(0, len(blocks), DIL_CHAINS):
        chunk = blocks[c0:c0 + DIL_CHAINS]
        ops = [operands(*blk) for blk in chunk]
        s = [jnp.where(valid_first if j == 0 else in_window,
                       _dot_nt(q, kk) * scale - slopes[h] * dist, -jnp.inf)
             for (r, j, h), (q, kk, vv) in zip(chunk, ops)]
        m = [jnp.max(si, axis=-1, keepdims=True) for si in s]
        p = [jnp.exp(si - mi) for si, mi in zip(s, m)]
        l = [jnp.sum(pi, axis=-1, keepdims=True) for pi in p]
        o = [_dot(pi.astype(BF16), vv) / li for pi, li, (q, kk, vv) in zip(p, l, ops)]
        for (r, j, h), oi, mi, li in zip(chunk, o, m, l):
            out_rows = pl.ds(j * BLOCK * dilation + r, BLOCK, stride=dilation)
            o_slab[h, out_rows, :] = oi
            lse_slab[h, out_rows, :] = jnp.broadcast_to(mi + jnp.log(li), (BLOCK, HEAD_DIM))

    for h in range(DIL_HEADS_PER_GROUP):
        cols = slice(h * HEAD_DIM, (h + 1) * HEAD_DIM)
        if not others:
            o_ref[0, :, cols] = o_slab[h].astype(o_ref.dtype)
            lse_ref[0, :, cols] = lse_slab[h]
            continue
        outs = [o_slab[h]] + [og_ref[0, :, cols].astype(F32) for og_ref, _ in others]
        lses = [lse_slab[h]] + [lg_ref[0, :, cols] for _, lg_ref in others]
        top = functools.reduce(jnp.maximum, lses)
        e = [jnp.exp(lg - top) for lg in lses]
        num = functools.reduce(lambda a, b: a + b, [eg * og for eg, og in zip(e, outs)])
        den = functools.reduce(lambda a, b: a + b, e)
        o_ref[0, :, cols] = (num / den).astype(o_ref.dtype)


def _dil_attention(qkv, batch, group, slopes, n_sub, others=()):
    window, dilation = DIL_GROUPS[group]
    s = qkv.shape[0] // batch
    sub_len = s // dilation
    n_blk = sub_len // BLOCK
    n_sub = min(n_sub, n_blk)
    has_prev = n_blk > 1
    assert window // dilation == BLOCK and sub_len % BLOCK == 0 and n_blk % n_sub == 0
    view = qkv.reshape(batch, dilation, sub_len, 3 * DIL_OUT)
    rows = n_sub * BLOCK

    def cur(col):
        return pl.BlockSpec((1, dilation, rows, DIL_OUT), lambda i, n: (i, 0, n, col))

    def prev(col):
        return pl.BlockSpec((1, dilation, BLOCK, DIL_OUT),
                            lambda i, n: (i, 0, jnp.maximum(n * n_sub - 1, 0), col))

    if has_prev:
        in_specs = [cur(0), cur(1), prev(1), cur(2), prev(2)]
    else:
        in_specs = [cur(0), cur(1), cur(2)]
    args = [view] * len(in_specs)
    tile = pl.BlockSpec((1, dilation * rows, DIL_OUT), lambda i, n: (i, n, 0))
    for o_g, lse_g in others:
        in_specs += [tile, tile]
        args += [o_g, lse_g]
    o_shape = jax.ShapeDtypeStruct((batch, s, DIL_OUT), BF16)
    lse_shape = jax.ShapeDtypeStruct((batch, s, DIL_OUT), F32)
    slab = pltpu.VMEM((DIL_HEADS_PER_GROUP, dilation * rows, HEAD_DIM), F32)
    return pl.pallas_call(
        functools.partial(_dil_kernel, n_sub=n_sub, has_prev=has_prev, dilation=dilation,
                          slopes=slopes, scale=HEAD_DIM ** -0.5, n_other=len(others)),
        out_shape=o_shape if others else (o_shape, lse_shape),
        grid=(batch, n_blk // n_sub),
        in_specs=in_specs,
        out_specs=tile if others else (tile, tile),
        scratch_shapes=[slab, slab],
        compiler_params=_params(2, 40),
        name=f"dil_attn_d{dilation}",
    )(*args)


def _post_kernel(osb_ref, odl_ref, gp_ref, bg_ref, x_ref, mod_ref, g2_ref, wsb_ref, wdl_ref, wout_ref,
                 x1_ref, h2_ref):
    d = x_ref.shape[-1]
    chunks = _row_chunks(x_ref.shape[0])
    y_sb = [_dot(osb_ref[c, :], wsb_ref[...]) for c in chunks]
    y_dl = [_dot(odl_ref[c, :], wdl_ref[...]) for c in chunks]
    gates = [_sigmoid(gp_ref[c, :].astype(F32) + bg_ref[...]) for c in chunks]
    mixed = [(g[:, :d] * a + g[:, d:] * b).astype(BF16) for g, a, b in zip(gates, y_sb, y_dl)]
    t = [_dot(mx, wout_ref[...]) for mx in mixed]
    for c, tc in zip(chunks, t):
        x1 = x_ref[c, :] + mod_ref[0, 2:3, :] * tc
        x1_ref[c, :] = x1
        h2 = _rms(x1) * g2_ref[...] * (1.0 + mod_ref[0, 4:5, :]) + mod_ref[0, 3:4, :]
        h2_ref[c, :] = h2.astype(h2_ref.dtype)


def _post(o_sb, o_dl, gate_pre, b_gate, x2d, mod3, g2, w_sb, w_dl, w_out, seq, tm=512):
    m, d = x2d.shape

    def row(width):
        return pl.BlockSpec((tm, width), lambda i: (i, 0))

    def whole(arr):
        return pl.BlockSpec(arr.shape, lambda i: (0,) * arr.ndim, pipeline_mode=pl.Buffered(1))

    return pl.pallas_call(
        _post_kernel,
        out_shape=(jax.ShapeDtypeStruct((m, d), F32), jax.ShapeDtypeStruct((m, d), BF16)),
        grid=(m // tm,),
        in_specs=[row(SB_W), row(DIL_OUT), row(2 * d), whole(b_gate), row(d),
                  pl.BlockSpec((1, N_MOD, d), lambda i: ((i * tm) // seq, 0, 0)),
                  whole(g2), whole(w_sb), whole(w_dl), whole(w_out)],
        out_specs=(row(d), row(d)),
        compiler_params=_params(1, 60),
        name="post",
    )(o_sb, o_dl, gate_pre, b_gate, x2d, mod3, g2, w_sb, w_dl, w_out)


def _swiglu_kernel(h_ref, wg_ref, wu_ref, o_ref, wgb_ref, wub_ref):
    @pl.when(pl.program_id(1) == 0)
    def _():
        wgb_ref[...] = wg_ref[...].astype(BF16)
        wub_ref[...] = wu_ref[...].astype(BF16)

    h = h_ref[...]
    g = _dot(h, wgb_ref[...])
    u = _dot(h, wub_ref[...])
    o_ref[...] = (g * _sigmoid(g) * u).astype(o_ref.dtype)


def _swiglu(h, wg, wu, layer, tm=2048, tn=512):
    m, k = h.shape
    n = wg.shape[-1]
    w_spec = pl.BlockSpec((None, k, tn), lambda j, i: (layer, 0, j))
    return pl.pallas_call(
        _swiglu_kernel,
        out_shape=jax.ShapeDtypeStruct((m, n), BF16),
        grid=(n // tn, m // tm),
        in_specs=[pl.BlockSpec((tm, k), lambda j, i: (i, 0)), w_spec, w_spec],
        out_specs=pl.BlockSpec((tm, tn), lambda j, i: (i, j)),
        scratch_shapes=[pltpu.VMEM((k, tn), BF16), pltpu.VMEM((k, tn), BF16)],
        compiler_params=_params(2, 60),
        name="swiglu",
    )(h, wg, wu)


def _down_kernel(a_ref, w_ref, x1_ref, mod_ref, gf_ref, o_ref, *, final_norm):
    for c in _row_chunks(a_ref.shape[0]):
        x2 = x1_ref[c, :] + mod_ref[0, 5:6, :] * _dot(a_ref[c, :], w_ref[...])
        o_ref[c, :] = _rms(x2) * gf_ref[...] if final_norm else x2


def _down(a, w, x1, mod3, g_final, seq, final_norm, tm=512):
    m, f = a.shape
    d = w.shape[1]
    return pl.pallas_call(
        functools.partial(_down_kernel, final_norm=final_norm),
        out_shape=jax.ShapeDtypeStruct((m, d), F32),
        grid=(m // tm,),
        in_specs=[pl.BlockSpec((tm, f), lambda i: (i, 0)),
                  pl.BlockSpec((f, d), lambda i: (0, 0), pipeline_mode=pl.Buffered(1)),
                  pl.BlockSpec((tm, d), lambda i: (i, 0)),
                  pl.BlockSpec((1, N_MOD, d), lambda i: ((i * tm) // seq, 0, 0)),
                  pl.BlockSpec((1, d), lambda i: (0, 0))],
        out_specs=pl.BlockSpec((tm, d), lambda i: (i, 0)),
        compiler_params=_params(1, 60),
        name="down",
    )(a, w, x1, mod3, g_final)


def _alibi_slopes():
    return tuple(2.0 ** (-8.0 * (i + 1.0) / DIL_HEADS) for i in range(DIL_HEADS))


def kernel(x, c, w_ada, b_ada, g_norm1, g_norm2, g_final, w_in, b_gate, w_proj_sb, w_proj_dil,
           w_out, w_ffn_gate, w_ffn_up, w_ffn_down):
    batch, seq, d = x.shape
    tokens = batch * seq
    n_layers = w_ada.shape[0]
    slopes = _alibi_slopes()
    n_groups = len(DIL_GROUPS)
    sb_tile = 256
    idx = jnp.arange(sb_tile)
    tri = (idx[:, None] >= idx[None, :]).astype(BF16)
    c_pad = jnp.pad(c, ((0, ADA_ROWS - batch), (0, 0)))
    perm_dils = tuple(dil for _, dil in DIL_GROUPS if dil > 1)
    dl_col0 = 3 * SB_W // W_COL
    gate_col0 = dl_col0 + 3 * DIL_W // W_COL
    assert DIL_OUT == W_COL
    group_order = sorted(range(n_groups), key=lambda g: -DIL_GROUPS[g][1])
    assert DIL_GROUPS[group_order[-1]][1] == 1

    for l in range(n_layers):
        mod3 = _ada(c_pad, w_ada[l], b_ada[l][None, :]).reshape(ADA_ROWS, N_MOD, d)

        h_all = _norm1(x, g_norm1[l][None, :], mod3, perm_dils)
        h = h_all[0].reshape(tokens, d)
        h_perm = dict(zip(perm_dils, (hp.reshape(tokens, d) for hp in h_all[1:])))

        qkv_sb = _proj(h, w_in, l, lambda j, r: 2 * j + r, 3 * SB_W // (2 * W_COL), 2, "proj_sb", tm=2048)
        gate_pre = _proj(h, w_in, l, lambda j, r: gate_col0 + 2 * j + r, 2 * d // (2 * W_COL), 2,
                         "proj_gate", tm=2048)

        o_sb = _sb_attention(qkv_sb.reshape(batch, seq, 3 * SB_W), tri, sb_tile, sb_tile)

        partial_groups = []
        for g in group_order:
            dil = DIL_GROUPS[g][1]
            qkv_g = _proj(h_perm.get(dil, h), w_in, l,
                          lambda j, r, g=g: dl_col0 + r * n_groups + g, 1, 3, f"proj_dil_d{dil}")
            g_slopes = slopes[g * DIL_HEADS_PER_GROUP:(g + 1) * DIL_HEADS_PER_GROUP]
            if g != group_order[-1]:
                partial_groups.append(_dil_attention(qkv_g, batch, g, g_slopes, n_sub=4))
            else:
                o_dl = _dil_attention(qkv_g, batch, g, g_slopes, n_sub=4, others=tuple(partial_groups))

        x1, h2 = _post(o_sb.reshape(tokens, SB_W), o_dl.reshape(tokens, DIL_OUT), gate_pre,
                       b_gate[l][None, :], x.reshape(tokens, d), mod3, g_norm2[l][None, :],
                       w_proj_sb[l].astype(BF16), w_proj_dil[l].astype(BF16), w_out[l].astype(BF16), seq)

        a = _swiglu(h2, w_ffn_gate, w_ffn_up, l)
        x = _down(a, w_ffn_down[l].astype(BF16), x1, mod3, g_final[None, :], seq,
                  final_norm=(l == n_layers - 1)).reshape(batch, seq, d)
    return x
```

```python
import functools

import jax
import jax.numpy as jnp
from jax import lax
from jax.experimental import pallas as pl
from jax.experimental.pallas import tpu as pltpu

F32 = jnp.float32
BF16 = jnp.bfloat16

HEAD_DIM = 128
SB_HEADS = 8
DIL_GROUPS = ((128, 1), (512, 4), (2048, 16))
DIL_HEADS_PER_GROUP = 4
DIL_HEADS = DIL_HEADS_PER_GROUP * len(DIL_GROUPS)
BLOCK = 128
SB_W = SB_HEADS * HEAD_DIM
DIL_W = DIL_HEADS * HEAD_DIM
DIL_OUT = DIL_HEADS_PER_GROUP * HEAD_DIM
N_MOD = 6
EPS = 1e-6
LOG2E = 1.4426950408889634

LANES = 128
MIB = 1024 * 1024
ADA_ROWS = 8
W_COL = 512
SUB_ROWS = 256
DIL_CHAINS = 8


def _params(n_axes, vmem_mib):
    return pltpu.CompilerParams(dimension_semantics=("arbitrary",) * n_axes,
                                vmem_limit_bytes=vmem_mib * MIB)


def _dot(a, b):
    return jnp.dot(a, b, preferred_element_type=F32)


def _dot_nt(a, b):
    return lax.dot_general(a, b, (((1,), (1,)), ((), ())), preferred_element_type=F32)


def _sigmoid(x):
    return 1.0 / (1.0 + jnp.exp(-x))


def _rms(x):
    return x * lax.rsqrt(jnp.mean(x * x, axis=-1, keepdims=True) + EPS)


def _row_chunks(rows):
    return [slice(r, r + SUB_ROWS) for r in range(0, rows, SUB_ROWS)]


def _ada_kernel(c_ref, w_ref, b_ref, o_ref):
    c = c_ref[...]
    s = (c * _sigmoid(c)).astype(BF16)
    o_ref[...] = _dot(s, w_ref[...].astype(BF16)) + b_ref[...]


def _ada(c_pad, w, b, tn=1024):
    d, n = w.shape
    return pl.pallas_call(
        _ada_kernel,
        out_shape=jax.ShapeDtypeStruct((ADA_ROWS, n), F32),
        grid=(n // tn,),
        in_specs=[pl.BlockSpec((ADA_ROWS, d), lambda j: (0, 0)),
                  pl.BlockSpec((d, tn), lambda j: (0, j)),
                  pl.BlockSpec((1, tn), lambda j: (0, j))],
        out_specs=pl.BlockSpec((ADA_ROWS, tn), lambda j: (0, j)),
        compiler_params=_params(1, 40),
        name="ada",
    )(c_pad, w, b)


def _norm1_kernel(x_ref, g_ref, mod_ref, o_ref, *rest, dils):
    n_perm = len(dils)
    perm_refs, slabs = rest[:n_perm], rest[n_perm:]
    ts, d_model = x_ref.shape[1], x_ref.shape[2]
    y = _rms(x_ref[0]) * g_ref[...]
    h = y * (1.0 + mod_ref[0, 1:2, :]) + mod_ref[0, 0:1, :]
    o_ref[0] = h.astype(o_ref.dtype)
    n_slab = d_model // LANES
    for s in range(n_slab):
        slabs[0][s] = h[:, s * LANES:(s + 1) * LANES]
    level = {1: slabs[0]}
    for n, (dil, p_ref) in enumerate(zip(dils, perm_refs)):
        base = max(p for p in level if dil % p == 0)
        step, src = dil // base, level[base]
        keep = slabs[n + 1] if n + 1 < len(slabs) else None
        for r in range(dil):
            r_hi, r_lo = divmod(r, base)
            start = r_lo * (ts // base) + r_hi
            for s in range(n_slab):
                rows = src[s, pl.ds(start, ts // dil, stride=step), :]
                p_ref[0, r, :, s * LANES:(s + 1) * LANES] = rows.astype(p_ref.dtype)
                if keep is not None:
                    keep[s, r * (ts // dil):(r + 1) * (ts // dil), :] = rows
        if keep is not None:
            level[dil] = keep


def _norm1(x, g, mod3, dils, ts=512):
    b, s, d = x.shape
    out_shape = [jax.ShapeDtypeStruct((b, s, d), BF16)]
    out_specs = [pl.BlockSpec((1, ts, d), lambda i, j: (i, j, 0))]
    for dil in dils:
        out_shape.append(jax.ShapeDtypeStruct((b, dil, s // dil, d), BF16))
        out_specs.append(pl.BlockSpec((1, dil, ts // dil, d), lambda i, j: (i, 0, j, 0)))
    slab = pltpu.VMEM((d // LANES, ts, LANES), F32)
    return pl.pallas_call(
        functools.partial(_norm1_kernel, dils=dils),
        out_shape=out_shape,
        grid=(b, s // ts),
        in_specs=[pl.BlockSpec((1, ts, d), lambda i, j: (i, j, 0)),
                  pl.BlockSpec((1, d), lambda i, j: (0, 0)),
                  pl.BlockSpec((1, N_MOD, d), lambda i, j: (i, 0, 0))],
        out_specs=out_specs,
        scratch_shapes=[slab] * max(1, len(dils)),
        compiler_params=_params(2, 40),
        name="norm1",
    )(x, g, mod3)


def _proj_kernel(*refs, n_rhs):
    h_ref, w_refs, o_ref, wb_ref = refs[0], refs[1:1 + n_rhs], refs[1 + n_rhs], refs[2 + n_rhs]

    @pl.when(pl.program_id(1) == 0)
    def _():
        for r, w_ref in enumerate(w_refs):
            wb_ref[:, r * W_COL:(r + 1) * W_COL] = w_ref[...].astype(BF16)

    for r in range(n_rhs):
        cols = slice(r * W_COL, (r + 1) * W_COL)
        o_ref[:, cols] = _dot(h_ref[...], wb_ref[:, cols]).astype(o_ref.dtype)


def _proj(h, w_in, layer, col_fn, n_tiles, n_rhs, name, tm=2048):
    m, k = h.shape
    tn = n_rhs * W_COL
    mode = pl.Buffered(1) if n_tiles == 1 else None

    def w_spec(r):
        return pl.BlockSpec((None, k, W_COL), lambda j, i: (layer, 0, col_fn(j, r)), pipeline_mode=mode)

    return pl.pallas_call(
        functools.partial(_proj_kernel, n_rhs=n_rhs),
        out_shape=jax.ShapeDtypeStruct((m, n_tiles * tn), BF16),
        grid=(n_tiles, m // tm),
        in_specs=[pl.BlockSpec((tm, k), lambda j, i: (i, 0))] + [w_spec(r) for r in range(n_rhs)],
        out_specs=pl.BlockSpec((tm, tn), lambda j, i: (i, j)),
        scratch_shapes=[pltpu.VMEM((k, tn), BF16)],
        compiler_params=_params(2, 56),
        name=name,
    )(h, *([w_in] * n_rhs))


def _sb_kernel(q_ref, k_ref, v_ref, tri_ref, o_ref, *, tq, bk, scale):
    seq = q_ref.shape[1]
    heads = [slice(h * HEAD_DIM, (h + 1) * HEAD_DIM) for h in range(q_ref.shape[2] // HEAD_DIM)]
    tri = tri_ref[...]
    row = lax.broadcasted_iota(jnp.int32, (tq, bk), 0)
    col = lax.broadcasted_iota(jnp.int32, (tq, bk), 1)
    for i in range(seq // tq):
        n_keys = (i + 1) * tq
        rows = slice(i * tq, (i + 1) * tq)
        z_all = [_dot_nt(q_ref[0, rows, hd], k_ref[0, :n_keys, hd]) * (scale * LOG2E) for hd in heads]
        run = [jnp.zeros((tq, 1), F32) for _ in heads]
        weights = [[None] * (n_keys // bk) for _ in heads]
        for jb in reversed(range(n_keys // bk)):
            on_diagonal = (jb + 1) * bk > i * tq
            keep = col + (jb * bk - i * tq) < row
            z = [za[:, jb * bk:(jb + 1) * bk] for za in z_all]
            sp = [jnp.maximum(zh, 0.0) + jnp.log(1.0 + jnp.exp2(-jnp.abs(zh))) * LOG2E for zh in z]
            own = [zh - sh for zh, sh in zip(z, sp)]
            if on_diagonal:
                sp = [jnp.where(keep, sh, 0.0) for sh in sp]
            sp16 = [sh.astype(BF16) for sh in sp]
            later = [_dot(sh, tri) for sh in sp16]
            a = [jnp.exp2(oh - (rh + lh)) for oh, rh, lh in zip(own, run, later)]
            if on_diagonal:
                a = [jnp.where(keep, ah, 0.0) for ah in a]
            for h, ah in enumerate(a):
                weights[h][jb] = ah.astype(BF16)
            run = [rh + lh[:, 0:1] + sh[:, 0:1].astype(F32) for rh, lh, sh in zip(run, later, sp16)]
        for h, hd in enumerate(heads):
            a_all = weights[h][0] if len(weights[h]) == 1 else jnp.concatenate(weights[h], axis=1)
            o_ref[0, rows, hd] = _dot(a_all, v_ref[0, :n_keys, hd]).astype(o_ref.dtype)


def _sb_attention(qkv, tri, tq, bk, heads_per_step=2):
    b, s, _ = qkv.shape
    n_steps = SB_HEADS // heads_per_step
    blk = (1, s, heads_per_step * HEAD_DIM)
    return pl.pallas_call(
        functools.partial(_sb_kernel, tq=tq, bk=bk, scale=HEAD_DIM ** -0.5),
        out_shape=jax.ShapeDtypeStruct((b, s, SB_W), BF16),
        grid=(b, n_steps),
        in_specs=[pl.BlockSpec(blk, lambda i, h: (i, 0, h)),
                  pl.BlockSpec(blk, lambda i, h: (i, 0, n_steps + h)),
                  pl.BlockSpec(blk, lambda i, h: (i, 0, 2 * n_steps + h)),
                  pl.BlockSpec((bk, bk), lambda i, h: (0, 0))],
        out_specs=pl.BlockSpec(blk, lambda i, h: (i, 0, h)),
        compiler_params=_params(2, 48),
        name="sb_attn",
    )(qkv, qkv, qkv, tri)


def _dil_kernel(*refs, n_sub, has_prev, dilation, slopes, scale, n_other):
    refs = list(refs)
    q_ref, kc_ref = refs.pop(0), refs.pop(0)
    kp_ref = refs.pop(0) if has_prev else None
    vc_ref = refs.pop(0)
    vp_ref = refs.pop(0) if has_prev else None
    others = [(refs.pop(0), refs.pop(0)) for _ in range(n_other)]
    o_ref = refs.pop(0)
    lse_ref = None if n_other else refs.pop(0)
    o_slab, lse_slab = refs

    n_keys = 2 * BLOCK if has_prev else BLOCK
    a_idx = lax.broadcasted_iota(jnp.int32, (BLOCK, n_keys), 0)
    b_idx = lax.broadcasted_iota(jnp.int32, (BLOCK, n_keys), 1)
    rel = a_idx + (n_keys - BLOCK) - b_idx
    in_window = (rel >= 0) & (rel <= BLOCK)
    dist = (dilation * rel).astype(F32)
    step = jnp.zeros((BLOCK, n_keys), jnp.int32) + pl.program_id(1)
    valid_first = in_window & ((b_idx >= BLOCK) | (step > 0)) if has_prev else in_window

    def operands(r, j, h):
        rows = slice(j * BLOCK, (j + 1) * BLOCK)
        cols = slice(h * HEAD_DIM, (h + 1) * HEAD_DIM)
        q, kk, vv = q_ref[0, r, rows, cols], kc_ref[0, r, rows, cols], vc_ref[0, r, rows, cols]
        if has_prev:
            if j == 0:
                k_prev, v_prev = kp_ref[0, r, :, cols], vp_ref[0, r, :, cols]
            else:
                prev_rows = slice((j - 1) * BLOCK, j * BLOCK)
                k_prev, v_prev = kc_ref[0, r, prev_rows, cols], vc_ref[0, r, prev_rows, cols]
            kk = jnp.concatenate([k_prev, kk], axis=0)
            vv = jnp.concatenate([v_prev, vv], axis=0)
        return q, kk, vv

    blocks = [(r, j, h) for r in range(dilation) for j in range(n_sub) for h in range(DIL_HEADS_PER_GROUP)]
    for c0 in range(0, len(blocks), DIL_CHAINS):
        chunk = blocks[c0:c0 + DIL_CHAINS]
        ops = [operands(*blk) for blk in chunk]
        s = [jnp.where(valid_first if j == 0 else in_window,
                       _dot_nt(q, kk) * scale - slopes[h] * dist, -jnp.inf)
             for (r, j, h), (q, kk, vv) in zip(chunk, ops)]
        m = [jnp.max(si, axis=-1, keepdims=True) for si in s]
        p = [jnp.exp(si - mi) for si, mi in zip(s, m)]
        l = [jnp.sum(pi, axis=-1, keepdims=True) for pi in p]
        o = [_dot(pi.astype(BF16), vv) / li for pi, li, (q, kk, vv) in zip(p, l, ops)]
        for (r, j, h), oi, mi, li in zip(chunk, o, m, l):
            out_rows = pl.ds(j * BLOCK * dilation + r, BLOCK, stride=dilation)
            o_slab[h, out_rows, :] = oi
            lse_slab[h, out_rows, :] = jnp.broadcast_to(mi + jnp.log(li), (BLOCK, HEAD_DIM))

    for h in range(DIL_HEADS_PER_GROUP):
        cols = slice(h * HEAD_DIM, (h + 1) * HEAD_DIM)
        if not others:
            o_ref[0, :, cols] = o_slab[h].astype(o_ref.dtype)
            lse_ref[0, :, cols] = lse_slab[h]
            continue
        outs = [o_slab[h]] + [og_ref[0, :, cols].astype(F32) for og_ref, _ in others]
        lses = [lse_slab[h]] + [lg_ref[0, :, cols] for _, lg_ref in others]
        top = functools.reduce(jnp.maximum, lses)
        e = [jnp.exp(lg - top) for lg in lses]
        num = functools.reduce(lambda a, b: a + b, [eg * og for eg, og in zip(e, outs)])
        den = functools.reduce(lambda a, b: a + b, e)
        o_ref[0, :, cols] = (num / den).astype(o_ref.dtype)


def _dil_attention(qkv, batch, group, slopes, n_sub, others=()):
    window, dilation = DIL_GROUPS[group]
    s = qkv.shape[0] // batch
    sub_len = s // dilation
    n_blk = sub_len // BLOCK
    n_sub = min(n_sub, n_blk)
    has_prev = n_blk > 1
    assert window // dilation == BLOCK and sub_len % BLOCK == 0 and n_blk % n_sub == 0
    view = qkv.reshape(batch, dilation, sub_len, 3 * DIL_OUT)
    rows = n_sub * BLOCK

    def cur(col):
        return pl.BlockSpec((1, dilation, rows, DIL_OUT), lambda i, n: (i, 0, n, col))

    def prev(col):
        return pl.BlockSpec((1, dilation, BLOCK, DIL_OUT),
                            lambda i, n: (i, 0, jnp.maximum(n * n_sub - 1, 0), col))

    if has_prev:
        in_specs = [cur(0), cur(1), prev(1), cur(2), prev(2)]
    else:
        in_specs = [cur(0), cur(1), cur(2)]
    args = [view] * len(in_specs)
    tile = pl.BlockSpec((1, dilation * rows, DIL_OUT), lambda i, n: (i, n, 0))
    for o_g, lse_g in others:
        in_specs += [tile, tile]
        args += [o_g, lse_g]
    o_shape = jax.ShapeDtypeStruct((batch, s, DIL_OUT), BF16)
    lse_shape = jax.ShapeDtypeStruct((batch, s, DIL_OUT), F32)
    slab = pltpu.VMEM((DIL_HEADS_PER_GROUP, dilation * rows, HEAD_DIM), F32)
    return pl.pallas_call(
        functools.partial(_dil_kernel, n_sub=n_sub, has_prev=has_prev, dilation=dilation,
                          slopes=slopes, scale=HEAD_DIM ** -0.5, n_other=len(others)),
        out_shape=o_shape if others else (o_shape, lse_shape),
        grid=(batch, n_blk // n_sub),
        in_specs=in_specs,
        out_specs=tile if others else (tile, tile),
        scratch_shapes=[slab, slab],
        compiler_params=_params(2, 40),
        name=f"dil_attn_d{dilation}",
    )(*args)


def _post_kernel(osb_ref, odl_ref, gp_ref, bg_ref, x_ref, mod_ref, g2_ref, wsb_ref, wdl_ref, wout_ref,
                 x1_ref, h2_ref):
    d = x_ref.shape[-1]
    chunks = _row_chunks(x_ref.shape[0])
    y_sb = [_dot(osb_ref[c, :], wsb_ref[...]) for c in chunks]
    y_dl = [_dot(odl_ref[c, :], wdl_ref[...]) for c in chunks]
    gates = [_sigmoid(gp_ref[c, :].astype(F32) + bg_ref[...]) for c in chunks]
    mixed = [(g[:, :d] * a + g[:, d:] * b).astype(BF16) for g, a, b in zip(gates, y_sb, y_dl)]
    t = [_dot(mx, wout_ref[...]) for mx in mixed]
    for c, tc in zip(chunks, t):
        x1 = x_ref[c, :] + mod_ref[0, 2:3, :] * tc
        x1_ref[c, :] = x1
        h2 = _rms(x1) * g2_ref[...] * (1.0 + mod_ref[0, 4:5, :]) + mod_ref[0, 3:4, :]
        h2_ref[c, :] = h2.astype(h2_ref.dtype)


def _post(o_sb, o_dl, gate_pre, b_gate, x2d, mod3, g2, w_sb, w_dl, w_out, seq, tm=512):
    m, d = x2d.shape

    def row(width):
        return pl.BlockSpec((tm, width), lambda i: (i, 0))

    def whole(arr):
        return pl.BlockSpec(arr.shape, lambda i: (0,) * arr.ndim, pipeline_mode=pl.Buffered(1))

    return pl.pallas_call(
        _post_kernel,
        out_shape=(jax.ShapeDtypeStruct((m, d), F32), jax.ShapeDtypeStruct((m, d), BF16)),
        grid=(m // tm,),
        in_specs=[row(SB_W), row(DIL_OUT), row(2 * d), whole(b_gate), row(d),
                  pl.BlockSpec((1, N_MOD, d), lambda i: ((i * tm) // seq, 0, 0)),
                  whole(g2), whole(w_sb), whole(w_dl), whole(w_out)],
        out_specs=(row(d), row(d)),
        compiler_params=_params(1, 60),
        name="post",
    )(o_sb, o_dl, gate_pre, b_gate, x2d, mod3, g2, w_sb, w_dl, w_out)


def _swiglu_kernel(h_ref, wg_ref, wu_ref, o_ref, wgb_ref, wub_ref):
    @pl.when(pl.program_id(1) == 0)
    def _():
        wgb_ref[...] = wg_ref[...].astype(BF16)
        wub_ref[...] = wu_ref[...].astype(BF16)

    h = h_ref[...]
    g = _dot(h, wgb_ref[...])
    u = _dot(h, wub_ref[...])
    o_ref[...] = (g * _sigmoid(g) * u).astype(o_ref.dtype)


def _swiglu(h, wg, wu, layer, tm=2048, tn=512):
    m, k = h.shape
    n = wg.shape[-1]
    w_spec = pl.BlockSpec((None, k, tn), lambda j, i: (layer, 0, j))
    return pl.pallas_call(
        _swiglu_kernel,
        out_shape=jax.ShapeDtypeStruct((m, n), BF16),
        grid=(n // tn, m // tm),
        in_specs=[pl.BlockSpec((tm, k), lambda j, i: (i, 0)), w_spec, w_spec],
        out_specs=pl.BlockSpec((tm, tn), lambda j, i: (i, j)),
        scratch_shapes=[pltpu.VMEM((k, tn), BF16), pltpu.VMEM((k, tn), BF16)],
        compiler_params=_params(2, 60),
        name="swiglu",
    )(h, wg, wu)


def _down_kernel(a_ref, w_ref, x1_ref, mod_ref, gf_ref, o_ref, *, final_norm):
    for c in _row_chunks(a_ref.shape[0]):
        x2 = x1_ref[c, :] + mod_ref[0, 5:6, :] * _dot(a_ref[c, :], w_ref[...])
        o_ref[c, :] = _rms(x2) * gf_ref[...] if final_norm else x2


def _down(a, w, x1, mod3, g_final, seq, final_norm, tm=512):
    m, f = a.shape
    d = w.shape[1]
    return pl.pallas_call(
        functools.partial(_down_kernel, final_norm=final_norm),
        out_shape=jax.ShapeDtypeStruct((m, d), F32),
        grid=(m // tm,),
        in_specs=[pl.BlockSpec((tm, f), lambda i: (i, 0)),
                  pl.BlockSpec((f, d), lambda i: (0, 0), pipeline_mode=pl.Buffered(1)),
                  pl.BlockSpec((tm, d), lambda i: (i, 0)),
                  pl.BlockSpec((1, N_MOD, d), lambda i: ((i * tm) // seq, 0, 0)),
                  pl.BlockSpec((1, d), lambda i: (0, 0))],
        out_specs=pl.BlockSpec((tm, d), lambda i: (i, 0)),
        compiler_params=_params(1, 60),
        name="down",
    )(a, w, x1, mod3, g_final)


def _alibi_slopes():
    return tuple(2.0 ** (-8.0 * (i + 1.0) / DIL_HEADS) for i in range(DIL_HEADS))


def kernel(x, c, w_ada, b_ada, g_norm1, g_norm2, g_final, w_in, b_gate, w_proj_sb, w_proj_dil,
           w_out, w_ffn_gate, w_ffn_up, w_ffn_down):
    batch, seq, d = x.shape
    tokens = batch * seq
    n_layers = w_ada.shape[0]
    slopes = _alibi_slopes()
    n_groups = len(DIL_GROUPS)
    sb_tile = 256
    idx = jnp.arange(sb_tile)
    tri = (idx[:, None] > idx[None, :]).astype(BF16)
    c_pad = jnp.pad(c, ((0, ADA_ROWS - batch), (0, 0)))
    perm_dils = tuple(sorted(dil for _, dil in DIL_GROUPS if dil > 1))
    dl_col0 = 3 * SB_W // W_COL
    gate_col0 = dl_col0 + 3 * DIL_W // W_COL
    assert DIL_OUT == W_COL
    group_order = sorted(range(n_groups), key=lambda g: -DIL_GROUPS[g][1])
    assert DIL_GROUPS[group_order[-1]][1] == 1

    for l in range(n_layers):
        mod3 = _ada(c_pad, w_ada[l], b_ada[l][None, :]).reshape(ADA_ROWS, N_MOD, d)

        h_all = _norm1(x, g_norm1[l][None, :], mod3, perm_dils)
        h = h_all[0].reshape(tokens, d)
        h_perm = dict(zip(perm_dils, (hp.reshape(tokens, d) for hp in h_all[1:])))

        qkv_sb = _proj(h, w_in, l, lambda j, r: 2 * j + r, 3 * SB_W // (2 * W_COL), 2, "proj_sb")
        gate_pre = _proj(h, w_in, l, lambda j, r: gate_col0 + 2 * j + r, 2 * d // (2 * W_COL), 2, "proj_gate")

        o_sb = _sb_attention(qkv_sb.reshape(batch, seq, 3 * SB_W), tri, sb_tile, sb_tile)

        partial_groups = []
        for g in group_order:
            dil = DIL_GROUPS[g][1]
            qkv_g = _proj(h_perm.get(dil, h), w_in, l,
                          lambda j, r, g=g: dl_col0 + r * n_groups + g, 1, 3, f"proj_dil_d{dil}")
            g_slopes = slopes[g * DIL_HEADS_PER_GROUP:(g + 1) * DIL_HEADS_PER_GROUP]
            if g != group_order[-1]:
                partial_groups.append(_dil_attention(qkv_g, batch, g, g_slopes, n_sub=4))
            else:
                o_dl = _dil_attention(qkv_g, batch, g, g_slopes, n_sub=4, others=tuple(partial_groups))

        x1, h2 = _post(o_sb.reshape(tokens, SB_W), o_dl.reshape(tokens, DIL_OUT), gate_pre,
                       b_gate[l][None, :], x.reshape(tokens, d), mod3, g_norm2[l][None, :],
                       w_proj_sb[l].astype(BF16), w_proj_dil[l].astype(BF16), w_out[l].astype(BF16), seq)

        a = _swiglu(h2, w_ffn_gate, w_ffn_up, l)
        x = _down(a, w_ffn_down[l].astype(BF16), x1, mod3, g_final[None, :], seq,
                  final_norm=(l == n_layers - 1)).reshape(batch, seq, d)
    return x
```

```python
import functools

import jax
import jax.numpy as jnp
from jax import lax
from jax.experimental import pallas as pl
from jax.experimental.pallas import tpu as pltpu

F32 = jnp.float32
BF16 = jnp.bfloat16

HEAD_DIM = 128
SB_HEADS = 8
DIL_GROUPS = ((128, 1), (512, 4), (2048, 16))
DIL_HEADS_PER_GROUP = 4
DIL_HEADS = DIL_HEADS_PER_GROUP * len(DIL_GROUPS)
BLOCK = 128
SB_W = SB_HEADS * HEAD_DIM
DIL_W = DIL_HEADS * HEAD_DIM
DIL_OUT = DIL_HEADS_PER_GROUP * HEAD_DIM
N_MOD = 6
EPS = 1e-6
LOG2E = 1.4426950408889634

LANES = 128
MIB = 1024 * 1024
ADA_ROWS = 8
W_COL = 512
SUB_ROWS = 256
DIL_CHAINS = 8


def _params(n_axes, vmem_mib):
    return pltpu.CompilerParams(dimension_semantics=("arbitrary",) * n_axes,
                                vmem_limit_bytes=vmem_mib * MIB)


def _dot(a, b):
    return jnp.dot(a, b, preferred_element_type=F32)


def _dot_nt(a, b):
    return lax.dot_general(a, b, (((1,), (1,)), ((), ())), preferred_element_type=F32)


def _sigmoid(x):
    return 1.0 / (1.0 + jnp.exp(-x))


def _rms(x):
    return x * lax.rsqrt(jnp.mean(x * x, axis=-1, keepdims=True) + EPS)


def _row_chunks(rows):
    return [slice(r, r + SUB_ROWS) for r in range(0, rows, SUB_ROWS)]


def _ada_kernel(c_ref, w_ref, b_ref, o_ref):
    c = c_ref[...]
    s = (c * _sigmoid(c)).astype(BF16)
    o_ref[...] = _dot(s, w_ref[...].astype(BF16)) + b_ref[...]


def _ada(c_pad, w, b, tn=1024):
    d, n = w.shape
    return pl.pallas_call(
        _ada_kernel,
        out_shape=jax.ShapeDtypeStruct((ADA_ROWS, n), F32),
        grid=(n // tn,),
        in_specs=[pl.BlockSpec((ADA_ROWS, d), lambda j: (0, 0)),
                  pl.BlockSpec((d, tn), lambda j: (0, j)),
                  pl.BlockSpec((1, tn), lambda j: (0, j))],
        out_specs=pl.BlockSpec((ADA_ROWS, tn), lambda j: (0, j)),
        compiler_params=_params(1, 40),
        name="ada",
    )(c_pad, w, b)


def _norm1_kernel(x_ref, g_ref, mod_ref, o_ref, *rest, dils):
    n_perm = len(dils)
    perm_refs, slabs = rest[:n_perm], rest[n_perm:]
    ts, d_model = x_ref.shape[1], x_ref.shape[2]
    y = _rms(x_ref[0]) * g_ref[...]
    h = y * (1.0 + mod_ref[0, 1:2, :]) + mod_ref[0, 0:1, :]
    o_ref[0] = h.astype(o_ref.dtype)
    n_slab = d_model // LANES
    for s in range(n_slab):
        slabs[0][s] = h[:, s * LANES:(s + 1) * LANES]
    level = {1: slabs[0]}
    for n, (dil, p_ref) in enumerate(zip(dils, perm_refs)):
        base = max(p for p in level if dil % p == 0)
        step, src = dil // base, level[base]
        keep = slabs[n + 1] if n + 1 < len(slabs) else None
        for r in range(dil):
            r_hi, r_lo = divmod(r, base)
            start = r_lo * (ts // base) + r_hi
            for s in range(n_slab):
                rows = src[s, pl.ds(start, ts // dil, stride=step), :]
                p_ref[0, r, :, s * LANES:(s + 1) * LANES] = rows.astype(p_ref.dtype)
                if keep is not None:
                    keep[s, r * (ts // dil):(r + 1) * (ts // dil), :] = rows
        if keep is not None:
            level[dil] = keep


def _norm1(x, g, mod3, dils, ts=512):
    b, s, d = x.shape
    out_shape = [jax.ShapeDtypeStruct((b, s, d), BF16)]
    out_specs = [pl.BlockSpec((1, ts, d), lambda i, j: (i, j, 0))]
    for dil in dils:
        out_shape.append(jax.ShapeDtypeStruct((b, dil, s // dil, d), BF16))
        out_specs.append(pl.BlockSpec((1, dil, ts // dil, d), lambda i, j: (i, 0, j, 0)))
    slab = pltpu.VMEM((d // LANES, ts, LANES), F32)
    return pl.pallas_call(
        functools.partial(_norm1_kernel, dils=dils),
        out_shape=out_shape,
        grid=(b, s // ts),
        in_specs=[pl.BlockSpec((1, ts, d), lambda i, j: (i, j, 0)),
                  pl.BlockSpec((1, d), lambda i, j: (0, 0)),
                  pl.BlockSpec((1, N_MOD, d), lambda i, j: (i, 0, 0))],
        out_specs=out_specs,
        scratch_shapes=[slab] * max(1, len(dils)),
        compiler_params=_params(2, 40),
        name="norm1",
    )(x, g, mod3)


def _proj_kernel(*refs, n_rhs):
    h_ref, w_refs, o_ref, wb_ref = refs[0], refs[1:1 + n_rhs], refs[1 + n_rhs], refs[2 + n_rhs]

    @pl.when(pl.program_id(1) == 0)
    def _():
        for r, w_ref in enumerate(w_refs):
            wb_ref[:, r * W_COL:(r + 1) * W_COL] = w_ref[...].astype(BF16)

    for r in range(n_rhs):
        cols = slice(r * W_COL, (r + 1) * W_COL)
        o_ref[:, cols] = _dot(h_ref[...], wb_ref[:, cols]).astype(o_ref.dtype)


def _proj(h, w_in, layer, col_fn, n_tiles, n_rhs, name, tm=2048):
    m, k = h.shape
    tn = n_rhs * W_COL
    mode = pl.Buffered(1) if n_tiles == 1 else None

    def w_spec(r):
        return pl.BlockSpec((None, k, W_COL), lambda j, i: (layer, 0, col_fn(j, r)), pipeline_mode=mode)

    return pl.pallas_call(
        functools.partial(_proj_kernel, n_rhs=n_rhs),
        out_shape=jax.ShapeDtypeStruct((m, n_tiles * tn), BF16),
        grid=(n_tiles, m // tm),
        in_specs=[pl.BlockSpec((tm, k), lambda j, i: (i, 0))] + [w_spec(r) for r in range(n_rhs)],
        out_specs=pl.BlockSpec((tm, tn), lambda j, i: (i, j)),
        scratch_shapes=[pltpu.VMEM((k, tn), BF16)],
        compiler_params=_params(2, 56),
        name=name,
    )(h, *([w_in] * n_rhs))


def _sb_kernel(*refs, n_rhs, tq, bk, scale):
    q_ref, k_ref, v_ref, tri_ref, h_ref = refs[:5]
    w_refs, (o_ref, gate_ref, wb_ref) = refs[5:5 + n_rhs], refs[5 + n_rhs:]

    @pl.when(pl.program_id(1) == 0)
    def _():
        for r, w_ref in enumerate(w_refs):
            wb_ref[:, r * W_COL:(r + 1) * W_COL] = w_ref[...].astype(BF16)

    seq = q_ref.shape[1]
    heads = [slice(h * HEAD_DIM, (h + 1) * HEAD_DIM) for h in range(q_ref.shape[2] // HEAD_DIM)]
    tri = tri_ref[...]
    row = lax.broadcasted_iota(jnp.int32, (tq, bk), 0)
    col = lax.broadcasted_iota(jnp.int32, (tq, bk), 1)
    gate_pieces = [(slice(r0, r0 + SUB_ROWS), slice(c0, c0 + SUB_ROWS))
                   for r0 in range(0, seq, SUB_ROWS) for c0 in range(0, n_rhs * W_COL, SUB_ROWS)]
    n_blocks = sum((i + 1) * tq // bk for i in range(seq // tq))
    pieces_per_block = -(-len(gate_pieces) // n_blocks)

    def emit_gate_pieces(count):
        for _ in range(min(count, len(gate_pieces))):
            g_rows, g_cols = gate_pieces.pop(0)
            gate_ref[0, g_rows, g_cols] = _dot(h_ref[0, g_rows, :], wb_ref[:, g_cols]).astype(gate_ref.dtype)

    for i in range(seq // tq):
        n_keys = (i + 1) * tq
        rows = slice(i * tq, (i + 1) * tq)
        z_all = [_dot_nt(q_ref[0, rows, hd], k_ref[0, :n_keys, hd]) * (scale * LOG2E) for hd in heads]
        run = [jnp.zeros((tq, 1), F32) for _ in heads]
        weights = [[None] * (n_keys // bk) for _ in heads]
        for jb in reversed(range(n_keys // bk)):
            emit_gate_pieces(pieces_per_block)
            on_diagonal = (jb + 1) * bk > i * tq
            keep = col + (jb * bk - i * tq) < row
            z = [za[:, jb * bk:(jb + 1) * bk] for za in z_all]
            sp = [jnp.maximum(zh, 0.0) + jnp.log(1.0 + jnp.exp2(-jnp.abs(zh))) * LOG2E for zh in z]
            own = [zh - sh for zh, sh in zip(z, sp)]
            if on_diagonal:
                sp = [jnp.where(keep, sh, 0.0) for sh in sp]
            sp16 = [sh.astype(BF16) for sh in sp]
            later = [_dot(sh, tri) for sh in sp16]
            a = [jnp.exp2(oh - (rh + lh)) for oh, rh, lh in zip(own, run, later)]
            if on_diagonal:
                a = [jnp.where(keep, ah, 0.0) for ah in a]
            for h, ah in enumerate(a):
                weights[h][jb] = ah.astype(BF16)
            run = [rh + lh[:, 0:1] + sh[:, 0:1].astype(F32) for rh, lh, sh in zip(run, later, sp16)]
        for h, hd in enumerate(heads):
            a_all = weights[h][0] if len(weights[h]) == 1 else jnp.concatenate(weights[h], axis=1)
            o_ref[0, rows, hd] = _dot(a_all, v_ref[0, :n_keys, hd]).astype(o_ref.dtype)
    emit_gate_pieces(len(gate_pieces))


def _sb_attention_and_gate(qkv, tri, h, w_in, layer, gate_col0, gate_width, tq, bk, heads_per_step=2):
    b, s, k = h.shape
    n_steps = SB_HEADS // heads_per_step
    n_rhs = gate_width // (n_steps * W_COL)
    assert n_rhs * n_steps * W_COL == gate_width
    tn = n_rhs * W_COL
    blk = (1, s, heads_per_step * HEAD_DIM)

    def w_spec(r):
        return pl.BlockSpec((None, k, W_COL), lambda p, i: (layer, 0, gate_col0 + p * n_rhs + r),
                            pipeline_mode=pl.Buffered(1))

    return pl.pallas_call(
        functools.partial(_sb_kernel, n_rhs=n_rhs, tq=tq, bk=bk, scale=HEAD_DIM ** -0.5),
        out_shape=(jax.ShapeDtypeStruct((b, s, SB_W), BF16), jax.ShapeDtypeStruct((b, s, gate_width), BF16)),
        grid=(n_steps, b),
        in_specs=[pl.BlockSpec(blk, lambda p, i: (i, 0, p)),
                  pl.BlockSpec(blk, lambda p, i: (i, 0, n_steps + p)),
                  pl.BlockSpec(blk, lambda p, i: (i, 0, 2 * n_steps + p)),
                  pl.BlockSpec((bk, bk), lambda p, i: (0, 0)),
                  pl.BlockSpec((1, s, k), lambda p, i: (i, 0, 0))] + [w_spec(r) for r in range(n_rhs)],
        out_specs=(pl.BlockSpec(blk, lambda p, i: (i, 0, p)), pl.BlockSpec((1, s, tn), lambda p, i: (i, 0, p))),
        scratch_shapes=[pltpu.VMEM((k, tn), BF16)],
        compiler_params=_params(2, 60),
        name="sb_attn_gate",
    )(qkv, qkv, qkv, tri, h, *([w_in] * n_rhs))


def _dil_kernel(*refs, n_sub, has_prev, dilation, slopes, scale, n_other):
    refs = list(refs)
    q_ref, kc_ref = refs.pop(0), refs.pop(0)
    kp_ref = refs.pop(0) if has_prev else None
    vc_ref = refs.pop(0)
    vp_ref = refs.pop(0) if has_prev else None
    others = [(refs.pop(0), refs.pop(0)) for _ in range(n_other)]
    o_ref = refs.pop(0)
    lse_ref = None if n_other else refs.pop(0)
    o_slab, lse_slab = refs

    n_keys = 2 * BLOCK if has_prev else BLOCK
    a_idx = lax.broadcasted_iota(jnp.int32, (BLOCK, n_keys), 0)
    b_idx = lax.broadcasted_iota(jnp.int32, (BLOCK, n_keys), 1)
    rel = a_idx + (n_keys - BLOCK) - b_idx
    in_window = (rel >= 0) & (rel <= BLOCK)
    dist = (dilation * rel).astype(F32)
    step = jnp.zeros((BLOCK, n_keys), jnp.int32) + pl.program_id(1)
    valid_first = in_window & ((b_idx >= BLOCK) | (step > 0)) if has_prev else in_window

    def operands(r, j, h):
        rows = slice(j * BLOCK, (j + 1) * BLOCK)
        cols = slice(h * HEAD_DIM, (h + 1) * HEAD_DIM)
        q, kk, vv = q_ref[0, r, rows, cols], kc_ref[0, r, rows, cols], vc_ref[0, r, rows, cols]
        if has_prev:
            if j == 0:
                k_prev, v_prev = kp_ref[0, r, :, cols], vp_ref[0, r, :, cols]
            else:
                prev_rows = slice((j - 1) * BLOCK, j * BLOCK)
                k_prev, v_prev = kc_ref[0, r, prev_rows, cols], vc_ref[0, r, prev_rows, cols]
            kk = jnp.concatenate([k_prev, kk], axis=0)
            vv = jnp.concatenate([v_prev, vv], axis=0)
        return q, kk, vv

    blocks = [(r, j, h) for r in range(dilation) for j in range(n_sub) for h in range(DIL_HEADS_PER_GROUP)]
    for c0 in range(0, len(blocks), DIL_CHAINS):
        chunk = blocks[c0:c0 + DIL_CHAINS]
        ops = [operands(*blk) for blk in chunk]
        s = [jnp.where(valid_first if j == 0 else in_window,
                       _dot_nt(q, kk) * scale - slopes[h] * dist, -jnp.inf)
             for (r, j, h), (q, kk, vv) in zip(chunk, ops)]
        m = [jnp.max(si, axis=-1, keepdims=True) for si in s]
        p = [jnp.exp(si - mi) for si, mi in zip(s, m)]
        l = [jnp.sum(pi, axis=-1, keepdims=True) for pi in p]
        o = [_dot(pi.astype(BF16), vv) / li for pi, li, (q, kk, vv) in zip(p, l, ops)]
        for (r, j, h), oi, mi, li in zip(chunk, o, m, l):
            out_rows = pl.ds(j * BLOCK * dilation + r, BLOCK, stride=dilation)
            o_slab[h, out_rows, :] = oi
            lse_slab[h, out_rows, :] = jnp.broadcast_to(mi + jnp.log(li), (BLOCK, HEAD_DIM))

    for h in range(DIL_HEADS_PER_GROUP):
        cols = slice(h * HEAD_DIM, (h + 1) * HEAD_DIM)
        if not others:
            o_ref[0, :, cols] = o_slab[h].astype(o_ref.dtype)
            lse_ref[0, :, cols] = lse_slab[h]
            continue
        outs = [o_slab[h]] + [og_ref[0, :, cols].astype(F32) for og_ref, _ in others]
        lses = [lse_slab[h]] + [lg_ref[0, :, cols] for _, lg_ref in others]
        top = functools.reduce(jnp.maximum, lses)
        e = [jnp.exp(lg - top) for lg in lses]
        num = functools.reduce(lambda a, b: a + b, [eg * og for eg, og in zip(e, outs)])
        den = functools.reduce(lambda a, b: a + b, e)
        o_ref[0, :, cols] = (num / den).astype(o_ref.dtype)


def _dil_attention(qkv, batch, group, slopes, n_sub, others=()):
    window, dilation = DIL_GROUPS[group]
    s = qkv.shape[0] // batch
    sub_len = s // dilation
    n_blk = sub_len // BLOCK
    n_sub = min(n_sub, n_blk)
    has_prev = n_blk > 1
    assert window // dilation == BLOCK and sub_len % BLOCK == 0 and n_blk % n_sub == 0
    view = qkv.reshape(batch, dilation, sub_len, 3 * DIL_OUT)
    rows = n_sub * BLOCK

    def cur(col):
        return pl.BlockSpec((1, dilation, rows, DIL_OUT), lambda i, n: (i, 0, n, col))

    def prev(col):
        return pl.BlockSpec((1, dilation, BLOCK, DIL_OUT),
                            lambda i, n: (i, 0, jnp.maximum(n * n_sub - 1, 0), col))

    if has_prev:
        in_specs = [cur(0), cur(1), prev(1), cur(2), prev(2)]
    else:
        in_specs = [cur(0), cur(1), cur(2)]
    args = [view] * len(in_specs)
    tile = pl.BlockSpec((1, dilation * rows, DIL_OUT), lambda i, n: (i, n, 0))
    for o_g, lse_g in others:
        in_specs += [tile, tile]
        args += [o_g, lse_g]
    o_shape = jax.ShapeDtypeStruct((batch, s, DIL_OUT), BF16)
    lse_shape = jax.ShapeDtypeStruct((batch, s, DIL_OUT), F32)
    slab = pltpu.VMEM((DIL_HEADS_PER_GROUP, dilation * rows, HEAD_DIM), F32)
    return pl.pallas_call(
        functools.partial(_dil_kernel, n_sub=n_sub, has_prev=has_prev, dilation=dilation,
                          slopes=slopes, scale=HEAD_DIM ** -0.5, n_other=len(others)),
        out_shape=o_shape if others else (o_shape, lse_shape),
        grid=(batch, n_blk // n_sub),
        in_specs=in_specs,
        out_specs=tile if others else (tile, tile),
        scratch_shapes=[slab, slab],
        compiler_params=_params(2, 40),
        name=f"dil_attn_d{dilation}",
    )(*args)


def _post_kernel(osb_ref, odl_ref, gp_ref, bg_ref, x_ref, mod_ref, g2_ref, wsb_ref, wdl_ref, wout_ref,
                 x1_ref, h2_ref):
    d = x_ref.shape[-1]
    chunks = _row_chunks(x_ref.shape[0])
    y_sb = [_dot(osb_ref[c, :], wsb_ref[...]) for c in chunks]
    y_dl = [_dot(odl_ref[c, :], wdl_ref[...]) for c in chunks]
    gates = [_sigmoid(gp_ref[c, :].astype(F32) + bg_ref[...]) for c in chunks]
    mixed = [(g[:, :d] * a + g[:, d:] * b).astype(BF16) for g, a, b in zip(gates, y_sb, y_dl)]
    t = [_dot(mx, wout_ref[...]) for mx in mixed]
    for c, tc in zip(chunks, t):
        x1 = x_ref[c, :] + mod_ref[0, 2:3, :] * tc
        x1_ref[c, :] = x1
        h2 = _rms(x1) * g2_ref[...] * (1.0 + mod_ref[0, 4:5, :]) + mod_ref[0, 3:4, :]
        h2_ref[c, :] = h2.astype(h2_ref.dtype)


def _post(o_sb, o_dl, gate_pre, b_gate, x2d, mod3, g2, w_sb, w_dl, w_out, seq, tm=512):
    m, d = x2d.shape

    def row(width):
        return pl.BlockSpec((tm, width), lambda i: (i, 0))

    def whole(arr):
        return pl.BlockSpec(arr.shape, lambda i: (0,) * arr.ndim, pipeline_mode=pl.Buffered(1))

    return pl.pallas_call(
        _post_kernel,
        out_shape=(jax.ShapeDtypeStruct((m, d), F32), jax.ShapeDtypeStruct((m, d), BF16)),
        grid=(m // tm,),
        in_specs=[row(SB_W), row(DIL_OUT), row(2 * d), whole(b_gate), row(d),
                  pl.BlockSpec((1, N_MOD, d), lambda i: ((i * tm) // seq, 0, 0)),
                  whole(g2), whole(w_sb), whole(w_dl), whole(w_out)],
        out_specs=(row(d), row(d)),
        compiler_params=_params(1, 60),
        name="post",
    )(o_sb, o_dl, gate_pre, b_gate, x2d, mod3, g2, w_sb, w_dl, w_out)


def _swiglu_kernel(h_ref, wg_ref, wu_ref, o_ref, wgb_ref, wub_ref):
    @pl.when(pl.program_id(1) == 0)
    def _():
        wgb_ref[...] = wg_ref[...].astype(BF16)
        wub_ref[...] = wu_ref[...].astype(BF16)

    h = h_ref[...]
    g = _dot(h, wgb_ref[...])
    u = _dot(h, wub_ref[...])
    o_ref[...] = (g * _sigmoid(g) * u).astype(o_ref.dtype)


def _swiglu(h, wg, wu, layer, tm=2048, tn=512):
    m, k = h.shape
    n = wg.shape[-1]
    w_spec = pl.BlockSpec((None, k, tn), lambda j, i: (layer, 0, j))
    return pl.pallas_call(
        _swiglu_kernel,
        out_shape=jax.ShapeDtypeStruct((m, n), BF16),
        grid=(n // tn, m // tm),
        in_specs=[pl.BlockSpec((tm, k), lambda j, i: (i, 0)), w_spec, w_spec],
        out_specs=pl.BlockSpec((tm, tn), lambda j, i: (i, j)),
        scratch_shapes=[pltpu.VMEM((k, tn), BF16), pltpu.VMEM((k, tn), BF16)],
        compiler_params=_params(2, 60),
        name="swiglu",
    )(h, wg, wu)


def _down_kernel(a_ref, w_ref, x1_ref, mod_ref, gf_ref, o_ref, *, final_norm):
    for c in _row_chunks(a_ref.shape[0]):
        x2 = x1_ref[c, :] + mod_ref[0, 5:6, :] * _dot(a_ref[c, :], w_ref[...])
        o_ref[c, :] = _rms(x2) * gf_ref[...] if final_norm else x2


def _down(a, w, x1, mod3, g_final, seq, final_norm, tm=512):
    m, f = a.shape
    d = w.shape[1]
    return pl.pallas_call(
        functools.partial(_down_kernel, final_norm=final_norm),
        out_shape=jax.ShapeDtypeStruct((m, d), F32),
        grid=(m // tm,),
        in_specs=[pl.BlockSpec((tm, f), lambda i: (i, 0)),
                  pl.BlockSpec((f, d), lambda i: (0, 0), pipeline_mode=pl.Buffered(1)),
                  pl.BlockSpec((tm, d), lambda i: (i, 0)),
                  pl.BlockSpec((1, N_MOD, d), lambda i: ((i * tm) // seq, 0, 0)),
                  pl.BlockSpec((1, d), lambda i: (0, 0))],
        out_specs=pl.BlockSpec((tm, d), lambda i: (i, 0)),
        compiler_params=_params(1, 60),
        name="down",
    )(a, w, x1, mod3, g_final)


def _alibi_slopes():
    return tuple(2.0 ** (-8.0 * (i + 1.0) / DIL_HEADS) for i in range(DIL_HEADS))


def kernel(x, c, w_ada, b_ada, g_norm1, g_norm2, g_final, w_in, b_gate, w_proj_sb, w_proj_dil,
           w_out, w_ffn_gate, w_ffn_up, w_ffn_down):
    batch, seq, d = x.shape
    tokens = batch * seq
    n_layers = w_ada.shape[0]
    slopes = _alibi_slopes()
    n_groups = len(DIL_GROUPS)
    sb_tile = 256
    idx = jnp.arange(sb_tile)
    tri = (idx[:, None] > idx[None, :]).astype(BF16)
    c_pad = jnp.pad(c, ((0, ADA_ROWS - batch), (0, 0)))
    perm_dils = tuple(sorted(dil for _, dil in DIL_GROUPS if dil > 1))
    dl_col0 = 3 * SB_W // W_COL
    gate_col0 = dl_col0 + 3 * DIL_W // W_COL
    assert DIL_OUT == W_COL
    group_order = sorted(range(n_groups), key=lambda g: -DIL_GROUPS[g][1])
    assert DIL_GROUPS[group_order[-1]][1] == 1

    for l in range(n_layers):
        mod3 = _ada(c_pad, w_ada[l], b_ada[l][None, :]).reshape(ADA_ROWS, N_MOD, d)

        h_all = _norm1(x, g_norm1[l][None, :], mod3, perm_dils)
        h = h_all[0].reshape(tokens, d)
        h_perm = dict(zip(perm_dils, (hp.reshape(tokens, d) for hp in h_all[1:])))

        qkv_sb = _proj(h, w_in, l, lambda j, r: 2 * j + r, 3 * SB_W // (2 * W_COL), 2, "proj_sb")
        o_sb, gate_pre = _sb_attention_and_gate(qkv_sb.reshape(batch, seq, 3 * SB_W), tri, h_all[0], w_in, l,
                                                gate_col0, 2 * d, sb_tile, sb_tile)
        gate_pre = gate_pre.reshape(tokens, 2 * d)

        partial_groups = []
        for g in group_order:
            dil = DIL_GROUPS[g][1]
            qkv_g = _proj(h_perm.get(dil, h), w_in, l,
                          lambda j, r, g=g: dl_col0 + r * n_groups + g, 1, 3, f"proj_dil_d{dil}")
            g_slopes = slopes[g * DIL_HEADS_PER_GROUP:(g + 1) * DIL_HEADS_PER_GROUP]
            if g != group_order[-1]:
                partial_groups.append(_dil_attention(qkv_g, batch, g, g_slopes, n_sub=4))
            else:
                o_dl = _dil_attention(qkv_g, batch, g, g_slopes, n_sub=4, others=tuple(partial_groups))

        x1, h2 = _post(o_sb.reshape(tokens, SB_W), o_dl.reshape(tokens, DIL_OUT), gate_pre,
                       b_gate[l][None, :], x.reshape(tokens, d), mod3, g_norm2[l][None, :],
                       w_proj_sb[l].astype(BF16), w_proj_dil[l].astype(BF16), w_out[l].astype(BF16), seq)

        a = _swiglu(h2, w_ffn_gate, w_ffn_up, l)
        x = _down(a, w_ffn_down[l].astype(BF16), x1, mod3, g_final[None, :], seq,
                  final_norm=(l == n_layers - 1)).reshape(batch, seq, d)
    return x
```

```python
import functools

import jax
import jax.numpy as jnp
from jax import lax
from jax.experimental import pallas as pl
from jax.experimental.pallas import tpu as pltpu

F32 = jnp.float32
BF16 = jnp.bfloat16

HEAD_DIM = 128
SB_HEADS = 8
DIL_GROUPS = ((128, 1), (512, 4), (2048, 16))
DIL_HEADS_PER_GROUP = 4
DIL_HEADS = DIL_HEADS_PER_GROUP * len(DIL_GROUPS)
BLOCK = 128
SB_W = SB_HEADS * HEAD_DIM
DIL_W = DIL_HEADS * HEAD_DIM
DIL_OUT = DIL_HEADS_PER_GROUP * HEAD_DIM
N_MOD = 6
EPS = 1e-6
LOG2E = 1.4426950408889634
LN2 = 0.6931471805599453

LANES = 128
MIB = 1024 * 1024
ADA_ROWS = 8
W_COL = 512
SUB_ROWS = 256
DIL_CHAINS = 8
LSE_LANES = LANES // DIL_HEADS_PER_GROUP
assert DIL_CHAINS % DIL_HEADS_PER_GROUP == 0


def _params(n_axes, vmem_mib):
    return pltpu.CompilerParams(dimension_semantics=("arbitrary",) * n_axes,
                                vmem_limit_bytes=vmem_mib * MIB)


def _dot(a, b):
    return jnp.dot(a, b, preferred_element_type=F32)


def _dot_nt(a, b):
    return lax.dot_general(a, b, (((1,), (1,)), ((), ())), preferred_element_type=F32)


def _sigmoid(x):
    return 1.0 / (1.0 + jnp.exp(-x))


def _rms(x):
    return x * lax.rsqrt(jnp.mean(x * x, axis=-1, keepdims=True) + EPS)


def _row_chunks(rows, size=SUB_ROWS):
    return [slice(r, r + size) for r in range(0, rows, size)]


def _ada_kernel(c_ref, w_ref, b_ref, o_ref):
    c = c_ref[...]
    s = (c * _sigmoid(c)).astype(BF16)
    o_ref[...] = _dot(s, w_ref[...].astype(BF16)) + b_ref[...]


def _ada(c_pad, w, b, tn=1024):
    d, n = w.shape
    return pl.pallas_call(
        _ada_kernel,
        out_shape=jax.ShapeDtypeStruct((ADA_ROWS, n), F32),
        grid=(n // tn,),
        in_specs=[pl.BlockSpec((ADA_ROWS, d), lambda j: (0, 0)),
                  pl.BlockSpec((d, tn), lambda j: (0, j)),
                  pl.BlockSpec((1, tn), lambda j: (0, j))],
        out_specs=pl.BlockSpec((ADA_ROWS, tn), lambda j: (0, j)),
        compiler_params=_params(1, 40),
        name="ada",
    )(c_pad, w, b)


def _norm1_kernel(x_ref, g_ref, mod_ref, o_ref, *rest, dils):
    n_perm = len(dils)
    perm_refs, slabs = rest[:n_perm], rest[n_perm:]
    ts, d_model = x_ref.shape[1], x_ref.shape[2]
    y = _rms(x_ref[0]) * g_ref[...]
    h = y * (1.0 + mod_ref[0, 1:2, :]) + mod_ref[0, 0:1, :]
    o_ref[0] = h.astype(o_ref.dtype)
    n_slab = d_model // LANES
    for s in range(n_slab):
        slabs[0][s] = h[:, s * LANES:(s + 1) * LANES]
    level = {1: slabs[0]}
    for n, (dil, p_ref) in enumerate(zip(dils, perm_refs)):
        base = max(p for p in level if dil % p == 0)
        step, src = dil // base, level[base]
        keep = slabs[n + 1] if n + 1 < len(slabs) else None
        for r in range(dil):
            r_hi, r_lo = divmod(r, base)
            start = r_lo * (ts // base) + r_hi
            for s in range(n_slab):
                rows = src[s, pl.ds(start, ts // dil, stride=step), :]
                p_ref[0, r, :, s * LANES:(s + 1) * LANES] = rows.astype(p_ref.dtype)
                if keep is not None:
                    keep[s, r * (ts // dil):(r + 1) * (ts // dil), :] = rows
        if keep is not None:
            level[dil] = keep


def _norm1(x, g, mod3, dils, ts=512):
    b, s, d = x.shape
    out_shape = [jax.ShapeDtypeStruct((b, s, d), BF16)]
    out_specs = [pl.BlockSpec((1, ts, d), lambda i, j: (i, j, 0))]
    for dil in dils:
        out_shape.append(jax.ShapeDtypeStruct((b, dil, s // dil, d), BF16))
        out_specs.append(pl.BlockSpec((1, dil, ts // dil, d), lambda i, j: (i, 0, j, 0)))
    slab = pltpu.VMEM((d // LANES, ts, LANES), F32)
    return pl.pallas_call(
        functools.partial(_norm1_kernel, dils=dils),
        out_shape=out_shape,
        grid=(b, s // ts),
        in_specs=[pl.BlockSpec((1, ts, d), lambda i, j: (i, j, 0)),
                  pl.BlockSpec((1, d), lambda i, j: (0, 0)),
                  pl.BlockSpec((1, N_MOD, d), lambda i, j: (i, 0, 0))],
        out_specs=out_specs,
        scratch_shapes=[slab] * max(1, len(dils)),
        compiler_params=_params(2, 40),
        name="norm1",
    )(x, g, mod3)


def _proj_kernel(*refs, n_rhs):
    h_ref, w_refs, o_ref, wb_ref = refs[0], refs[1:1 + n_rhs], refs[1 + n_rhs], refs[2 + n_rhs]

    @pl.when(pl.program_id(1) == 0)
    def _():
        for r, w_ref in enumerate(w_refs):
            wb_ref[:, r * W_COL:(r + 1) * W_COL] = w_ref[...].astype(BF16)

    for r in range(n_rhs):
        cols = slice(r * W_COL, (r + 1) * W_COL)
        o_ref[:, cols] = _dot(h_ref[...], wb_ref[:, cols]).astype(o_ref.dtype)


def _proj(h, w_in, layer, col_fn, n_tiles, n_rhs, name, tm=2048):
    m, k = h.shape
    tn = n_rhs * W_COL
    mode = pl.Buffered(1) if n_tiles == 1 else None

    def w_spec(r):
        return pl.BlockSpec((None, k, W_COL), lambda j, i: (layer, 0, col_fn(j, r)), pipeline_mode=mode)

    return pl.pallas_call(
        functools.partial(_proj_kernel, n_rhs=n_rhs),
        out_shape=jax.ShapeDtypeStruct((m, n_tiles * tn), BF16),
        grid=(n_tiles, m // tm),
        in_specs=[pl.BlockSpec((tm, k), lambda j, i: (i, 0))] + [w_spec(r) for r in range(n_rhs)],
        out_specs=pl.BlockSpec((tm, tn), lambda j, i: (i, j)),
        scratch_shapes=[pltpu.VMEM((k, tn), BF16)],
        compiler_params=_params(2, 56),
        name=name,
    )(h, *([w_in] * n_rhs))


def _sb_kernel(*refs, n_rhs, tq, bk, scale):
    q_ref, k_ref, v_ref, tri_ref, h_ref = refs[:5]
    w_refs, (o_ref, gate_ref, wb_ref) = refs[5:5 + n_rhs], refs[5 + n_rhs:]

    @pl.when(pl.program_id(1) == 0)
    def _():
        for r, w_ref in enumerate(w_refs):
            wb_ref[:, r * W_COL:(r + 1) * W_COL] = w_ref[...].astype(BF16)

    seq = q_ref.shape[1]
    heads = [slice(h * HEAD_DIM, (h + 1) * HEAD_DIM) for h in range(q_ref.shape[2] // HEAD_DIM)]
    tri = tri_ref[...]
    row = lax.broadcasted_iota(jnp.int32, (tq, bk), 0)
    col = lax.broadcasted_iota(jnp.int32, (tq, bk), 1)
    gate_pieces = [(slice(r0, r0 + SUB_ROWS), slice(c0, c0 + SUB_ROWS))
                   for r0 in range(0, seq, SUB_ROWS) for c0 in range(0, n_rhs * W_COL, SUB_ROWS)]
    n_blocks = sum((i + 1) * tq // bk for i in range(seq // tq))
    pieces_per_block = -(-len(gate_pieces) // n_blocks)

    def emit_gate_pieces(count):
        for _ in range(min(count, len(gate_pieces))):
            g_rows, g_cols = gate_pieces.pop(0)
            gate_ref[0, g_rows, g_cols] = _dot(h_ref[0, g_rows, :], wb_ref[:, g_cols]).astype(gate_ref.dtype)

    for i in range(seq // tq):
        n_keys = (i + 1) * tq
        rows = slice(i * tq, (i + 1) * tq)
        z_all = [_dot_nt(q_ref[0, rows, hd], k_ref[0, :n_keys, hd]) * (scale * LOG2E) for hd in heads]
        run = [jnp.zeros((tq, 1), F32) for _ in heads]
        weights = [[None] * (n_keys // bk) for _ in heads]
        for jb in reversed(range(n_keys // bk)):
            emit_gate_pieces(pieces_per_block)
            on_diagonal = (jb + 1) * bk > i * tq
            keep = col + (jb * bk - i * tq) < row
            z = [za[:, jb * bk:(jb + 1) * bk] for za in z_all]
            sp = [jnp.maximum(zh, 0.0) + jnp.log(1.0 + jnp.exp2(-jnp.abs(zh))) * LOG2E for zh in z]
            own = [zh - sh for zh, sh in zip(z, sp)]
            if on_diagonal:
                sp = [jnp.where(keep, sh, 0.0) for sh in sp]
            sp16 = [sh.astype(BF16) for sh in sp]
            later = [_dot(sh, tri) for sh in sp16]
            a = [jnp.exp2(oh - (rh + lh)) for oh, rh, lh in zip(own, run, later)]
            if on_diagonal:
                a = [jnp.where(keep, ah, 0.0) for ah in a]
            for h, ah in enumerate(a):
                weights[h][jb] = ah.astype(BF16)
            run = [rh + lh[:, 0:1] + sh[:, 0:1].astype(F32) for rh, lh, sh in zip(run, later, sp16)]
        for h, hd in enumerate(heads):
            a_all = weights[h][0] if len(weights[h]) == 1 else jnp.concatenate(weights[h], axis=1)
            o_ref[0, rows, hd] = _dot(a_all, v_ref[0, :n_keys, hd]).astype(o_ref.dtype)
    emit_gate_pieces(len(gate_pieces))


def _sb_attention_and_gate(qkv, tri, h, w_in, layer, gate_col0, gate_width, tq, bk, heads_per_step=2):
    b, s, k = h.shape
    n_steps = SB_HEADS // heads_per_step
    n_rhs = gate_width // (n_steps * W_COL)
    assert n_rhs * n_steps * W_COL == gate_width
    tn = n_rhs * W_COL
    blk = (1, s, heads_per_step * HEAD_DIM)

    def w_spec(r):
        return pl.BlockSpec((None, k, W_COL), lambda p, i: (layer, 0, gate_col0 + p * n_rhs + r),
                            pipeline_mode=pl.Buffered(1))

    return pl.pallas_call(
        functools.partial(_sb_kernel, n_rhs=n_rhs, tq=tq, bk=bk, scale=HEAD_DIM ** -0.5),
        out_shape=(jax.ShapeDtypeStruct((b, s, SB_W), BF16), jax.ShapeDtypeStruct((b, s, gate_width), BF16)),
        grid=(n_steps, b),
        in_specs=[pl.BlockSpec(blk, lambda p, i: (i, 0, p)),
                  pl.BlockSpec(blk, lambda p, i: (i, 0, n_steps + p)),
                  pl.BlockSpec(blk, lambda p, i: (i, 0, 2 * n_steps + p)),
                  pl.BlockSpec((bk, bk), lambda p, i: (0, 0)),
                  pl.BlockSpec((1, s, k), lambda p, i: (i, 0, 0))] + [w_spec(r) for r in range(n_rhs)],
        out_specs=(pl.BlockSpec(blk, lambda p, i: (i, 0, p)), pl.BlockSpec((1, s, tn), lambda p, i: (i, 0, p))),
        scratch_shapes=[pltpu.VMEM((k, tn), BF16)],
        compiler_params=_params(2, 60),
        name="sb_attn_gate",
    )(qkv, qkv, qkv, tri, h, *([w_in] * n_rhs))


def _dil_kernel(*refs, n_sub, has_prev, dilation, slopes, scale, n_other):
    refs = list(refs)
    q_ref, kc_ref = refs.pop(0), refs.pop(0)
    kp_ref = refs.pop(0) if has_prev else None
    vc_ref = refs.pop(0)
    vp_ref = refs.pop(0) if has_prev else None
    others = [(refs.pop(0), refs.pop(0)) for _ in range(n_other)]
    o_ref = refs.pop(0)
    lse_ref = None if n_other else refs.pop(0)
    o_slab, lse_slab = refs

    n_keys = 2 * BLOCK if has_prev else BLOCK
    a_idx = lax.broadcasted_iota(jnp.int32, (BLOCK, n_keys), 0)
    b_idx = lax.broadcasted_iota(jnp.int32, (BLOCK, n_keys), 1)
    rel = a_idx + (n_keys - BLOCK) - b_idx
    in_window = (rel >= 0) & (rel <= BLOCK)
    dist = (dilation * rel).astype(F32)
    bias = [(slope * LOG2E) * dist for slope in slopes]
    lane = lax.broadcasted_iota(jnp.int32, (BLOCK, LANES), 1)
    step = jnp.zeros((BLOCK, n_keys), jnp.int32) + pl.program_id(1)
    valid_first = in_window & ((b_idx >= BLOCK) | (step > 0)) if has_prev else in_window

    def operands(r, j, h):
        rows = slice(j * BLOCK, (j + 1) * BLOCK)
        cols = slice(h * HEAD_DIM, (h + 1) * HEAD_DIM)
        q, kk, vv = q_ref[0, r, rows, cols], kc_ref[0, r, rows, cols], vc_ref[0, r, rows, cols]
        if has_prev:
            if j == 0:
                k_prev, v_prev = kp_ref[0, r, :, cols], vp_ref[0, r, :, cols]
            else:
                prev_rows = slice((j - 1) * BLOCK, j * BLOCK)
                k_prev, v_prev = kc_ref[0, r, prev_rows, cols], vc_ref[0, r, prev_rows, cols]
            kk = jnp.concatenate([k_prev, kk], axis=0)
            vv = jnp.concatenate([v_prev, vv], axis=0)
        return q, kk, vv

    blocks = [(r, j, h) for r in range(dilation) for j in range(n_sub) for h in range(DIL_HEADS_PER_GROUP)]
    for c0 in range(0, len(blocks), DIL_CHAINS):
        chunk = blocks[c0:c0 + DIL_CHAINS]
        ops = [operands(*blk) for blk in chunk]
        s = [jnp.where(valid_first if j == 0 else in_window,
                       _dot_nt(q, kk) * (scale * LOG2E) - bias[h], -jnp.inf)
             for (r, j, h), (q, kk, vv) in zip(chunk, ops)]
        m = [jnp.max(si, axis=-1, keepdims=True) for si in s]
        p = [jnp.exp2(si - mi) for si, mi in zip(s, m)]
        l = [jnp.sum(pi, axis=-1, keepdims=True) for pi in p]
        o = [_dot(pi.astype(BF16), vv) / li for pi, li, (q, kk, vv) in zip(p, l, ops)]
        lse = [mi * LN2 + jnp.log(li) for mi, li in zip(m, l)]
        for (r, j, h), oi in zip(chunk, o):
            o_slab[h, pl.ds(j * BLOCK * dilation + r, BLOCK, stride=dilation), :] = oi
        for g0 in range(0, len(chunk), DIL_HEADS_PER_GROUP):
            r, j, _ = chunk[g0]
            packed = lse[g0 + DIL_HEADS_PER_GROUP - 1]
            for h in reversed(range(DIL_HEADS_PER_GROUP - 1)):
                packed = jnp.where(lane < (h + 1) * LSE_LANES, lse[g0 + h], packed)
            lse_slab[pl.ds(j * BLOCK * dilation + r, BLOCK, stride=dilation), :] = packed

    if not others:
        lse_ref[0] = lse_slab[...]
    for h in range(DIL_HEADS_PER_GROUP):
        cols = slice(h * HEAD_DIM, (h + 1) * HEAD_DIM)
        if not others:
            o_ref[0, :, cols] = o_slab[h].astype(o_ref.dtype)
            continue
        lse_col = slice(h * LSE_LANES, h * LSE_LANES + 1)
        outs = [o_slab[h]] + [og_ref[0, :, cols].astype(F32) for og_ref, _ in others]
        lses = [lse_slab[:, lse_col]] + [lg_ref[0, :, lse_col] for _, lg_ref in others]
        top = functools.reduce(jnp.maximum, lses)
        e = [jnp.exp(lg - top) for lg in lses]
        inv = 1.0 / functools.reduce(lambda a, b: a + b, e)
        num = functools.reduce(lambda a, b: a + b, [(eg * inv) * og for eg, og in zip(e, outs)])
        o_ref[0, :, cols] = num.astype(o_ref.dtype)


def _dil_attention(qkv, batch, group, slopes, n_sub, others=()):
    window, dilation = DIL_GROUPS[group]
    s = qkv.shape[0] // batch
    sub_len = s // dilation
    n_blk = sub_len // BLOCK
    n_sub = min(n_sub, n_blk)
    has_prev = n_blk > 1
    assert window // dilation == BLOCK and sub_len % BLOCK == 0 and n_blk % n_sub == 0
    view = qkv.reshape(batch, dilation, sub_len, 3 * DIL_OUT)
    rows = n_sub * BLOCK

    def cur(col):
        return pl.BlockSpec((1, dilation, rows, DIL_OUT), lambda i, n: (i, 0, n, col))

    def prev(col):
        return pl.BlockSpec((1, dilation, BLOCK, DIL_OUT),
                            lambda i, n: (i, 0, jnp.maximum(n * n_sub - 1, 0), col))

    if has_prev:
        in_specs = [cur(0), cur(1), prev(1), cur(2), prev(2)]
    else:
        in_specs = [cur(0), cur(1), cur(2)]
    args = [view] * len(in_specs)
    tile = pl.BlockSpec((1, dilation * rows, DIL_OUT), lambda i, n: (i, n, 0))
    lse_tile = pl.BlockSpec((1, dilation * rows, LANES), lambda i, n: (i, n, 0))
    for o_g, lse_g in others:
        in_specs += [tile, lse_tile]
        args += [o_g, lse_g]
    o_shape = jax.ShapeDtypeStruct((batch, s, DIL_OUT), BF16)
    lse_shape = jax.ShapeDtypeStruct((batch, s, LANES), F32)
    return pl.pallas_call(
        functools.partial(_dil_kernel, n_sub=n_sub, has_prev=has_prev, dilation=dilation,
                          slopes=slopes, scale=HEAD_DIM ** -0.5, n_other=len(others)),
        out_shape=o_shape if others else (o_shape, lse_shape),
        grid=(batch, n_blk // n_sub),
        in_specs=in_specs,
        out_specs=tile if others else (tile, lse_tile),
        scratch_shapes=[pltpu.VMEM((DIL_HEADS_PER_GROUP, dilation * rows, HEAD_DIM), F32),
                        pltpu.VMEM((dilation * rows, LANES), F32)],
        compiler_params=_params(2, 40),
        name=f"dil_attn_d{dilation}",
    )(*args)


def _post_kernel(osb_ref, odl_ref, gp_ref, bg_ref, x_ref, mod_ref, g2_ref, wsb_ref, wdl_ref, wout_ref,
                 x1_ref, h2_ref):
    d = x_ref.shape[-1]
    chunks = _row_chunks(x_ref.shape[0])
    y_sb = [_dot(osb_ref[c, :], wsb_ref[...]) for c in chunks]
    y_dl = [_dot(odl_ref[c, :], wdl_ref[...]) for c in chunks]
    gates = [_sigmoid(gp_ref[c, :].astype(F32) + bg_ref[...]) for c in chunks]
    mixed = [(g[:, :d] * a + g[:, d:] * b).astype(BF16) for g, a, b in zip(gates, y_sb, y_dl)]
    t = [_dot(mx, wout_ref[...]) for mx in mixed]
    for c, tc in zip(chunks, t):
        x1 = x_ref[c, :] + mod_ref[0, 2:3, :] * tc
        x1_ref[c, :] = x1
        h2 = _rms(x1) * g2_ref[...] * (1.0 + mod_ref[0, 4:5, :]) + mod_ref[0, 3:4, :]
        h2_ref[c, :] = h2.astype(h2_ref.dtype)


def _post(o_sb, o_dl, gate_pre, b_gate, x2d, mod3, g2, w_sb, w_dl, w_out, seq, tm=512):
    m, d = x2d.shape

    def row(width):
        return pl.BlockSpec((tm, width), lambda i: (i, 0))

    def whole(arr):
        return pl.BlockSpec(arr.shape, lambda i: (0,) * arr.ndim, pipeline_mode=pl.Buffered(1))

    return pl.pallas_call(
        _post_kernel,
        out_shape=(jax.ShapeDtypeStruct((m, d), F32), jax.ShapeDtypeStruct((m, d), BF16)),
        grid=(m // tm,),
        in_specs=[row(SB_W), row(DIL_OUT), row(2 * d), whole(b_gate), row(d),
                  pl.BlockSpec((1, N_MOD, d), lambda i: ((i * tm) // seq, 0, 0)),
                  whole(g2), whole(w_sb), whole(w_dl), whole(w_out)],
        out_specs=(row(d), row(d)),
        compiler_params=_params(1, 60),
        name="post",
    )(o_sb, o_dl, gate_pre, b_gate, x2d, mod3, g2, w_sb, w_dl, w_out)


def _swiglu_kernel(h_ref, wg_ref, wu_ref, o_ref, wgb_ref, wub_ref):
    @pl.when(pl.program_id(1) == 0)
    def _():
        wgb_ref[...] = wg_ref[...].astype(BF16)
        wub_ref[...] = wu_ref[...].astype(BF16)

    for c in _row_chunks(h_ref.shape[0], 4 * SUB_ROWS):
        g = _dot(h_ref[c, :], wgb_ref[...])
        u = _dot(h_ref[c, :], wub_ref[...])
        o_ref[c, :] = (g * _sigmoid(g) * u).astype(o_ref.dtype)


def _swiglu(h, wg, wu, layer, tm=2048, tn=512):
    m, k = h.shape
    n = wg.shape[-1]
    w_spec = pl.BlockSpec((None, k, tn), lambda j, i: (layer, 0, j))
    return pl.pallas_call(
        _swiglu_kernel,
        out_shape=jax.ShapeDtypeStruct((m, n), BF16),
        grid=(n // tn, m // tm),
        in_specs=[pl.BlockSpec((tm, k), lambda j, i: (i, 0)), w_spec, w_spec],
        out_specs=pl.BlockSpec((tm, tn), lambda j, i: (i, j)),
        scratch_shapes=[pltpu.VMEM((k, tn), BF16), pltpu.VMEM((k, tn), BF16)],
        compiler_params=_params(2, 60),
        name="swiglu",
    )(h, wg, wu)


def _down_kernel(a_ref, w_ref, x1_ref, mod_ref, gf_ref, o_ref, *, final_norm):
    for c in _row_chunks(a_ref.shape[0]):
        x2 = x1_ref[c, :] + mod_ref[0, 5:6, :] * _dot(a_ref[c, :], w_ref[...])
        o_ref[c, :] = _rms(x2) * gf_ref[...] if final_norm else x2


def _down(a, w, x1, mod3, g_final, seq, final_norm, tm=512):
    m, f = a.shape
    d = w.shape[1]
    return pl.pallas_call(
        functools.partial(_down_kernel, final_norm=final_norm),
        out_shape=jax.ShapeDtypeStruct((m, d), F32),
        grid=(m // tm,),
        in_specs=[pl.BlockSpec((tm, f), lambda i: (i, 0)),
                  pl.BlockSpec((f, d), lambda i: (0, 0), pipeline_mode=pl.Buffered(1)),
                  pl.BlockSpec((tm, d), lambda i: (i, 0)),
                  pl.BlockSpec((1, N_MOD, d), lambda i: ((i * tm) // seq, 0, 0)),
                  pl.BlockSpec((1, d), lambda i: (0, 0))],
        out_specs=pl.BlockSpec((tm, d), lambda i: (i, 0)),
        compiler_params=_params(1, 60),
        name="down",
    )(a, w, x1, mod3, g_final)


def _alibi_slopes():
    return tuple(2.0 ** (-8.0 * (i + 1.0) / DIL_HEADS) for i in range(DIL_HEADS))


def kernel(x, c, w_ada, b_ada, g_norm1, g_norm2, g_final, w_in, b_gate, w_proj_sb, w_proj_dil,
           w_out, w_ffn_gate, w_ffn_up, w_ffn_down):
    batch, seq, d = x.shape
    tokens = batch * seq
    n_layers = w_ada.shape[0]
    slopes = _alibi_slopes()
    n_groups = len(DIL_GROUPS)
    sb_tile = 256
    idx = jnp.arange(sb_tile)
    tri = (idx[:, None] > idx[None, :]).astype(BF16)
    c_pad = jnp.pad(c, ((0, ADA_ROWS - batch), (0, 0)))
    perm_dils = tuple(sorted(dil for _, dil in DIL_GROUPS if dil > 1))
    dl_col0 = 3 * SB_W // W_COL
    gate_col0 = dl_col0 + 3 * DIL_W // W_COL
    assert DIL_OUT == W_COL
    group_order = sorted(range(n_groups), key=lambda g: -DIL_GROUPS[g][1])
    assert DIL_GROUPS[group_order[-1]][1] == 1

    for l in range(n_layers):
        mod3 = _ada(c_pad, w_ada[l], b_ada[l][None, :]).reshape(ADA_ROWS, N_MOD, d)

        h_all = _norm1(x, g_norm1[l][None, :], mod3, perm_dils)
        h = h_all[0].reshape(tokens, d)
        h_perm = dict(zip(perm_dils, (hp.reshape(tokens, d) for hp in h_all[1:])))

        qkv_sb = _proj(h, w_in, l, lambda j, r: 2 * j + r, 3 * SB_W // (2 * W_COL), 2, "proj_sb")
        o_sb, gate_pre = _sb_attention_and_gate(qkv_sb.reshape(batch, seq, 3 * SB_W), tri, h_all[0], w_in, l,
                                                gate_col0, 2 * d, sb_tile, sb_tile)
        gate_pre = gate_pre.reshape(tokens, 2 * d)

        partial_groups = []
        for g in group_order:
            dil = DIL_GROUPS[g][1]
            qkv_g = _proj(h_perm.get(dil, h), w_in, l,
                          lambda j, r, g=g: dl_col0 + r * n_groups + g, 1, 3, f"proj_dil_d{dil}")
            g_slopes = slopes[g * DIL_HEADS_PER_GROUP:(g + 1) * DIL_HEADS_PER_GROUP]
            if g != group_order[-1]:
                partial_groups.append(_dil_attention(qkv_g, batch, g, g_slopes, n_sub=4))
            else:
                o_dl = _dil_attention(qkv_g, batch, g, g_slopes, n_sub=4, others=tuple(partial_groups))

        x1, h2 = _post(o_sb.reshape(tokens, SB_W), o_dl.reshape(tokens, DIL_OUT), gate_pre,
                       b_gate[l][None, :], x.reshape(tokens, d), mod3, g_norm2[l][None, :],
                       w_proj_sb[l].astype(BF16), w_proj_dil[l].astype(BF16), w_out[l].astype(BF16), seq)

        a = _swiglu(h2, w_ffn_gate, w_ffn_up, l)
        x = _down(a, w_ffn_down[l].astype(BF16), x1, mod3, g_final[None, :], seq,
                  final_norm=(l == n_layers - 1)).reshape(batch, seq, d)
    return x
```

```python
import functools

import jax
import jax.numpy as jnp
from jax import lax
from jax.experimental import pallas as pl
from jax.experimental.pallas import tpu as pltpu

F32 = jnp.float32
BF16 = jnp.bfloat16

HEAD_DIM = 128
SB_HEADS = 8
DIL_GROUPS = ((128, 1), (512, 4), (2048, 16))
DIL_HEADS_PER_GROUP = 4
DIL_HEADS = DIL_HEADS_PER_GROUP * len(DIL_GROUPS)
BLOCK = 128
SB_W = SB_HEADS * HEAD_DIM
DIL_W = DIL_HEADS * HEAD_DIM
DIL_OUT = DIL_HEADS_PER_GROUP * HEAD_DIM
N_MOD = 6
EPS = 1e-6
LOG2E = 1.4426950408889634
LN2 = 0.6931471805599453

LANES = 128
MIB = 1024 * 1024
ADA_ROWS = 8
W_COL = 512
SUB_ROWS = 256
DIL_CHAINS = 8
LSE_LANES = LANES // DIL_HEADS_PER_GROUP
assert DIL_CHAINS % DIL_HEADS_PER_GROUP == 0


def _params(n_axes, vmem_mib):
    return pltpu.CompilerParams(dimension_semantics=("arbitrary",) * n_axes,
                                vmem_limit_bytes=vmem_mib * MIB)


def _dot(a, b):
    return jnp.dot(a, b, preferred_element_type=F32)


def _dot_nt(a, b):
    return lax.dot_general(a, b, (((1,), (1,)), ((), ())), preferred_element_type=F32)


def _sigmoid(x):
    return 1.0 / (1.0 + jnp.exp(-x))


def _rms(x):
    return x * lax.rsqrt(jnp.mean(x * x, axis=-1, keepdims=True) + EPS)


def _row_chunks(rows, size=SUB_ROWS):
    return [slice(r, r + size) for r in range(0, rows, size)]


def _ada_kernel(c_ref, w_ref, b_ref, o_ref):
    c = c_ref[...]
    s = (c * _sigmoid(c)).astype(BF16)
    o_ref[...] = _dot(s, w_ref[...].astype(BF16)) + b_ref[...]


def _ada(c_pad, w, b, tn=1024):
    d, n = w.shape
    return pl.pallas_call(
        _ada_kernel,
        out_shape=jax.ShapeDtypeStruct((ADA_ROWS, n), F32),
        grid=(n // tn,),
        in_specs=[pl.BlockSpec((ADA_ROWS, d), lambda j: (0, 0)),
                  pl.BlockSpec((d, tn), lambda j: (0, j)),
                  pl.BlockSpec((1, tn), lambda j: (0, j))],
        out_specs=pl.BlockSpec((ADA_ROWS, tn), lambda j: (0, j)),
        compiler_params=_params(1, 40),
        name="ada",
    )(c_pad, w, b)


def _norm1_kernel(x_ref, g_ref, mod_ref, o_ref, *rest, dils):
    n_perm = len(dils)
    perm_refs, slabs = rest[:n_perm], rest[n_perm:]
    ts, d_model = x_ref.shape[1], x_ref.shape[2]
    y = _rms(x_ref[0]) * g_ref[...]
    h = y * (1.0 + mod_ref[0, 1:2, :]) + mod_ref[0, 0:1, :]
    o_ref[0] = h.astype(o_ref.dtype)
    n_slab = d_model // LANES
    for s in range(n_slab):
        slabs[0][s] = h[:, s * LANES:(s + 1) * LANES]
    level = {1: slabs[0]}
    for n, (dil, p_ref) in enumerate(zip(dils, perm_refs)):
        base = max(p for p in level if dil % p == 0)
        step, src = dil // base, level[base]
        keep = slabs[n + 1] if n + 1 < len(slabs) else None
        for r in range(dil):
            r_hi, r_lo = divmod(r, base)
            start = r_lo * (ts // base) + r_hi
            for s in range(n_slab):
                rows = src[s, pl.ds(start, ts // dil, stride=step), :]
                p_ref[0, r, :, s * LANES:(s + 1) * LANES] = rows.astype(p_ref.dtype)
                if keep is not None:
                    keep[s, r * (ts // dil):(r + 1) * (ts // dil), :] = rows
        if keep is not None:
            level[dil] = keep


def _norm1(x, g, mod3, dils, ts=512):
    b, s, d = x.shape
    out_shape = [jax.ShapeDtypeStruct((b, s, d), BF16)]
    out_specs = [pl.BlockSpec((1, ts, d), lambda i, j: (i, j, 0))]
    for dil in dils:
        out_shape.append(jax.ShapeDtypeStruct((b, dil, s // dil, d), BF16))
        out_specs.append(pl.BlockSpec((1, dil, ts // dil, d), lambda i, j: (i, 0, j, 0)))
    slab = pltpu.VMEM((d // LANES, ts, LANES), F32)
    return pl.pallas_call(
        functools.partial(_norm1_kernel, dils=dils),
        out_shape=out_shape,
        grid=(b, s // ts),
        in_specs=[pl.BlockSpec((1, ts, d), lambda i, j: (i, j, 0)),
                  pl.BlockSpec((1, d), lambda i, j: (0, 0)),
                  pl.BlockSpec((1, N_MOD, d), lambda i, j: (i, 0, 0))],
        out_specs=out_specs,
        scratch_shapes=[slab] * max(1, len(dils)),
        compiler_params=_params(2, 40),
        name="norm1",
    )(x, g, mod3)


def _proj_kernel(*refs, n_rhs):
    h_ref, w_refs, o_ref, wb_ref = refs[0], refs[1:1 + n_rhs], refs[1 + n_rhs], refs[2 + n_rhs]

    @pl.when(pl.program_id(1) == 0)
    def _():
        for r, w_ref in enumerate(w_refs):
            wb_ref[:, r * W_COL:(r + 1) * W_COL] = w_ref[...].astype(BF16)

    for c in _row_chunks(h_ref.shape[0], 4 * SUB_ROWS):
        for r in range(n_rhs):
            cols = slice(r * W_COL, (r + 1) * W_COL)
            o_ref[c, cols] = _dot(h_ref[c, :], wb_ref[:, cols]).astype(o_ref.dtype)


def _proj(h, w_in, layer, col_fn, n_tiles, n_rhs, name, tm=2048):
    m, k = h.shape
    tn = n_rhs * W_COL
    mode = pl.Buffered(1) if n_tiles == 1 else None

    def w_spec(r):
        return pl.BlockSpec((None, k, W_COL), lambda j, i: (layer, 0, col_fn(j, r)), pipeline_mode=mode)

    return pl.pallas_call(
        functools.partial(_proj_kernel, n_rhs=n_rhs),
        out_shape=jax.ShapeDtypeStruct((m, n_tiles * tn), BF16),
        grid=(n_tiles, m // tm),
        in_specs=[pl.BlockSpec((tm, k), lambda j, i: (i, 0))] + [w_spec(r) for r in range(n_rhs)],
        out_specs=pl.BlockSpec((tm, tn), lambda j, i: (i, j)),
        scratch_shapes=[pltpu.VMEM((k, tn), BF16)],
        compiler_params=_params(2, 56),
        name=name,
    )(h, *([w_in] * n_rhs))


def _sb_kernel(*refs, n_rhs, tq, bk, scale):
    q_ref, k_ref, v_ref, tri_ref, h_ref = refs[:5]
    w_refs, (o_ref, gate_ref, wb_ref) = refs[5:5 + n_rhs], refs[5 + n_rhs:]

    @pl.when(pl.program_id(1) == 0)
    def _():
        for r, w_ref in enumerate(w_refs):
            wb_ref[:, r * W_COL:(r + 1) * W_COL] = w_ref[...].astype(BF16)

    seq = q_ref.shape[1]
    heads = [slice(h * HEAD_DIM, (h + 1) * HEAD_DIM) for h in range(q_ref.shape[2] // HEAD_DIM)]
    tri = tri_ref[...]
    row = lax.broadcasted_iota(jnp.int32, (tq, bk), 0)
    col = lax.broadcasted_iota(jnp.int32, (tq, bk), 1)
    gate_pieces = [(slice(r0, r0 + SUB_ROWS), slice(c0, c0 + SUB_ROWS))
                   for r0 in range(0, seq, SUB_ROWS) for c0 in range(0, n_rhs * W_COL, SUB_ROWS)]
    n_blocks = sum((i + 1) * tq // bk for i in range(seq // tq))
    pieces_per_block = -(-len(gate_pieces) // n_blocks)

    def emit_gate_pieces(count):
        for _ in range(min(count, len(gate_pieces))):
            g_rows, g_cols = gate_pieces.pop(0)
            gate_ref[0, g_rows, g_cols] = _dot(h_ref[0, g_rows, :], wb_ref[:, g_cols]).astype(gate_ref.dtype)

    for i in range(seq // tq):
        n_keys = (i + 1) * tq
        rows = slice(i * tq, (i + 1) * tq)
        z_all = [_dot_nt(q_ref[0, rows, hd], k_ref[0, :n_keys, hd]) * (scale * LOG2E) for hd in heads]
        run = [jnp.zeros((tq, 1), F32) for _ in heads]
        weights = [[None] * (n_keys // bk) for _ in heads]
        for jb in reversed(range(n_keys // bk)):
            emit_gate_pieces(pieces_per_block)
            on_diagonal = (jb + 1) * bk > i * tq
            keep = col + (jb * bk - i * tq) < row
            z = [za[:, jb * bk:(jb + 1) * bk] for za in z_all]
            sp = [jnp.maximum(zh, 0.0) + jnp.log(1.0 + jnp.exp2(-jnp.abs(zh))) * LOG2E for zh in z]
            own = [zh - sh for zh, sh in zip(z, sp)]
            if on_diagonal:
                sp = [jnp.where(keep, sh, 0.0) for sh in sp]
            sp16 = [sh.astype(BF16) for sh in sp]
            later = [_dot(sh, tri) for sh in sp16]
            a = [jnp.exp2(oh - (rh + lh)) for oh, rh, lh in zip(own, run, later)]
            if on_diagonal:
                a = [jnp.where(keep, ah, 0.0) for ah in a]
            for h, ah in enumerate(a):
                weights[h][jb] = ah.astype(BF16)
            run = [rh + lh[:, 0:1] + sh[:, 0:1].astype(F32) for rh, lh, sh in zip(run, later, sp16)]
        for h, hd in enumerate(heads):
            a_all = weights[h][0] if len(weights[h]) == 1 else jnp.concatenate(weights[h], axis=1)
            o_ref[0, rows, hd] = _dot(a_all, v_ref[0, :n_keys, hd]).astype(o_ref.dtype)
    emit_gate_pieces(len(gate_pieces))


def _sb_attention_and_gate(qkv, tri, h, w_in, layer, gate_col0, gate_width, tq, bk, heads_per_step=2):
    b, s, k = h.shape
    n_steps = SB_HEADS // heads_per_step
    n_rhs = gate_width // (n_steps * W_COL)
    assert n_rhs * n_steps * W_COL == gate_width
    tn = n_rhs * W_COL
    blk = (1, s, heads_per_step * HEAD_DIM)

    def w_spec(r):
        return pl.BlockSpec((None, k, W_COL), lambda p, i: (layer, 0, gate_col0 + p * n_rhs + r),
                            pipeline_mode=pl.Buffered(1))

    return pl.pallas_call(
        functools.partial(_sb_kernel, n_rhs=n_rhs, tq=tq, bk=bk, scale=HEAD_DIM ** -0.5),
        out_shape=(jax.ShapeDtypeStruct((b, s, SB_W), BF16), jax.ShapeDtypeStruct((b, s, gate_width), BF16)),
        grid=(n_steps, b),
        in_specs=[pl.BlockSpec(blk, lambda p, i: (i, 0, p)),
                  pl.BlockSpec(blk, lambda p, i: (i, 0, n_steps + p)),
                  pl.BlockSpec(blk, lambda p, i: (i, 0, 2 * n_steps + p)),
                  pl.BlockSpec((bk, bk), lambda p, i: (0, 0)),
                  pl.BlockSpec((1, s, k), lambda p, i: (i, 0, 0))] + [w_spec(r) for r in range(n_rhs)],
        out_specs=(pl.BlockSpec(blk, lambda p, i: (i, 0, p)), pl.BlockSpec((1, s, tn), lambda p, i: (i, 0, p))),
        scratch_shapes=[pltpu.VMEM((k, tn), BF16)],
        compiler_params=_params(2, 60),
        name="sb_attn_gate",
    )(qkv, qkv, qkv, tri, h, *([w_in] * n_rhs))


def _dil_kernel(*refs, n_sub, has_prev, dilation, slopes, scale, n_other):
    refs = list(refs)
    q_ref, kc_ref = refs.pop(0), refs.pop(0)
    kp_ref = refs.pop(0) if has_prev else None
    vc_ref = refs.pop(0)
    vp_ref = refs.pop(0) if has_prev else None
    others = [(refs.pop(0), refs.pop(0)) for _ in range(n_other)]
    o_ref = refs.pop(0)
    lse_ref = None if n_other else refs.pop(0)
    o_slab, lse_slab = refs

    n_keys = 2 * BLOCK if has_prev else BLOCK
    a_idx = lax.broadcasted_iota(jnp.int32, (BLOCK, n_keys), 0)
    b_idx = lax.broadcasted_iota(jnp.int32, (BLOCK, n_keys), 1)
    rel = a_idx + (n_keys - BLOCK) - b_idx
    in_window = (rel >= 0) & (rel <= BLOCK)
    dist = (dilation * rel).astype(F32)
    bias = [(slope * LOG2E) * dist for slope in slopes]
    lane = lax.broadcasted_iota(jnp.int32, (BLOCK, LANES), 1)
    step = jnp.zeros((BLOCK, n_keys), jnp.int32) + pl.program_id(1)
    valid_first = in_window & ((b_idx >= BLOCK) | (step > 0)) if has_prev else in_window

    def operands(r, j, h):
        rows = slice(j * BLOCK, (j + 1) * BLOCK)
        cols = slice(h * HEAD_DIM, (h + 1) * HEAD_DIM)
        q, kk, vv = q_ref[0, r, rows, cols], kc_ref[0, r, rows, cols], vc_ref[0, r, rows, cols]
        if has_prev:
            if j == 0:
                k_prev, v_prev = kp_ref[0, r, :, cols], vp_ref[0, r, :, cols]
            else:
                prev_rows = slice((j - 1) * BLOCK, j * BLOCK)
                k_prev, v_prev = kc_ref[0, r, prev_rows, cols], vc_ref[0, r, prev_rows, cols]
            kk = jnp.concatenate([k_prev, kk], axis=0)
            vv = jnp.concatenate([v_prev, vv], axis=0)
        return q, kk, vv

    blocks = [(r, j, h) for r in range(dilation) for j in range(n_sub) for h in range(DIL_HEADS_PER_GROUP)]
    for c0 in range(0, len(blocks), DIL_CHAINS):
        chunk = blocks[c0:c0 + DIL_CHAINS]
        ops = [operands(*blk) for blk in chunk]
        s = [jnp.where(valid_first if j == 0 else in_window,
                       _dot_nt(q, kk) * (scale * LOG2E) - bias[h], -jnp.inf)
             for (r, j, h), (q, kk, vv) in zip(chunk, ops)]
        m = [jnp.max(si, axis=-1, keepdims=True) for si in s]
        p = [jnp.exp2(si - mi) for si, mi in zip(s, m)]
        l = [jnp.sum(pi, axis=-1, keepdims=True) for pi in p]
        o = [_dot(pi.astype(BF16), vv) / li for pi, li, (q, kk, vv) in zip(p, l, ops)]
        lse = [mi * LN2 + jnp.log(li) for mi, li in zip(m, l)]
        for (r, j, h), oi in zip(chunk, o):
            o_slab[h, pl.ds(j * BLOCK * dilation + r, BLOCK, stride=dilation), :] = oi
        for g0 in range(0, len(chunk), DIL_HEADS_PER_GROUP):
            r, j, _ = chunk[g0]
            packed = lse[g0 + DIL_HEADS_PER_GROUP - 1]
            for h in reversed(range(DIL_HEADS_PER_GROUP - 1)):
                packed = jnp.where(lane < (h + 1) * LSE_LANES, lse[g0 + h], packed)
            lse_slab[pl.ds(j * BLOCK * dilation + r, BLOCK, stride=dilation), :] = packed

    if not others:
        lse_ref[0] = lse_slab[...]
        for h in range(DIL_HEADS_PER_GROUP):
            o_ref[0, :, h * HEAD_DIM:(h + 1) * HEAD_DIM] = o_slab[h].astype(o_ref.dtype)
        return
    lses = [lse_slab[...]] + [lg_ref[0] for _, lg_ref in others]
    top = functools.reduce(jnp.maximum, lses)
    e = [jnp.exp(lg - top) for lg in lses]
    inv = 1.0 / functools.reduce(lambda a, b: a + b, e)
    weights = [eg * inv for eg in e]
    for h in range(DIL_HEADS_PER_GROUP):
        cols = slice(h * HEAD_DIM, (h + 1) * HEAD_DIM)
        lse_col = slice(h * LSE_LANES, h * LSE_LANES + 1)
        outs = [o_slab[h]] + [og_ref[0, :, cols].astype(F32) for og_ref, _ in others]
        mixed = functools.reduce(lambda a, b: a + b, [wg[:, lse_col] * og for wg, og in zip(weights, outs)])
        o_ref[0, :, cols] = mixed.astype(o_ref.dtype)


def _dil_attention(qkv, batch, group, slopes, n_sub, others=()):
    window, dilation = DIL_GROUPS[group]
    s = qkv.shape[0] // batch
    sub_len = s // dilation
    n_blk = sub_len // BLOCK
    n_sub = min(n_sub, n_blk)
    has_prev = n_blk > 1
    assert window // dilation == BLOCK and sub_len % BLOCK == 0 and n_blk % n_sub == 0
    view = qkv.reshape(batch, dilation, sub_len, 3 * DIL_OUT)
    rows = n_sub * BLOCK

    def cur(col):
        return pl.BlockSpec((1, dilation, rows, DIL_OUT), lambda i, n: (i, 0, n, col))

    def prev(col):
        return pl.BlockSpec((1, dilation, BLOCK, DIL_OUT),
                            lambda i, n: (i, 0, jnp.maximum(n * n_sub - 1, 0), col))

    if has_prev:
        in_specs = [cur(0), cur(1), prev(1), cur(2), prev(2)]
    else:
        in_specs = [cur(0), cur(1), cur(2)]
    args = [view] * len(in_specs)
    tile = pl.BlockSpec((1, dilation * rows, DIL_OUT), lambda i, n: (i, n, 0))
    lse_tile = pl.BlockSpec((1, dilation * rows, LANES), lambda i, n: (i, n, 0))
    for o_g, lse_g in others:
        in_specs += [tile, lse_tile]
        args += [o_g, lse_g]
    o_shape = jax.ShapeDtypeStruct((batch, s, DIL_OUT), BF16)
    lse_shape = jax.ShapeDtypeStruct((batch, s, LANES), F32)
    return pl.pallas_call(
        functools.partial(_dil_kernel, n_sub=n_sub, has_prev=has_prev, dilation=dilation,
                          slopes=slopes, scale=HEAD_DIM ** -0.5, n_other=len(others)),
        out_shape=o_shape if others else (o_shape, lse_shape),
        grid=(batch, n_blk // n_sub),
        in_specs=in_specs,
        out_specs=tile if others else (tile, lse_tile),
        scratch_shapes=[pltpu.VMEM((DIL_HEADS_PER_GROUP, dilation * rows, HEAD_DIM), F32),
                        pltpu.VMEM((dilation * rows, LANES), F32)],
        compiler_params=_params(2, 40),
        name=f"dil_attn_d{dilation}",
    )(*args)


def _post_kernel(osb_ref, odl_ref, gp_ref, bg_ref, x_ref, mod_ref, g2_ref, wsb_ref, wdl_ref, wout_ref,
                 x1_ref, h2_ref):
    d = x_ref.shape[-1]
    chunks = _row_chunks(x_ref.shape[0])
    y_sb = [_dot(osb_ref[c, :], wsb_ref[...]) for c in chunks]
    y_dl = [_dot(odl_ref[c, :], wdl_ref[...]) for c in chunks]
    gates = [_sigmoid(gp_ref[c, :].astype(F32) + bg_ref[...]) for c in chunks]
    mixed = [(g[:, :d] * a + g[:, d:] * b).astype(BF16) for g, a, b in zip(gates, y_sb, y_dl)]
    t = [_dot(mx, wout_ref[...]) for mx in mixed]
    for c, tc in zip(chunks, t):
        x1 = x_ref[c, :] + mod_ref[0, 2:3, :] * tc
        x1_ref[c, :] = x1
        h2 = _rms(x1) * g2_ref[...] * (1.0 + mod_ref[0, 4:5, :]) + mod_ref[0, 3:4, :]
        h2_ref[c, :] = h2.astype(h2_ref.dtype)


def _post(o_sb, o_dl, gate_pre, b_gate, x2d, mod3, g2, w_sb, w_dl, w_out, seq, tm=512):
    m, d = x2d.shape

    def row(width):
        return pl.BlockSpec((tm, width), lambda i: (i, 0))

    def whole(arr):
        return pl.BlockSpec(arr.shape, lambda i: (0,) * arr.ndim, pipeline_mode=pl.Buffered(1))

    return pl.pallas_call(
        _post_kernel,
        out_shape=(jax.ShapeDtypeStruct((m, d), F32), jax.ShapeDtypeStruct((m, d), BF16)),
        grid=(m // tm,),
        in_specs=[row(SB_W), row(DIL_OUT), row(2 * d), whole(b_gate), row(d),
                  pl.BlockSpec((1, N_MOD, d), lambda i: ((i * tm) // seq, 0, 0)),
                  whole(g2), whole(w_sb), whole(w_dl), whole(w_out)],
        out_specs=(row(d), row(d)),
        compiler_params=_params(1, 60),
        name="post",
    )(o_sb, o_dl, gate_pre, b_gate, x2d, mod3, g2, w_sb, w_dl, w_out)


def _swiglu_kernel(h_ref, wg_ref, wu_ref, o_ref, wgb_ref, wub_ref):
    @pl.when(pl.program_id(1) == 0)
    def _():
        wgb_ref[...] = wg_ref[...].astype(BF16)
        wub_ref[...] = wu_ref[...].astype(BF16)

    for c in _row_chunks(h_ref.shape[0], 4 * SUB_ROWS):
        g = _dot(h_ref[c, :], wgb_ref[...])
        u = _dot(h_ref[c, :], wub_ref[...])
        o_ref[c, :] = (g * _sigmoid(g) * u).astype(o_ref.dtype)


def _swiglu(h, wg, wu, layer, tm=2048, tn=512):
    m, k = h.shape
    n = wg.shape[-1]
    w_spec = pl.BlockSpec((None, k, tn), lambda j, i: (layer, 0, j))
    return pl.pallas_call(
        _swiglu_kernel,
        out_shape=jax.ShapeDtypeStruct((m, n), BF16),
        grid=(n // tn, m // tm),
        in_specs=[pl.BlockSpec((tm, k), lambda j, i: (i, 0)), w_spec, w_spec],
        out_specs=pl.BlockSpec((tm, tn), lambda j, i: (i, j)),
        scratch_shapes=[pltpu.VMEM((k, tn), BF16), pltpu.VMEM((k, tn), BF16)],
        compiler_params=_params(2, 60),
        name="swiglu",
    )(h, wg, wu)


def _down_kernel(a_ref, w_ref, x1_ref, mod_ref, gf_ref, o_ref, *, final_norm):
    for c in _row_chunks(a_ref.shape[0]):
        x2 = x1_ref[c, :] + mod_ref[0, 5:6, :] * _dot(a_ref[c, :], w_ref[...])
        o_ref[c, :] = _rms(x2) * gf_ref[...] if final_norm else x2


def _down(a, w, x1, mod3, g_final, seq, final_norm, tm=512):
    m, f = a.shape
    d = w.shape[1]
    return pl.pallas_call(
        functools.partial(_down_kernel, final_norm=final_norm),
        out_shape=jax.ShapeDtypeStruct((m, d), F32),
        grid=(m // tm,),
        in_specs=[pl.BlockSpec((tm, f), lambda i: (i, 0)),
                  pl.BlockSpec((f, d), lambda i: (0, 0), pipeline_mode=pl.Buffered(1)),
                  pl.BlockSpec((tm, d), lambda i: (i, 0)),
                  pl.BlockSpec((1, N_MOD, d), lambda i: ((i * tm) // seq, 0, 0)),
                  pl.BlockSpec((1, d), lambda i: (0, 0))],
        out_specs=pl.BlockSpec((tm, d), lambda i: (i, 0)),
        compiler_params=_params(1, 60),
        name="down",
    )(a, w, x1, mod3, g_final)


def _alibi_slopes():
    return tuple(2.0 ** (-8.0 * (i + 1.0) / DIL_HEADS) for i in range(DIL_HEADS))


def kernel(x, c, w_ada, b_ada, g_norm1, g_norm2, g_final, w_in, b_gate, w_proj_sb, w_proj_dil,
           w_out, w_ffn_gate, w_ffn_up, w_ffn_down):
    batch, seq, d = x.shape
    tokens = batch * seq
    n_layers = w_ada.shape[0]
    slopes = _alibi_slopes()
    n_groups = len(DIL_GROUPS)
    sb_tile = 256
    idx = jnp.arange(sb_tile)
    tri = (idx[:, None] > idx[None, :]).astype(BF16)
    c_pad = jnp.pad(c, ((0, ADA_ROWS - batch), (0, 0)))
    perm_dils = tuple(sorted(dil for _, dil in DIL_GROUPS if dil > 1))
    dl_col0 = 3 * SB_W // W_COL
    gate_col0 = dl_col0 + 3 * DIL_W // W_COL
    assert DIL_OUT == W_COL
    group_order = sorted(range(n_groups), key=lambda g: -DIL_GROUPS[g][1])
    assert DIL_GROUPS[group_order[-1]][1] == 1

    for l in range(n_layers):
        mod3 = _ada(c_pad, w_ada[l], b_ada[l][None, :]).reshape(ADA_ROWS, N_MOD, d)

        h_all = _norm1(x, g_norm1[l][None, :], mod3, perm_dils)
        h = h_all[0].reshape(tokens, d)
        h_perm = dict(zip(perm_dils, (hp.reshape(tokens, d) for hp in h_all[1:])))

        qkv_sb = _proj(h, w_in, l, lambda j, r: 2 * j + r, 3 * SB_W // (2 * W_COL), 2, "proj_sb")
        o_sb, gate_pre = _sb_attention_and_gate(qkv_sb.reshape(batch, seq, 3 * SB_W), tri, h_all[0], w_in, l,
                                                gate_col0, 2 * d, sb_tile, sb_tile)
        gate_pre = gate_pre.reshape(tokens, 2 * d)

        partial_groups = []
        for g in group_order:
            dil = DIL_GROUPS[g][1]
            qkv_g = _proj(h_perm.get(dil, h), w_in, l,
                          lambda j, r, g=g: dl_col0 + r * n_groups + g, 1, 3, f"proj_dil_d{dil}")
            g_slopes = slopes[g * DIL_HEADS_PER_GROUP:(g + 1) * DIL_HEADS_PER_GROUP]
            if g != group_order[-1]:
                partial_groups.append(_dil_attention(qkv_g, batch, g, g_slopes, n_sub=4))
            else:
                o_dl = _dil_attention(qkv_g, batch, g, g_slopes, n_sub=4, others=tuple(partial_groups))

        x1, h2 = _post(o_sb.reshape(tokens, SB_W), o_dl.reshape(tokens, DIL_OUT), gate_pre,
                       b_gate[l][None, :], x.reshape(tokens, d), mod3, g_norm2[l][None, :],
                       w_proj_sb[l].astype(BF16), w_proj_dil[l].astype(BF16), w_out[l].astype(BF16), seq)

        a = _swiglu(h2, w_ffn_gate, w_ffn_up, l)
        x = _down(a, w_ffn_down[l].astype(BF16), x1, mod3, g_final[None, :], seq,
                  final_norm=(l == n_layers - 1)).reshape(batch, seq, d)
    return x
```

```python
import functools

import jax
import jax.numpy as jnp
from jax import lax
from jax.experimental import pallas as pl
from jax.experimental.pallas import tpu as pltpu

F32 = jnp.float32
BF16 = jnp.bfloat16

HEAD_DIM = 128
SB_HEADS = 8
DIL_GROUPS = ((128, 1), (512, 4), (2048, 16))
DIL_HEADS_PER_GROUP = 4
DIL_HEADS = DIL_HEADS_PER_GROUP * len(DIL_GROUPS)
BLOCK = 128
SB_W = SB_HEADS * HEAD_DIM
DIL_W = DIL_HEADS * HEAD_DIM
DIL_OUT = DIL_HEADS_PER_GROUP * HEAD_DIM
N_MOD = 6
EPS = 1e-6
LOG2E = 1.4426950408889634
LN2 = 0.6931471805599453

LANES = 128
MIB = 1024 * 1024
ADA_ROWS = 8
W_COL = 512
W_STAGE_ROWS = 128
SUB_ROWS = 256
DIL_CHAINS = 8
LSE_LANES = LANES // DIL_HEADS_PER_GROUP
assert DIL_CHAINS % DIL_HEADS_PER_GROUP == 0


def _params(n_axes, vmem_mib):
    return pltpu.CompilerParams(dimension_semantics=("arbitrary",) * n_axes,
                                vmem_limit_bytes=vmem_mib * MIB)


def _dot(a, b):
    return jnp.dot(a, b, preferred_element_type=F32)


def _dot_nt(a, b):
    return lax.dot_general(a, b, (((1,), (1,)), ((), ())), preferred_element_type=F32)


def _sigmoid(x):
    return 1.0 / (1.0 + jnp.exp(-x))


def _rms(x):
    return x * lax.rsqrt(jnp.mean(x * x, axis=-1, keepdims=True) + EPS)


def _row_chunks(rows, size=SUB_ROWS):
    return [slice(r, r + size) for r in range(0, rows, size)]


def _ada_kernel(c_ref, w_ref, b_ref, o_ref):
    c = c_ref[...]
    s = (c * _sigmoid(c)).astype(BF16)
    o_ref[...] = _dot(s, w_ref[...].astype(BF16)) + b_ref[...]


def _ada(c_pad, w, b, tn=1024):
    d, n = w.shape
    return pl.pallas_call(
        _ada_kernel,
        out_shape=jax.ShapeDtypeStruct((ADA_ROWS, n), F32),
        grid=(n // tn,),
        in_specs=[pl.BlockSpec((ADA_ROWS, d), lambda j: (0, 0)),
                  pl.BlockSpec((d, tn), lambda j: (0, j)),
                  pl.BlockSpec((1, tn), lambda j: (0, j))],
        out_specs=pl.BlockSpec((ADA_ROWS, tn), lambda j: (0, j)),
        compiler_params=_params(1, 40),
        name="ada",
    )(c_pad, w, b)


def _norm1_kernel(x_ref, g_ref, mod_ref, o_ref, *rest, dils):
    n_perm = len(dils)
    perm_refs, slabs = rest[:n_perm], rest[n_perm:]
    ts, d_model = x_ref.shape[1], x_ref.shape[2]
    y = _rms(x_ref[0]) * g_ref[...]
    h = y * (1.0 + mod_ref[0, 1:2, :]) + mod_ref[0, 0:1, :]
    o_ref[0] = h.astype(o_ref.dtype)
    n_slab = d_model // LANES
    for s in range(n_slab):
        slabs[0][s] = h[:, s * LANES:(s + 1) * LANES]
    level = {1: slabs[0]}
    for n, (dil, p_ref) in enumerate(zip(dils, perm_refs)):
        base = max(p for p in level if dil % p == 0)
        step, src = dil // base, level[base]
        keep = slabs[n + 1] if n + 1 < len(slabs) else None
        for r in range(dil):
            r_hi, r_lo = divmod(r, base)
            start = r_lo * (ts // base) + r_hi
            for s in range(n_slab):
                rows = src[s, pl.ds(start, ts // dil, stride=step), :]
                p_ref[0, r, :, s * LANES:(s + 1) * LANES] = rows.astype(p_ref.dtype)
                if keep is not None:
                    keep[s, r * (ts // dil):(r + 1) * (ts // dil), :] = rows
        if keep is not None:
            level[dil] = keep


def _norm1(x, g, mod3, dils, ts=512):
    b, s, d = x.shape
    out_shape = [jax.ShapeDtypeStruct((b, s, d), BF16)]
    out_specs = [pl.BlockSpec((1, ts, d), lambda i, j: (i, j, 0))]
    for dil in dils:
        out_shape.append(jax.ShapeDtypeStruct((b, dil, s // dil, d), BF16))
        out_specs.append(pl.BlockSpec((1, dil, ts // dil, d), lambda i, j: (i, 0, j, 0)))
    slab = pltpu.VMEM((d // LANES, ts, LANES), F32)
    return pl.pallas_call(
        functools.partial(_norm1_kernel, dils=dils),
        out_shape=out_shape,
        grid=(b, s // ts),
        in_specs=[pl.BlockSpec((1, ts, d), lambda i, j: (i, j, 0)),
                  pl.BlockSpec((1, d), lambda i, j: (0, 0)),
                  pl.BlockSpec((1, N_MOD, d), lambda i, j: (i, 0, 0))],
        out_specs=out_specs,
        scratch_shapes=[slab] * max(1, len(dils)),
        compiler_params=_params(2, 40),
        name="norm1",
    )(x, g, mod3)


def _proj_kernel(*refs, n_rhs):
    h_ref, w_refs, o_ref, wb_ref = refs[0], refs[1:1 + n_rhs], refs[1 + n_rhs], refs[2 + n_rhs]

    @pl.when(pl.program_id(1) == 0)
    def _():
        for r, w_ref in enumerate(w_refs):
            wb_ref[:, r * W_COL:(r + 1) * W_COL] = w_ref[...].astype(BF16)

    for c in _row_chunks(h_ref.shape[0], 4 * SUB_ROWS):
        for r in range(n_rhs):
            cols = slice(r * W_COL, (r + 1) * W_COL)
            o_ref[c, cols] = _dot(h_ref[c, :], wb_ref[:, cols]).astype(o_ref.dtype)


def _proj(h, w_in, layer, col_fn, n_tiles, n_rhs, name, tm=2048):
    m, k = h.shape
    tn = n_rhs * W_COL
    mode = pl.Buffered(1) if n_tiles == 1 else None

    def w_spec(r):
        return pl.BlockSpec((None, k, W_COL), lambda j, i: (layer, 0, col_fn(j, r)), pipeline_mode=mode)

    return pl.pallas_call(
        functools.partial(_proj_kernel, n_rhs=n_rhs),
        out_shape=jax.ShapeDtypeStruct((m, n_tiles * tn), BF16),
        grid=(n_tiles, m // tm),
        in_specs=[pl.BlockSpec((tm, k), lambda j, i: (i, 0))] + [w_spec(r) for r in range(n_rhs)],
        out_specs=pl.BlockSpec((tm, tn), lambda j, i: (i, j)),
        scratch_shapes=[pltpu.VMEM((k, tn), BF16)],
        compiler_params=_params(2, 56),
        name=name,
    )(h, *([w_in] * n_rhs))


def _sb_kernel(*refs, n_rhs, tq, bk, scale):
    q_ref, k_ref, v_ref, tri_ref, h_ref = refs[:5]
    w_refs, (o_ref, gate_ref, wb_ref) = refs[5:5 + n_rhs], refs[5 + n_rhs:]

    @pl.when(pl.program_id(1) == 0)
    def _():
        for r, w_ref in enumerate(w_refs):
            wb_ref[:, r * W_COL:(r + 1) * W_COL] = w_ref[...].astype(BF16)

    seq = q_ref.shape[1]
    heads = [slice(h * HEAD_DIM, (h + 1) * HEAD_DIM) for h in range(q_ref.shape[2] // HEAD_DIM)]
    tri = tri_ref[...]
    row = lax.broadcasted_iota(jnp.int32, (tq, bk), 0)
    col = lax.broadcasted_iota(jnp.int32, (tq, bk), 1)
    gate_pieces = [(slice(r0, r0 + SUB_ROWS), slice(c0, c0 + SUB_ROWS))
                   for r0 in range(0, seq, SUB_ROWS) for c0 in range(0, n_rhs * W_COL, SUB_ROWS)]
    n_blocks = sum((i + 1) * tq // bk for i in range(seq // tq))
    pieces_per_block = -(-len(gate_pieces) // n_blocks)

    def emit_gate_pieces(count):
        for _ in range(min(count, len(gate_pieces))):
            g_rows, g_cols = gate_pieces.pop(0)
            gate_ref[0, g_rows, g_cols] = _dot(h_ref[0, g_rows, :], wb_ref[:, g_cols]).astype(gate_ref.dtype)

    for i in range(seq // tq):
        n_keys = (i + 1) * tq
        rows = slice(i * tq, (i + 1) * tq)
        z_all = [_dot_nt(q_ref[0, rows, hd], k_ref[0, :n_keys, hd]) * (scale * LOG2E) for hd in heads]
        run = [jnp.zeros((tq, 1), F32) for _ in heads]
        weights = [[None] * (n_keys // bk) for _ in heads]
        for jb in reversed(range(n_keys // bk)):
            emit_gate_pieces(pieces_per_block)
            on_diagonal = (jb + 1) * bk > i * tq
            keep = col + (jb * bk - i * tq) < row
            z = [za[:, jb * bk:(jb + 1) * bk] for za in z_all]
            sp = [jnp.maximum(zh, 0.0) + jnp.log(1.0 + jnp.exp2(-jnp.abs(zh))) * LOG2E for zh in z]
            own = [zh - sh for zh, sh in zip(z, sp)]
            if on_diagonal:
                sp = [jnp.where(keep, sh, 0.0) for sh in sp]
            sp16 = [sh.astype(BF16) for sh in sp]
            later = [_dot(sh, tri) for sh in sp16]
            a = [jnp.exp2(oh - (rh + lh)) for oh, rh, lh in zip(own, run, later)]
            if on_diagonal:
                a = [jnp.where(keep, ah, 0.0) for ah in a]
            for h, ah in enumerate(a):
                weights[h][jb] = ah.astype(BF16)
            run = [rh + lh[:, 0:1] + sh[:, 0:1].astype(F32) for rh, lh, sh in zip(run, later, sp16)]
        for h, hd in enumerate(heads):
            a_all = weights[h][0] if len(weights[h]) == 1 else jnp.concatenate(weights[h], axis=1)
            o_ref[0, rows, hd] = _dot(a_all, v_ref[0, :n_keys, hd]).astype(o_ref.dtype)
    emit_gate_pieces(len(gate_pieces))


def _sb_attention_and_gate(qkv, tri, h, w_in, layer, gate_col0, gate_width, tq, bk, heads_per_step=2):
    b, s, k = h.shape
    n_steps = SB_HEADS // heads_per_step
    n_rhs = gate_width // (n_steps * W_COL)
    assert n_rhs * n_steps * W_COL == gate_width
    tn = n_rhs * W_COL
    blk = (1, s, heads_per_step * HEAD_DIM)

    def w_spec(r):
        return pl.BlockSpec((None, k, W_COL), lambda p, i: (layer, 0, gate_col0 + p * n_rhs + r),
                            pipeline_mode=pl.Buffered(1))

    return pl.pallas_call(
        functools.partial(_sb_kernel, n_rhs=n_rhs, tq=tq, bk=bk, scale=HEAD_DIM ** -0.5),
        out_shape=(jax.ShapeDtypeStruct((b, s, SB_W), BF16), jax.ShapeDtypeStruct((b, s, gate_width), BF16)),
        grid=(n_steps, b),
        in_specs=[pl.BlockSpec(blk, lambda p, i: (i, 0, p)),
                  pl.BlockSpec(blk, lambda p, i: (i, 0, n_steps + p)),
                  pl.BlockSpec(blk, lambda p, i: (i, 0, 2 * n_steps + p)),
                  pl.BlockSpec((bk, bk), lambda p, i: (0, 0)),
                  pl.BlockSpec((1, s, k), lambda p, i: (i, 0, 0))] + [w_spec(r) for r in range(n_rhs)],
        out_specs=(pl.BlockSpec(blk, lambda p, i: (i, 0, p)), pl.BlockSpec((1, s, tn), lambda p, i: (i, 0, p))),
        scratch_shapes=[pltpu.VMEM((k, tn), BF16)],
        compiler_params=_params(2, 60),
        name="sb_attn_gate",
    )(qkv, qkv, qkv, tri, h, *([w_in] * n_rhs))


def _dil_kernel(*refs, n_sub, has_prev, dilation, slopes, scale, n_other):
    refs = list(refs)
    q_ref, kc_ref = refs.pop(0), refs.pop(0)
    kp_ref = refs.pop(0) if has_prev else None
    vc_ref = refs.pop(0)
    vp_ref = refs.pop(0) if has_prev else None
    others = [(refs.pop(0), refs.pop(0)) for _ in range(n_other)]
    o_ref = refs.pop(0)
    lse_ref = None if n_other else refs.pop(0)
    o_slab, lse_slab = refs

    n_keys = 2 * BLOCK if has_prev else BLOCK
    a_idx = lax.broadcasted_iota(jnp.int32, (BLOCK, n_keys), 0)
    b_idx = lax.broadcasted_iota(jnp.int32, (BLOCK, n_keys), 1)
    rel = a_idx + (n_keys - BLOCK) - b_idx
    in_window = (rel >= 0) & (rel <= BLOCK)
    dist = (dilation * rel).astype(F32)
    bias = [(slope * LOG2E) * dist for slope in slopes]
    lane = lax.broadcasted_iota(jnp.int32, (BLOCK, LANES), 1)
    step = jnp.zeros((BLOCK, n_keys), jnp.int32) + pl.program_id(1)
    valid_first = in_window & ((b_idx >= BLOCK) | (step > 0)) if has_prev else in_window

    def operands(r, j, h):
        rows = slice(j * BLOCK, (j + 1) * BLOCK)
        cols = slice(h * HEAD_DIM, (h + 1) * HEAD_DIM)
        q, kk, vv = q_ref[0, r, rows, cols], kc_ref[0, r, rows, cols], vc_ref[0, r, rows, cols]
        if has_prev:
            if j == 0:
                k_prev, v_prev = kp_ref[0, r, :, cols], vp_ref[0, r, :, cols]
            else:
                prev_rows = slice((j - 1) * BLOCK, j * BLOCK)
                k_prev, v_prev = kc_ref[0, r, prev_rows, cols], vc_ref[0, r, prev_rows, cols]
            kk = jnp.concatenate([k_prev, kk], axis=0)
            vv = jnp.concatenate([v_prev, vv], axis=0)
        return q, kk, vv

    blocks = [(r, j, h) for r in range(dilation) for j in range(n_sub) for h in range(DIL_HEADS_PER_GROUP)]
    for c0 in range(0, len(blocks), DIL_CHAINS):
        chunk = blocks[c0:c0 + DIL_CHAINS]
        ops = [operands(*blk) for blk in chunk]
        s = [jnp.where(valid_first if j == 0 else in_window,
                       _dot_nt(q, kk) * (scale * LOG2E) - bias[h], -jnp.inf)
             for (r, j, h), (q, kk, vv) in zip(chunk, ops)]
        m = [jnp.max(si, axis=-1, keepdims=True) for si in s]
        p = [jnp.exp2(si - mi) for si, mi in zip(s, m)]
        l = [jnp.sum(pi, axis=-1, keepdims=True) for pi in p]
        o = [_dot(pi.astype(BF16), vv) / li for pi, li, (q, kk, vv) in zip(p, l, ops)]
        lse = [mi * LN2 + jnp.log(li) for mi, li in zip(m, l)]
        for (r, j, h), oi in zip(chunk, o):
            o_slab[h, pl.ds(j * BLOCK * dilation + r, BLOCK, stride=dilation), :] = oi
        for g0 in range(0, len(chunk), DIL_HEADS_PER_GROUP):
            r, j, _ = chunk[g0]
            packed = lse[g0 + DIL_HEADS_PER_GROUP - 1]
            for h in reversed(range(DIL_HEADS_PER_GROUP - 1)):
                packed = jnp.where(lane < (h + 1) * LSE_LANES, lse[g0 + h], packed)
            lse_slab[pl.ds(j * BLOCK * dilation + r, BLOCK, stride=dilation), :] = packed

    if not others:
        lse_ref[0] = lse_slab[...]
        for h in range(DIL_HEADS_PER_GROUP):
            o_ref[0, :, h * HEAD_DIM:(h + 1) * HEAD_DIM] = o_slab[h].astype(o_ref.dtype)
        return
    lses = [lse_slab[...]] + [lg_ref[0] for _, lg_ref in others]
    top = functools.reduce(jnp.maximum, lses)
    e = [jnp.exp(lg - top) for lg in lses]
    inv = 1.0 / functools.reduce(lambda a, b: a + b, e)
    weights = [eg * inv for eg in e]
    for h in range(DIL_HEADS_PER_GROUP):
        cols = slice(h * HEAD_DIM, (h + 1) * HEAD_DIM)
        lse_col = slice(h * LSE_LANES, h * LSE_LANES + 1)
        outs = [o_slab[h]] + [og_ref[0, :, cols].astype(F32) for og_ref, _ in others]
        mixed = functools.reduce(lambda a, b: a + b, [wg[:, lse_col] * og for wg, og in zip(weights, outs)])
        o_ref[0, :, cols] = mixed.astype(o_ref.dtype)


def _dil_attention(qkv, batch, group, slopes, n_sub, others=()):
    window, dilation = DIL_GROUPS[group]
    s = qkv.shape[0] // batch
    sub_len = s // dilation
    n_blk = sub_len // BLOCK
    n_sub = min(n_sub, n_blk)
    has_prev = n_blk > 1
    assert window // dilation == BLOCK and sub_len % BLOCK == 0 and n_blk % n_sub == 0
    view = qkv.reshape(batch, dilation, sub_len, 3 * DIL_OUT)
    rows = n_sub * BLOCK

    def cur(col):
        return pl.BlockSpec((1, dilation, rows, DIL_OUT), lambda i, n: (i, 0, n, col))

    def prev(col):
        return pl.BlockSpec((1, dilation, BLOCK, DIL_OUT),
                            lambda i, n: (i, 0, jnp.maximum(n * n_sub - 1, 0), col))

    if has_prev:
        in_specs = [cur(0), cur(1), prev(1), cur(2), prev(2)]
    else:
        in_specs = [cur(0), cur(1), cur(2)]
    args = [view] * len(in_specs)
    tile = pl.BlockSpec((1, dilation * rows, DIL_OUT), lambda i, n: (i, n, 0))
    lse_tile = pl.BlockSpec((1, dilation * rows, LANES), lambda i, n: (i, n, 0))
    for o_g, lse_g in others:
        in_specs += [tile, lse_tile]
        args += [o_g, lse_g]
    o_shape = jax.ShapeDtypeStruct((batch, s, DIL_OUT), BF16)
    lse_shape = jax.ShapeDtypeStruct((batch, s, LANES), F32)
    return pl.pallas_call(
        functools.partial(_dil_kernel, n_sub=n_sub, has_prev=has_prev, dilation=dilation,
                          slopes=slopes, scale=HEAD_DIM ** -0.5, n_other=len(others)),
        out_shape=o_shape if others else (o_shape, lse_shape),
        grid=(batch, n_blk // n_sub),
        in_specs=in_specs,
        out_specs=tile if others else (tile, lse_tile),
        scratch_shapes=[pltpu.VMEM((DIL_HEADS_PER_GROUP, dilation * rows, HEAD_DIM), F32),
                        pltpu.VMEM((dilation * rows, LANES), F32)],
        compiler_params=_params(2, 40),
        name=f"dil_attn_d{dilation}",
    )(*args)


def _post_kernel(osb_ref, odl_ref, gp_ref, bg_ref, x_ref, mod_ref, g2_ref, wsb_hbm, wdl_hbm, wout_hbm,
                 x1_ref, h2_ref, wsb_ref, wdl_ref, wout_ref, stage_ref, sem_ref, *, layer):
    @pl.when(pl.program_id(0) == 0)
    def _():
        for w_hbm, wb_ref in ((wsb_hbm, wsb_ref), (wdl_hbm, wdl_ref), (wout_hbm, wout_ref)):
            _load_weight_bf16(w_hbm, layer, wb_ref, stage_ref, sem_ref)

    d = x_ref.shape[-1]
    chunks = _row_chunks(x_ref.shape[0])
    y_sb = [_dot(osb_ref[c, :], wsb_ref[...]) for c in chunks]
    y_dl = [_dot(odl_ref[c, :], wdl_ref[...]) for c in chunks]
    gates = [_sigmoid(gp_ref[c, :].astype(F32) + bg_ref[...]) for c in chunks]
    mixed = [(g[:, :d] * a + g[:, d:] * b).astype(BF16) for g, a, b in zip(gates, y_sb, y_dl)]
    t = [_dot(mx, wout_ref[...]) for mx in mixed]
    for c, tc in zip(chunks, t):
        x1 = x_ref[c, :] + mod_ref[0, 2:3, :] * tc
        x1_ref[c, :] = x1
        h2 = _rms(x1) * g2_ref[...] * (1.0 + mod_ref[0, 4:5, :]) + mod_ref[0, 3:4, :]
        h2_ref[c, :] = h2.astype(h2_ref.dtype)


def _post(o_sb, o_dl, gate_pre, b_gate, x2d, mod3, g2, w_sb, w_dl, w_out, layer, seq, tm=512):
    m, d = x2d.shape
    weights = (w_sb, w_dl, w_out)
    assert all(w.shape[1] % W_STAGE_ROWS == 0 and w.shape[2] == d for w in weights)

    def row(width):
        return pl.BlockSpec((tm, width), lambda i: (i, 0))

    def whole(arr):
        return pl.BlockSpec(arr.shape, lambda i: (0,) * arr.ndim, pipeline_mode=pl.Buffered(1))

    return pl.pallas_call(
        functools.partial(_post_kernel, layer=layer),
        out_shape=(jax.ShapeDtypeStruct((m, d), F32), jax.ShapeDtypeStruct((m, d), BF16)),
        grid=(m // tm,),
        in_specs=[row(SB_W), row(DIL_OUT), row(2 * d), whole(b_gate), row(d),
                  pl.BlockSpec((1, N_MOD, d), lambda i: ((i * tm) // seq, 0, 0)),
                  whole(g2)] + [pl.BlockSpec(memory_space=pl.ANY)] * len(weights),
        out_specs=(row(d), row(d)),
        scratch_shapes=[pltpu.VMEM(w.shape[1:], BF16) for w in weights]
                       + [pltpu.VMEM((2, W_STAGE_ROWS, d), F32), pltpu.SemaphoreType.DMA((2,))],
        compiler_params=_params(1, 60),
        name="post",
    )(o_sb, o_dl, gate_pre, b_gate, x2d, mod3, g2, *weights)


def _swiglu_kernel(h_ref, wg_ref, wu_ref, o_ref, wgb_ref, wub_ref):
    @pl.when(pl.program_id(1) == 0)
    def _():
        wgb_ref[...] = wg_ref[...].astype(BF16)
        wub_ref[...] = wu_ref[...].astype(BF16)

    for c in _row_chunks(h_ref.shape[0], 4 * SUB_ROWS):
        g = _dot(h_ref[c, :], wgb_ref[...])
        u = _dot(h_ref[c, :], wub_ref[...])
        o_ref[c, :] = (g * _sigmoid(g) * u).astype(o_ref.dtype)


def _swiglu(h, wg, wu, layer, tm=2048, tn=512):
    m, k = h.shape
    n = wg.shape[-1]
    w_spec = pl.BlockSpec((None, k, tn), lambda j, i: (layer, 0, j))
    return pl.pallas_call(
        _swiglu_kernel,
        out_shape=jax.ShapeDtypeStruct((m, n), BF16),
        grid=(n // tn, m // tm),
        in_specs=[pl.BlockSpec((tm, k), lambda j, i: (i, 0)), w_spec, w_spec],
        out_specs=pl.BlockSpec((tm, tn), lambda j, i: (i, j)),
        scratch_shapes=[pltpu.VMEM((k, tn), BF16), pltpu.VMEM((k, tn), BF16)],
        compiler_params=_params(2, 60),
        name="swiglu",
    )(h, wg, wu)


def _load_weight_bf16(w_hbm, layer, dst_ref, stage_ref, sem_ref):
    n_chunks = dst_ref.shape[0] // W_STAGE_ROWS

    def chunk_copy(c, slot):
        src = w_hbm.at[layer, pl.ds(c * W_STAGE_ROWS, W_STAGE_ROWS), :]
        return pltpu.make_async_copy(src, stage_ref.at[slot], sem_ref.at[slot])

    chunk_copy(0, 0).start()

    def body(c, carry):
        slot = lax.rem(c, 2)

        @pl.when(c + 1 < n_chunks)
        def _():
            chunk_copy(c + 1, 1 - slot).start()

        chunk_copy(c, slot).wait()
        rows = pl.ds(pl.multiple_of(c * W_STAGE_ROWS, W_STAGE_ROWS), W_STAGE_ROWS)
        dst_ref[rows, :] = stage_ref[slot].astype(dst_ref.dtype)
        return carry

    lax.fori_loop(0, n_chunks, body, 0)


def _down_kernel(a_ref, w_hbm, x1_ref, mod_ref, gf_ref, o_ref, wb_ref, stage_ref, sem_ref, *, layer, final_norm):
    @pl.when(pl.program_id(0) == 0)
    def _():
        _load_weight_bf16(w_hbm, layer, wb_ref, stage_ref, sem_ref)

    for c in _row_chunks(a_ref.shape[0]):
        x2 = x1_ref[c, :] + mod_ref[0, 5:6, :] * _dot(a_ref[c, :], wb_ref[...])
        o_ref[c, :] = _rms(x2) * gf_ref[...] if final_norm else x2


def _down(a, w, layer, x1, mod3, g_final, seq, final_norm, tm=512):
    m, f = a.shape
    d = w.shape[-1]
    assert f % W_STAGE_ROWS == 0
    return pl.pallas_call(
        functools.partial(_down_kernel, layer=layer, final_norm=final_norm),
        out_shape=jax.ShapeDtypeStruct((m, d), F32),
        grid=(m // tm,),
        in_specs=[pl.BlockSpec((tm, f), lambda i: (i, 0)),
                  pl.BlockSpec(memory_space=pl.ANY),
                  pl.BlockSpec((tm, d), lambda i: (i, 0)),
                  pl.BlockSpec((1, N_MOD, d), lambda i: ((i * tm) // seq, 0, 0)),
                  pl.BlockSpec((1, d), lambda i: (0, 0))],
        out_specs=pl.BlockSpec((tm, d), lambda i: (i, 0)),
        scratch_shapes=[pltpu.VMEM((f, d), BF16), pltpu.VMEM((2, W_STAGE_ROWS, d), F32),
                        pltpu.SemaphoreType.DMA((2,))],
        compiler_params=_params(1, 60),
        name="down",
    )(a, w, x1, mod3, g_final)


def _alibi_slopes():
    return tuple(2.0 ** (-8.0 * (i + 1.0) / DIL_HEADS) for i in range(DIL_HEADS))


def kernel(x, c, w_ada, b_ada, g_norm1, g_norm2, g_final, w_in, b_gate, w_proj_sb, w_proj_dil,
           w_out, w_ffn_gate, w_ffn_up, w_ffn_down):
    batch, seq, d = x.shape
    tokens = batch * seq
    n_layers = w_ada.shape[0]
    slopes = _alibi_slopes()
    n_groups = len(DIL_GROUPS)
    sb_tile = 256
    idx = jnp.arange(sb_tile)
    tri = (idx[:, None] > idx[None, :]).astype(BF16)
    c_pad = jnp.pad(c, ((0, ADA_ROWS - batch), (0, 0)))
    perm_dils = tuple(sorted(dil for _, dil in DIL_GROUPS if dil > 1))
    dl_col0 = 3 * SB_W // W_COL
    gate_col0 = dl_col0 + 3 * DIL_W // W_COL
    assert DIL_OUT == W_COL
    group_order = sorted(range(n_groups), key=lambda g: -DIL_GROUPS[g][1])
    assert DIL_GROUPS[group_order[-1]][1] == 1

    for l in range(n_layers):
        mod3 = _ada(c_pad, w_ada[l], b_ada[l][None, :]).reshape(ADA_ROWS, N_MOD, d)

        h_all = _norm1(x, g_norm1[l][None, :], mod3, perm_dils)
        h = h_all[0].reshape(tokens, d)
        h_perm = dict(zip(perm_dils, (hp.reshape(tokens, d) for hp in h_all[1:])))

        qkv_sb = _proj(h, w_in, l, lambda j, r: 2 * j + r, 3 * SB_W // (2 * W_COL), 2, "proj_sb")
        o_sb, gate_pre = _sb_attention_and_gate(qkv_sb.reshape(batch, seq, 3 * SB_W), tri, h_all[0], w_in, l,
                                                gate_col0, 2 * d, sb_tile, sb_tile)
        gate_pre = gate_pre.reshape(tokens, 2 * d)

        partial_groups = []
        for g in group_order:
            dil = DIL_GROUPS[g][1]
            qkv_g = _proj(h_perm.get(dil, h), w_in, l,
                          lambda j, r, g=g: dl_col0 + r * n_groups + g, 1, 3, f"proj_dil_d{dil}")
            g_slopes = slopes[g * DIL_HEADS_PER_GROUP:(g + 1) * DIL_HEADS_PER_GROUP]
            if g != group_order[-1]:
                partial_groups.append(_dil_attention(qkv_g, batch, g, g_slopes, n_sub=4))
            else:
                o_dl = _dil_attention(qkv_g, batch, g, g_slopes, n_sub=4, others=tuple(partial_groups))

        x1, h2 = _post(o_sb.reshape(tokens, SB_W), o_dl.reshape(tokens, DIL_OUT), gate_pre,
                       b_gate[l][None, :], x.reshape(tokens, d), mod3, g_norm2[l][None, :],
                       w_proj_sb, w_proj_dil, w_out, l, seq)

        a = _swiglu(h2, w_ffn_gate, w_ffn_up, l)
        x = _down(a, w_ffn_down, l, x1, mod3, g_final[None, :], seq,
                  final_norm=(l == n_layers - 1)).reshape(batch, seq, d)
    return x
```

```python
import functools

import jax
import jax.numpy as jnp
from jax import lax
from jax.experimental import pallas as pl
from jax.experimental.pallas import tpu as pltpu

F32 = jnp.float32
BF16 = jnp.bfloat16

HEAD_DIM = 128
SB_HEADS = 8
DIL_GROUPS = ((128, 1), (512, 4), (2048, 16))
DIL_HEADS_PER_GROUP = 4
DIL_HEADS = DIL_HEADS_PER_GROUP * len(DIL_GROUPS)
BLOCK = 128
SB_W = SB_HEADS * HEAD_DIM
DIL_W = DIL_HEADS * HEAD_DIM
DIL_OUT = DIL_HEADS_PER_GROUP * HEAD_DIM
N_MOD = 6
EPS = 1e-6
LOG2E = 1.4426950408889634
LN2 = 0.6931471805599453

LANES = 128
MIB = 1024 * 1024
ADA_ROWS = 8
W_COL = 512
W_STAGE_ROWS = 128
W_STAGE_SLOTS = 4
SUB_ROWS = 256
DIL_CHAINS = 8
LSE_LANES = LANES // DIL_HEADS_PER_GROUP
assert DIL_CHAINS % DIL_HEADS_PER_GROUP == 0


def _params(n_axes, vmem_mib):
    return pltpu.CompilerParams(dimension_semantics=("arbitrary",) * n_axes,
                                vmem_limit_bytes=vmem_mib * MIB)


def _dot(a, b):
    return jnp.dot(a, b, preferred_element_type=F32)


def _dot_nt(a, b):
    return lax.dot_general(a, b, (((1,), (1,)), ((), ())), preferred_element_type=F32)


def _sigmoid(x):
    return 1.0 / (1.0 + jnp.exp(-x))


def _rms(x):
    return x * lax.rsqrt(jnp.mean(x * x, axis=-1, keepdims=True) + EPS)


def _row_chunks(rows, size=SUB_ROWS):
    return [slice(r, r + size) for r in range(0, rows, size)]


def _ada_kernel(c_ref, w_ref, b_ref, o_ref):
    c = c_ref[...]
    s = (c * _sigmoid(c)).astype(BF16)
    o_ref[...] = _dot(s, w_ref[...].astype(BF16)) + b_ref[...]


def _ada(c_pad, w, b, tn=1024):
    d, n = w.shape
    return pl.pallas_call(
        _ada_kernel,
        out_shape=jax.ShapeDtypeStruct((ADA_ROWS, n), F32),
        grid=(n // tn,),
        in_specs=[pl.BlockSpec((ADA_ROWS, d), lambda j: (0, 0)),
                  pl.BlockSpec((d, tn), lambda j: (0, j)),
                  pl.BlockSpec((1, tn), lambda j: (0, j))],
        out_specs=pl.BlockSpec((ADA_ROWS, tn), lambda j: (0, j)),
        compiler_params=_params(1, 40),
        name="ada",
    )(c_pad, w, b)


def _norm1_kernel(x_ref, g_ref, mod_ref, o_ref, *rest, dils):
    n_perm = len(dils)
    perm_refs, slabs = rest[:n_perm], rest[n_perm:]
    ts, d_model = x_ref.shape[1], x_ref.shape[2]
    y = _rms(x_ref[0]) * g_ref[...]
    h = y * (1.0 + mod_ref[0, 1:2, :]) + mod_ref[0, 0:1, :]
    o_ref[0] = h.astype(o_ref.dtype)
    n_slab = d_model // LANES
    for s in range(n_slab):
        slabs[0][s] = h[:, s * LANES:(s + 1) * LANES]
    level = {1: slabs[0]}
    for n, (dil, p_ref) in enumerate(zip(dils, perm_refs)):
        base = max(p for p in level if dil % p == 0)
        step, src = dil // base, level[base]
        keep = slabs[n + 1] if n + 1 < len(slabs) else None
        for r in range(dil):
            r_hi, r_lo = divmod(r, base)
            start = r_lo * (ts // base) + r_hi
            for s in range(n_slab):
                rows = src[s, pl.ds(start, ts // dil, stride=step), :]
                p_ref[0, r, :, s * LANES:(s + 1) * LANES] = rows.astype(p_ref.dtype)
                if keep is not None:
                    keep[s, r * (ts // dil):(r + 1) * (ts // dil), :] = rows
        if keep is not None:
            level[dil] = keep


def _norm1(x, g, mod3, dils, ts=512):
    b, s, d = x.shape
    out_shape = [jax.ShapeDtypeStruct((b, s, d), BF16)]
    out_specs = [pl.BlockSpec((1, ts, d), lambda i, j: (i, j, 0))]
    for dil in dils:
        out_shape.append(jax.ShapeDtypeStruct((b, dil, s // dil, d), BF16))
        out_specs.append(pl.BlockSpec((1, dil, ts // dil, d), lambda i, j: (i, 0, j, 0)))
    slab = pltpu.VMEM((d // LANES, ts, LANES), F32)
    return pl.pallas_call(
        functools.partial(_norm1_kernel, dils=dils),
        out_shape=out_shape,
        grid=(b, s // ts),
        in_specs=[pl.BlockSpec((1, ts, d), lambda i, j: (i, j, 0)),
                  pl.BlockSpec((1, d), lambda i, j: (0, 0)),
                  pl.BlockSpec((1, N_MOD, d), lambda i, j: (i, 0, 0))],
        out_specs=out_specs,
        scratch_shapes=[slab] * max(1, len(dils)),
        compiler_params=_params(2, 40),
        name="norm1",
    )(x, g, mod3)


def _proj_kernel(*refs, n_rhs):
    h_ref, w_refs, o_ref, wb_ref = refs[0], refs[1:1 + n_rhs], refs[1 + n_rhs], refs[2 + n_rhs]

    @pl.when(pl.program_id(1) == 0)
    def _():
        for r, w_ref in enumerate(w_refs):
            wb_ref[:, r * W_COL:(r + 1) * W_COL] = w_ref[...].astype(BF16)

    for c in _row_chunks(h_ref.shape[0], 4 * SUB_ROWS):
        for r in range(n_rhs):
            cols = slice(r * W_COL, (r + 1) * W_COL)
            o_ref[c, cols] = _dot(h_ref[c, :], wb_ref[:, cols]).astype(o_ref.dtype)


def _proj(h, w_in, layer, col_fn, n_tiles, n_rhs, name, tm=2048):
    m, k = h.shape
    tn = n_rhs * W_COL
    mode = pl.Buffered(1) if n_tiles == 1 else None

    def w_spec(r):
        return pl.BlockSpec((None, k, W_COL), lambda j, i: (layer, 0, col_fn(j, r)), pipeline_mode=mode)

    return pl.pallas_call(
        functools.partial(_proj_kernel, n_rhs=n_rhs),
        out_shape=jax.ShapeDtypeStruct((m, n_tiles * tn), BF16),
        grid=(n_tiles, m // tm),
        in_specs=[pl.BlockSpec((tm, k), lambda j, i: (i, 0))] + [w_spec(r) for r in range(n_rhs)],
        out_specs=pl.BlockSpec((tm, tn), lambda j, i: (i, j)),
        scratch_shapes=[pltpu.VMEM((k, tn), BF16)],
        compiler_params=_params(2, 56),
        name=name,
    )(h, *([w_in] * n_rhs))


def _sb_kernel(*refs, n_rhs, tq, bk, scale):
    q_ref, k_ref, v_ref, tri_ref, h_ref = refs[:5]
    w_refs, (o_ref, gate_ref, wb_ref) = refs[5:5 + n_rhs], refs[5 + n_rhs:]

    @pl.when(pl.program_id(1) == 0)
    def _():
        for r, w_ref in enumerate(w_refs):
            wb_ref[:, r * W_COL:(r + 1) * W_COL] = w_ref[...].astype(BF16)

    seq = q_ref.shape[1]
    heads = [slice(h * HEAD_DIM, (h + 1) * HEAD_DIM) for h in range(q_ref.shape[2] // HEAD_DIM)]
    tri = tri_ref[...]
    row = lax.broadcasted_iota(jnp.int32, (tq, bk), 0)
    col = lax.broadcasted_iota(jnp.int32, (tq, bk), 1)
    gate_pieces = [(slice(r0, r0 + SUB_ROWS), slice(c0, c0 + SUB_ROWS))
                   for r0 in range(0, seq, SUB_ROWS) for c0 in range(0, n_rhs * W_COL, SUB_ROWS)]
    n_blocks = sum((i + 1) * tq // bk for i in range(seq // tq))
    pieces_per_block = -(-len(gate_pieces) // n_blocks)

    def emit_gate_pieces(count):
        for _ in range(min(count, len(gate_pieces))):
            g_rows, g_cols = gate_pieces.pop(0)
            gate_ref[0, g_rows, g_cols] = _dot(h_ref[0, g_rows, :], wb_ref[:, g_cols]).astype(gate_ref.dtype)

    for i in range(seq // tq):
        n_keys = (i + 1) * tq
        rows = slice(i * tq, (i + 1) * tq)
        z_all = [_dot_nt(q_ref[0, rows, hd], k_ref[0, :n_keys, hd]) * (scale * LOG2E) for hd in heads]
        run = [jnp.zeros((tq, 1), F32) for _ in heads]
        weights = [[None] * (n_keys // bk) for _ in heads]
        for jb in reversed(range(n_keys // bk)):
            emit_gate_pieces(pieces_per_block)
            on_diagonal = (jb + 1) * bk > i * tq
            keep = col + (jb * bk - i * tq) < row
            z = [za[:, jb * bk:(jb + 1) * bk] for za in z_all]
            sp = [jnp.maximum(zh, 0.0) + jnp.log(1.0 + jnp.exp2(-jnp.abs(zh))) * LOG2E for zh in z]
            own = [zh - sh for zh, sh in zip(z, sp)]
            if on_diagonal:
                sp = [jnp.where(keep, sh, 0.0) for sh in sp]
            sp16 = [sh.astype(BF16) for sh in sp]
            later = [_dot(sh, tri) for sh in sp16]
            a = [jnp.exp2(oh - (rh + lh)) for oh, rh, lh in zip(own, run, later)]
            if on_diagonal:
                a = [jnp.where(keep, ah, 0.0) for ah in a]
            for h, ah in enumerate(a):
                weights[h][jb] = ah.astype(BF16)
            run = [rh + lh[:, 0:1] + sh[:, 0:1].astype(F32) for rh, lh, sh in zip(run, later, sp16)]
        for h, hd in enumerate(heads):
            a_all = weights[h][0] if len(weights[h]) == 1 else jnp.concatenate(weights[h], axis=1)
            o_ref[0, rows, hd] = _dot(a_all, v_ref[0, :n_keys, hd]).astype(o_ref.dtype)
    emit_gate_pieces(len(gate_pieces))


def _sb_attention_and_gate(qkv, tri, h, w_in, layer, gate_col0, gate_width, tq, bk, heads_per_step=2):
    b, s, k = h.shape
    n_steps = SB_HEADS // heads_per_step
    n_rhs = gate_width // (n_steps * W_COL)
    assert n_rhs * n_steps * W_COL == gate_width
    tn = n_rhs * W_COL
    blk = (1, s, heads_per_step * HEAD_DIM)

    def w_spec(r):
        return pl.BlockSpec((None, k, W_COL), lambda p, i: (layer, 0, gate_col0 + p * n_rhs + r),
                            pipeline_mode=pl.Buffered(1))

    return pl.pallas_call(
        functools.partial(_sb_kernel, n_rhs=n_rhs, tq=tq, bk=bk, scale=HEAD_DIM ** -0.5),
        out_shape=(jax.ShapeDtypeStruct((b, s, SB_W), BF16), jax.ShapeDtypeStruct((b, s, gate_width), BF16)),
        grid=(n_steps, b),
        in_specs=[pl.BlockSpec(blk, lambda p, i: (i, 0, p)),
                  pl.BlockSpec(blk, lambda p, i: (i, 0, n_steps + p)),
                  pl.BlockSpec(blk, lambda p, i: (i, 0, 2 * n_steps + p)),
                  pl.BlockSpec((bk, bk), lambda p, i: (0, 0)),
                  pl.BlockSpec((1, s, k), lambda p, i: (i, 0, 0))] + [w_spec(r) for r in range(n_rhs)],
        out_specs=(pl.BlockSpec(blk, lambda p, i: (i, 0, p)), pl.BlockSpec((1, s, tn), lambda p, i: (i, 0, p))),
        scratch_shapes=[pltpu.VMEM((k, tn), BF16)],
        compiler_params=_params(2, 60),
        name="sb_attn_gate",
    )(qkv, qkv, qkv, tri, h, *([w_in] * n_rhs))


def _dil_kernel(*refs, n_sub, has_prev, dilation, slopes, scale, n_other):
    refs = list(refs)
    q_ref, kc_ref = refs.pop(0), refs.pop(0)
    kp_ref = refs.pop(0) if has_prev else None
    vc_ref = refs.pop(0)
    vp_ref = refs.pop(0) if has_prev else None
    others = [(refs.pop(0), refs.pop(0)) for _ in range(n_other)]
    o_ref = refs.pop(0)
    lse_ref = None if n_other else refs.pop(0)
    o_slab, lse_slab = refs

    n_keys = 2 * BLOCK if has_prev else BLOCK
    a_idx = lax.broadcasted_iota(jnp.int32, (BLOCK, n_keys), 0)
    b_idx = lax.broadcasted_iota(jnp.int32, (BLOCK, n_keys), 1)
    rel = a_idx + (n_keys - BLOCK) - b_idx
    in_window = (rel >= 0) & (rel <= BLOCK)
    dist = (dilation * rel).astype(F32)
    bias = [(slope * LOG2E) * dist for slope in slopes]
    lane = lax.broadcasted_iota(jnp.int32, (BLOCK, LANES), 1)
    step = jnp.zeros((BLOCK, n_keys), jnp.int32) + pl.program_id(1)
    valid_first = in_window & ((b_idx >= BLOCK) | (step > 0)) if has_prev else in_window

    def operands(r, j, h):
        rows = slice(j * BLOCK, (j + 1) * BLOCK)
        cols = slice(h * HEAD_DIM, (h + 1) * HEAD_DIM)
        q, kk, vv = q_ref[0, r, rows, cols], kc_ref[0, r, rows, cols], vc_ref[0, r, rows, cols]
        if has_prev:
            if j == 0:
                k_prev, v_prev = kp_ref[0, r, :, cols], vp_ref[0, r, :, cols]
            else:
                prev_rows = slice((j - 1) * BLOCK, j * BLOCK)
                k_prev, v_prev = kc_ref[0, r, prev_rows, cols], vc_ref[0, r, prev_rows, cols]
            kk = jnp.concatenate([k_prev, kk], axis=0)
            vv = jnp.concatenate([v_prev, vv], axis=0)
        return q, kk, vv

    blocks = [(r, j, h) for r in range(dilation) for j in range(n_sub) for h in range(DIL_HEADS_PER_GROUP)]
    for c0 in range(0, len(blocks), DIL_CHAINS):
        chunk = blocks[c0:c0 + DIL_CHAINS]
        ops = [operands(*blk) for blk in chunk]
        s = [jnp.where(valid_first if j == 0 else in_window,
                       _dot_nt(q, kk) * (scale * LOG2E) - bias[h], -jnp.inf)
             for (r, j, h), (q, kk, vv) in zip(chunk, ops)]
        m = [jnp.max(si, axis=-1, keepdims=True) for si in s]
        p = [jnp.exp2(si - mi) for si, mi in zip(s, m)]
        l = [jnp.sum(pi, axis=-1, keepdims=True) for pi in p]
        o = [_dot(pi.astype(BF16), vv) / li for pi, li, (q, kk, vv) in zip(p, l, ops)]
        lse = [mi * LN2 + jnp.log(li) for mi, li in zip(m, l)]
        for (r, j, h), oi in zip(chunk, o):
            o_slab[h, pl.ds(j * BLOCK * dilation + r, BLOCK, stride=dilation), :] = oi
        for g0 in range(0, len(chunk), DIL_HEADS_PER_GROUP):
            r, j, _ = chunk[g0]
            packed = lse[g0 + DIL_HEADS_PER_GROUP - 1]
            for h in reversed(range(DIL_HEADS_PER_GROUP - 1)):
                packed = jnp.where(lane < (h + 1) * LSE_LANES, lse[g0 + h], packed)
            lse_slab[pl.ds(j * BLOCK * dilation + r, BLOCK, stride=dilation), :] = packed

    if not others:
        lse_ref[0] = lse_slab[...]
        for h in range(DIL_HEADS_PER_GROUP):
            o_ref[0, :, h * HEAD_DIM:(h + 1) * HEAD_DIM] = o_slab[h].astype(o_ref.dtype)
        return
    lses = [lse_slab[...]] + [lg_ref[0] for _, lg_ref in others]
    top = functools.reduce(jnp.maximum, lses)
    e = [jnp.exp(lg - top) for lg in lses]
    inv = 1.0 / functools.reduce(lambda a, b: a + b, e)
    weights = [eg * inv for eg in e]
    for h in range(DIL_HEADS_PER_GROUP):
        cols = slice(h * HEAD_DIM, (h + 1) * HEAD_DIM)
        lse_col = slice(h * LSE_LANES, h * LSE_LANES + 1)
        outs = [o_slab[h]] + [og_ref[0, :, cols].astype(F32) for og_ref, _ in others]
        mixed = functools.reduce(lambda a, b: a + b, [wg[:, lse_col] * og for wg, og in zip(weights, outs)])
        o_ref[0, :, cols] = mixed.astype(o_ref.dtype)


def _dil_attention(qkv, batch, group, slopes, n_sub, others=()):
    window, dilation = DIL_GROUPS[group]
    s = qkv.shape[0] // batch
    sub_len = s // dilation
    n_blk = sub_len // BLOCK
    n_sub = min(n_sub, n_blk)
    has_prev = n_blk > 1
    assert window // dilation == BLOCK and sub_len % BLOCK == 0 and n_blk % n_sub == 0
    view = qkv.reshape(batch, dilation, sub_len, 3 * DIL_OUT)
    rows = n_sub * BLOCK

    def cur(col):
        return pl.BlockSpec((1, dilation, rows, DIL_OUT), lambda i, n: (i, 0, n, col))

    def prev(col):
        return pl.BlockSpec((1, dilation, BLOCK, DIL_OUT),
                            lambda i, n: (i, 0, jnp.maximum(n * n_sub - 1, 0), col))

    if has_prev:
        in_specs = [cur(0), cur(1), prev(1), cur(2), prev(2)]
    else:
        in_specs = [cur(0), cur(1), cur(2)]
    args = [view] * len(in_specs)
    tile = pl.BlockSpec((1, dilation * rows, DIL_OUT), lambda i, n: (i, n, 0))
    lse_tile = pl.BlockSpec((1, dilation * rows, LANES), lambda i, n: (i, n, 0))
    for o_g, lse_g in others:
        in_specs += [tile, lse_tile]
        args += [o_g, lse_g]
    o_shape = jax.ShapeDtypeStruct((batch, s, DIL_OUT), BF16)
    lse_shape = jax.ShapeDtypeStruct((batch, s, LANES), F32)
    return pl.pallas_call(
        functools.partial(_dil_kernel, n_sub=n_sub, has_prev=has_prev, dilation=dilation,
                          slopes=slopes, scale=HEAD_DIM ** -0.5, n_other=len(others)),
        out_shape=o_shape if others else (o_shape, lse_shape),
        grid=(batch, n_blk // n_sub),
        in_specs=in_specs,
        out_specs=tile if others else (tile, lse_tile),
        scratch_shapes=[pltpu.VMEM((DIL_HEADS_PER_GROUP, dilation * rows, HEAD_DIM), F32),
                        pltpu.VMEM((dilation * rows, LANES), F32)],
        compiler_params=_params(2, 40),
        name=f"dil_attn_d{dilation}",
    )(*args)


def _post_kernel(osb_ref, odl_ref, gp_ref, bg_ref, x_ref, mod_ref, g2_ref, wsb_hbm, wdl_hbm, wout_hbm,
                 x1_ref, h2_ref, wsb_ref, wdl_ref, wout_ref, stage_ref, sem_ref, *, layer):
    @pl.when(pl.program_id(0) == 0)
    def _():
        for w_hbm, wb_ref in ((wsb_hbm, wsb_ref), (wdl_hbm, wdl_ref), (wout_hbm, wout_ref)):
            _load_weight_bf16(w_hbm, layer, wb_ref, stage_ref, sem_ref)

    d = x_ref.shape[-1]
    chunks = _row_chunks(x_ref.shape[0])
    y_sb = [_dot(osb_ref[c, :], wsb_ref[...]) for c in chunks]
    y_dl = [_dot(odl_ref[c, :], wdl_ref[...]) for c in chunks]
    gates = [_sigmoid(gp_ref[c, :].astype(F32) + bg_ref[...]) for c in chunks]
    mixed = [(g[:, :d] * a + g[:, d:] * b).astype(BF16) for g, a, b in zip(gates, y_sb, y_dl)]
    t = [_dot(mx, wout_ref[...]) for mx in mixed]
    for c, tc in zip(chunks, t):
        x1 = x_ref[c, :] + mod_ref[0, 2:3, :] * tc
        x1_ref[c, :] = x1
        h2 = _rms(x1) * g2_ref[...] * (1.0 + mod_ref[0, 4:5, :]) + mod_ref[0, 3:4, :]
        h2_ref[c, :] = h2.astype(h2_ref.dtype)


def _post(o_sb, o_dl, gate_pre, b_gate, x2d, mod3, g2, w_sb, w_dl, w_out, layer, seq, tm=512):
    m, d = x2d.shape
    weights = (w_sb, w_dl, w_out)
    assert all(w.shape[1] % W_STAGE_ROWS == 0 and w.shape[2] == d for w in weights)

    def row(width):
        return pl.BlockSpec((tm, width), lambda i: (i, 0))

    def whole(arr):
        return pl.BlockSpec(arr.shape, lambda i: (0,) * arr.ndim, pipeline_mode=pl.Buffered(1))

    return pl.pallas_call(
        functools.partial(_post_kernel, layer=layer),
        out_shape=(jax.ShapeDtypeStruct((m, d), F32), jax.ShapeDtypeStruct((m, d), BF16)),
        grid=(m // tm,),
        in_specs=[row(SB_W), row(DIL_OUT), row(2 * d), whole(b_gate), row(d),
                  pl.BlockSpec((1, N_MOD, d), lambda i: ((i * tm) // seq, 0, 0)),
                  whole(g2)] + [pl.BlockSpec(memory_space=pl.ANY)] * len(weights),
        out_specs=(row(d), row(d)),
        scratch_shapes=[pltpu.VMEM(w.shape[1:], BF16) for w in weights]
                       + [pltpu.VMEM((W_STAGE_SLOTS, W_STAGE_ROWS, d), F32),
                          pltpu.SemaphoreType.DMA((W_STAGE_SLOTS,))],
        compiler_params=_params(1, 60),
        name="post",
    )(o_sb, o_dl, gate_pre, b_gate, x2d, mod3, g2, *weights)


def _swiglu_kernel(h_ref, wg_ref, wu_ref, o_ref, wgb_ref, wub_ref):
    @pl.when(pl.program_id(1) == 0)
    def _():
        wgb_ref[...] = wg_ref[...].astype(BF16)
        wub_ref[...] = wu_ref[...].astype(BF16)

    for c in _row_chunks(h_ref.shape[0], 4 * SUB_ROWS):
        g = _dot(h_ref[c, :], wgb_ref[...])
        u = _dot(h_ref[c, :], wub_ref[...])
        o_ref[c, :] = (g * _sigmoid(g) * u).astype(o_ref.dtype)


def _swiglu(h, wg, wu, layer, tm=2048, tn=512):
    m, k = h.shape
    n = wg.shape[-1]
    w_spec = pl.BlockSpec((None, k, tn), lambda j, i: (layer, 0, j))
    return pl.pallas_call(
        _swiglu_kernel,
        out_shape=jax.ShapeDtypeStruct((m, n), BF16),
        grid=(n // tn, m // tm),
        in_specs=[pl.BlockSpec((tm, k), lambda j, i: (i, 0)), w_spec, w_spec],
        out_specs=pl.BlockSpec((tm, tn), lambda j, i: (i, j)),
        scratch_shapes=[pltpu.VMEM((k, tn), BF16), pltpu.VMEM((k, tn), BF16)],
        compiler_params=_params(2, 60),
        name="swiglu",
    )(h, wg, wu)


def _load_weight_bf16(w_hbm, layer, dst_ref, stage_ref, sem_ref):
    n_chunks = dst_ref.shape[0] // W_STAGE_ROWS
    n_slots = stage_ref.shape[0]
    ahead = n_slots - 1
    assert n_chunks >= ahead

    def chunk_copy(c, slot):
        src = w_hbm.at[layer, pl.ds(c * W_STAGE_ROWS, W_STAGE_ROWS), :]
        return pltpu.make_async_copy(src, stage_ref.at[slot], sem_ref.at[slot])

    for c in range(ahead):
        chunk_copy(c, c).start()

    def body(c, carry):
        slot = lax.rem(c, n_slots)

        @pl.when(c + ahead < n_chunks)
        def _():
            chunk_copy(c + ahead, lax.rem(c + ahead, n_slots)).start()

        chunk_copy(c, slot).wait()
        rows = pl.ds(pl.multiple_of(c * W_STAGE_ROWS, W_STAGE_ROWS), W_STAGE_ROWS)
        dst_ref[rows, :] = stage_ref[slot].astype(dst_ref.dtype)
        return carry

    lax.fori_loop(0, n_chunks, body, 0)


def _down_kernel(a_ref, w_hbm, x1_ref, mod_ref, gf_ref, o_ref, wb_ref, stage_ref, sem_ref, *, layer, final_norm):
    @pl.when(pl.program_id(0) == 0)
    def _():
        _load_weight_bf16(w_hbm, layer, wb_ref, stage_ref, sem_ref)

    for c in _row_chunks(a_ref.shape[0]):
        x2 = x1_ref[c, :] + mod_ref[0, 5:6, :] * _dot(a_ref[c, :], wb_ref[...])
        o_ref[c, :] = _rms(x2) * gf_ref[...] if final_norm else x2


def _down(a, w, layer, x1, mod3, g_final, seq, final_norm, tm=512):
    m, f = a.shape
    d = w.shape[-1]
    assert f % W_STAGE_ROWS == 0
    return pl.pallas_call(
        functools.partial(_down_kernel, layer=layer, final_norm=final_norm),
        out_shape=jax.ShapeDtypeStruct((m, d), F32),
        grid=(m // tm,),
        in_specs=[pl.BlockSpec((tm, f), lambda i: (i, 0)),
                  pl.BlockSpec(memory_space=pl.ANY),
                  pl.BlockSpec((tm, d), lambda i: (i, 0)),
                  pl.BlockSpec((1, N_MOD, d), lambda i: ((i * tm) // seq, 0, 0)),
                  pl.BlockSpec((1, d), lambda i: (0, 0))],
        out_specs=pl.BlockSpec((tm, d), lambda i: (i, 0)),
        scratch_shapes=[pltpu.VMEM((f, d), BF16), pltpu.VMEM((W_STAGE_SLOTS, W_STAGE_ROWS, d), F32),
                        pltpu.SemaphoreType.DMA((W_STAGE_SLOTS,))],
        compiler_params=_params(1, 60),
        name="down",
    )(a, w, x1, mod3, g_final)


def _alibi_slopes():
    return tuple(2.0 ** (-8.0 * (i + 1.0) / DIL_HEADS) for i in range(DIL_HEADS))


def kernel(x, c, w_ada, b_ada, g_norm1, g_norm2, g_final, w_in, b_gate, w_proj_sb, w_proj_dil,
           w_out, w_ffn_gate, w_ffn_up, w_ffn_down):
    batch, seq, d = x.shape
    tokens = batch * seq
    n_layers = w_ada.shape[0]
    slopes = _alibi_slopes()
    n_groups = len(DIL_GROUPS)
    sb_tile = 256
    idx = jnp.arange(sb_tile)
    tri = (idx[:, None] > idx[None, :]).astype(BF16)
    c_pad = jnp.pad(c, ((0, ADA_ROWS - batch), (0, 0)))
    perm_dils = tuple(sorted(dil for _, dil in DIL_GROUPS if dil > 1))
    dl_col0 = 3 * SB_W // W_COL
    gate_col0 = dl_col0 + 3 * DIL_W // W_COL
    assert DIL_OUT == W_COL
    group_order = sorted(range(n_groups), key=lambda g: -DIL_GROUPS[g][1])
    assert DIL_GROUPS[group_order[-1]][1] == 1

    for l in range(n_layers):
        mod3 = _ada(c_pad, w_ada[l], b_ada[l][None, :]).reshape(ADA_ROWS, N_MOD, d)

        h_all = _norm1(x, g_norm1[l][None, :], mod3, perm_dils)
        h = h_all[0].reshape(tokens, d)
        h_perm = dict(zip(perm_dils, (hp.reshape(tokens, d) for hp in h_all[1:])))

        qkv_sb = _proj(h, w_in, l, lambda j, r: 2 * j + r, 3 * SB_W // (2 * W_COL), 2, "proj_sb")
        o_sb, gate_pre = _sb_attention_and_gate(qkv_sb.reshape(batch, seq, 3 * SB_W), tri, h_all[0], w_in, l,
                                                gate_col0, 2 * d, sb_tile, sb_tile)
        gate_pre = gate_pre.reshape(tokens, 2 * d)

        partial_groups = []
        for g in group_order:
            dil = DIL_GROUPS[g][1]
            qkv_g = _proj(h_perm.get(dil, h), w_in, l,
                          lambda j, r, g=g: dl_col0 + r * n_groups + g, 1, 3, f"proj_dil_d{dil}")
            g_slopes = slopes[g * DIL_HEADS_PER_GROUP:(g + 1) * DIL_HEADS_PER_GROUP]
            if g != group_order[-1]:
                partial_groups.append(_dil_attention(qkv_g, batch, g, g_slopes, n_sub=4))
            else:
                o_dl = _dil_attention(qkv_g, batch, g, g_slopes, n_sub=4, others=tuple(partial_groups))

        x1, h2 = _post(o_sb.reshape(tokens, SB_W), o_dl.reshape(tokens, DIL_OUT), gate_pre,
                       b_gate[l][None, :], x.reshape(tokens, d), mod3, g_norm2[l][None, :],
                       w_proj_sb, w_proj_dil, w_out, l, seq)

        a = _swiglu(h2, w_ffn_gate, w_ffn_up, l)
        x = _down(a, w_ffn_down, l, x1, mod3, g_final[None, :], seq,
                  final_norm=(l == n_layers - 1)).reshape(batch, seq, d)
    return x
```

```python
import functools

import jax
import jax.numpy as jnp
from jax import lax
from jax.experimental import pallas as pl
from jax.experimental.pallas import tpu as pltpu

F32 = jnp.float32
BF16 = jnp.bfloat16

HEAD_DIM = 128
SB_HEADS = 8
DIL_GROUPS = ((128, 1), (512, 4), (2048, 16))
DIL_HEADS_PER_GROUP = 4
DIL_HEADS = DIL_HEADS_PER_GROUP * len(DIL_GROUPS)
BLOCK = 128
SB_W = SB_HEADS * HEAD_DIM
DIL_W = DIL_HEADS * HEAD_DIM
DIL_OUT = DIL_HEADS_PER_GROUP * HEAD_DIM
N_MOD = 6
EPS = 1e-6
LOG2E = 1.4426950408889634
LN2 = 0.6931471805599453
QUERY_SCALE = HEAD_DIM ** -0.5 * LOG2E

LANES = 128
MIB = 1024 * 1024
ADA_ROWS = 8
W_COL = 512
W_STAGE_ROWS = 128
W_STAGE_SLOTS = 4
SUB_ROWS = 256
DIL_CHAINS = 8
LSE_LANES = LANES // DIL_HEADS_PER_GROUP
assert DIL_CHAINS % DIL_HEADS_PER_GROUP == 0


def _params(n_axes, vmem_mib):
    return pltpu.CompilerParams(dimension_semantics=("arbitrary",) * n_axes,
                                vmem_limit_bytes=vmem_mib * MIB)


def _dot(a, b):
    return jnp.dot(a, b, preferred_element_type=F32)


def _dot_nt(a, b):
    return lax.dot_general(a, b, (((1,), (1,)), ((), ())), preferred_element_type=F32)


def _sigmoid(x):
    return 1.0 / (1.0 + jnp.exp(-x))


def _rms(x):
    return x * lax.rsqrt(jnp.mean(x * x, axis=-1, keepdims=True) + EPS)


def _row_chunks(rows, size=SUB_ROWS):
    return [slice(r, r + size) for r in range(0, rows, size)]


def _ada_kernel(c_ref, w_ref, b_ref, o_ref):
    c = c_ref[...]
    s = (c * _sigmoid(c)).astype(BF16)
    o_ref[...] = _dot(s, w_ref[...].astype(BF16)) + b_ref[...]


def _ada(c_pad, w, b, tn=1024):
    d, n = w.shape
    return pl.pallas_call(
        _ada_kernel,
        out_shape=jax.ShapeDtypeStruct((ADA_ROWS, n), F32),
        grid=(n // tn,),
        in_specs=[pl.BlockSpec((ADA_ROWS, d), lambda j: (0, 0)),
                  pl.BlockSpec((d, tn), lambda j: (0, j)),
                  pl.BlockSpec((1, tn), lambda j: (0, j))],
        out_specs=pl.BlockSpec((ADA_ROWS, tn), lambda j: (0, j)),
        compiler_params=_params(1, 40),
        name="ada",
    )(c_pad, w, b)


def _norm1_kernel(x_ref, g_ref, mod_ref, o_ref, *rest, dils):
    n_perm = len(dils)
    perm_refs, slabs = rest[:n_perm], rest[n_perm:]
    ts, d_model = x_ref.shape[1], x_ref.shape[2]
    y = _rms(x_ref[0]) * g_ref[...]
    h = y * (1.0 + mod_ref[0, 1:2, :]) + mod_ref[0, 0:1, :]
    o_ref[0] = h.astype(o_ref.dtype)
    n_slab = d_model // LANES
    for s in range(n_slab):
        slabs[0][s] = h[:, s * LANES:(s + 1) * LANES]
    level = {1: slabs[0]}
    for n, (dil, p_ref) in enumerate(zip(dils, perm_refs)):
        base = max(p for p in level if dil % p == 0)
        step, src = dil // base, level[base]
        keep = slabs[n + 1] if n + 1 < len(slabs) else None
        for r in range(dil):
            r_hi, r_lo = divmod(r, base)
            start = r_lo * (ts // base) + r_hi
            for s in range(n_slab):
                rows = src[s, pl.ds(start, ts // dil, stride=step), :]
                p_ref[0, r, :, s * LANES:(s + 1) * LANES] = rows.astype(p_ref.dtype)
                if keep is not None:
                    keep[s, r * (ts // dil):(r + 1) * (ts // dil), :] = rows
        if keep is not None:
            level[dil] = keep


def _norm1(x, g, mod3, dils, ts=512):
    b, s, d = x.shape
    out_shape = [jax.ShapeDtypeStruct((b, s, d), BF16)]
    out_specs = [pl.BlockSpec((1, ts, d), lambda i, j: (i, j, 0))]
    for dil in dils:
        out_shape.append(jax.ShapeDtypeStruct((b, dil, s // dil, d), BF16))
        out_specs.append(pl.BlockSpec((1, dil, ts // dil, d), lambda i, j: (i, 0, j, 0)))
    slab = pltpu.VMEM((d // LANES, ts, LANES), F32)
    return pl.pallas_call(
        functools.partial(_norm1_kernel, dils=dils),
        out_shape=out_shape,
        grid=(b, s // ts),
        in_specs=[pl.BlockSpec((1, ts, d), lambda i, j: (i, j, 0)),
                  pl.BlockSpec((1, d), lambda i, j: (0, 0)),
                  pl.BlockSpec((1, N_MOD, d), lambda i, j: (i, 0, 0))],
        out_specs=out_specs,
        scratch_shapes=[slab] * max(1, len(dils)),
        compiler_params=_params(2, 40),
        name="norm1",
    )(x, g, mod3)


def _proj_kernel(*refs, n_rhs, query_blocks):
    h_ref, w_refs, o_ref, wb_ref = refs[0], refs[1:1 + n_rhs], refs[1 + n_rhs], refs[2 + n_rhs]

    @pl.when(pl.program_id(1) == 0)
    def _():
        for r, w_ref in enumerate(w_refs):
            wb_ref[:, r * W_COL:(r + 1) * W_COL] = w_ref[...].astype(BF16)

    for c in _row_chunks(h_ref.shape[0], 4 * SUB_ROWS):
        for r in range(n_rhs):
            cols = slice(r * W_COL, (r + 1) * W_COL)
            acc = _dot(h_ref[c, :], wb_ref[:, cols])
            tiles = [j for j, rr in query_blocks if rr == r]
            if tiles:
                is_query = functools.reduce(jnp.logical_or, [pl.program_id(0) == j for j in tiles])
                acc = acc * jnp.where(is_query, QUERY_SCALE, 1.0)
            o_ref[c, cols] = acc.astype(o_ref.dtype)


def _proj(h, w_in, layer, col_fn, n_tiles, n_rhs, name, query_blocks, tm=2048):
    m, k = h.shape
    tn = n_rhs * W_COL
    mode = pl.Buffered(1) if n_tiles == 1 else None

    def w_spec(r):
        return pl.BlockSpec((None, k, W_COL), lambda j, i: (layer, 0, col_fn(j, r)), pipeline_mode=mode)

    return pl.pallas_call(
        functools.partial(_proj_kernel, n_rhs=n_rhs, query_blocks=tuple(query_blocks)),
        out_shape=jax.ShapeDtypeStruct((m, n_tiles * tn), BF16),
        grid=(n_tiles, m // tm),
        in_specs=[pl.BlockSpec((tm, k), lambda j, i: (i, 0))] + [w_spec(r) for r in range(n_rhs)],
        out_specs=pl.BlockSpec((tm, tn), lambda j, i: (i, j)),
        scratch_shapes=[pltpu.VMEM((k, tn), BF16)],
        compiler_params=_params(2, 56),
        name=name,
    )(h, *([w_in] * n_rhs))


def _sb_kernel(*refs, n_rhs, tq, bk):
    q_ref, k_ref, v_ref, tri_ref, h_ref = refs[:5]
    w_refs, (o_ref, gate_ref, wb_ref) = refs[5:5 + n_rhs], refs[5 + n_rhs:]

    @pl.when(pl.program_id(1) == 0)
    def _():
        for r, w_ref in enumerate(w_refs):
            wb_ref[:, r * W_COL:(r + 1) * W_COL] = w_ref[...].astype(BF16)

    seq = q_ref.shape[1]
    heads = [slice(h * HEAD_DIM, (h + 1) * HEAD_DIM) for h in range(q_ref.shape[2] // HEAD_DIM)]
    tri = tri_ref[...]
    row = lax.broadcasted_iota(jnp.int32, (tq, bk), 0)
    col = lax.broadcasted_iota(jnp.int32, (tq, bk), 1)
    gate_pieces = [(slice(r0, r0 + SUB_ROWS), slice(c0, c0 + SUB_ROWS))
                   for r0 in range(0, seq, SUB_ROWS) for c0 in range(0, n_rhs * W_COL, SUB_ROWS)]
    n_blocks = sum((i + 1) * tq // bk for i in range(seq // tq))
    pieces_per_block = -(-len(gate_pieces) // n_blocks)

    def emit_gate_pieces(count):
        for _ in range(min(count, len(gate_pieces))):
            g_rows, g_cols = gate_pieces.pop(0)
            gate_ref[0, g_rows, g_cols] = _dot(h_ref[0, g_rows, :], wb_ref[:, g_cols]).astype(gate_ref.dtype)

    for i in range(seq // tq):
        n_keys = (i + 1) * tq
        rows = slice(i * tq, (i + 1) * tq)
        z_all = [_dot_nt(q_ref[0, rows, hd], k_ref[0, :n_keys, hd]) for hd in heads]
        run = [jnp.zeros((tq, 1), F32) for _ in heads]
        weights = [[None] * (n_keys // bk) for _ in heads]
        for jb in reversed(range(n_keys // bk)):
            emit_gate_pieces(pieces_per_block)
            on_diagonal = (jb + 1) * bk > i * tq
            keep = col + (jb * bk - i * tq) < row
            z = [za[:, jb * bk:(jb + 1) * bk] for za in z_all]
            sp = [jnp.maximum(zh, 0.0) + jnp.log(1.0 + jnp.exp2(-jnp.abs(zh))) * LOG2E for zh in z]
            own = [zh - sh for zh, sh in zip(z, sp)]
            if on_diagonal:
                sp = [jnp.where(keep, sh, 0.0) for sh in sp]
            sp16 = [sh.astype(BF16) for sh in sp]
            later = [_dot(sh, tri) for sh in sp16]
            a = [jnp.exp2(oh - (rh + lh)) for oh, rh, lh in zip(own, run, later)]
            if on_diagonal:
                a = [jnp.where(keep, ah, 0.0) for ah in a]
            for h, ah in enumerate(a):
                weights[h][jb] = ah.astype(BF16)
            run = [rh + lh[:, 0:1] + sh[:, 0:1].astype(F32) for rh, lh, sh in zip(run, later, sp16)]
        for h, hd in enumerate(heads):
            a_all = weights[h][0] if len(weights[h]) == 1 else jnp.concatenate(weights[h], axis=1)
            o_ref[0, rows, hd] = _dot(a_all, v_ref[0, :n_keys, hd]).astype(o_ref.dtype)
    emit_gate_pieces(len(gate_pieces))


def _sb_attention_and_gate(qkv, tri, h, w_in, layer, gate_col0, gate_width, tq, bk, heads_per_step=2):
    b, s, k = h.shape
    n_steps = SB_HEADS // heads_per_step
    n_rhs = gate_width // (n_steps * W_COL)
    assert n_rhs * n_steps * W_COL == gate_width
    tn = n_rhs * W_COL
    blk = (1, s, heads_per_step * HEAD_DIM)

    def w_spec(r):
        return pl.BlockSpec((None, k, W_COL), lambda p, i: (layer, 0, gate_col0 + p * n_rhs + r),
                            pipeline_mode=pl.Buffered(1))

    return pl.pallas_call(
        functools.partial(_sb_kernel, n_rhs=n_rhs, tq=tq, bk=bk),
        out_shape=(jax.ShapeDtypeStruct((b, s, SB_W), BF16), jax.ShapeDtypeStruct((b, s, gate_width), BF16)),
        grid=(n_steps, b),
        in_specs=[pl.BlockSpec(blk, lambda p, i: (i, 0, p)),
                  pl.BlockSpec(blk, lambda p, i: (i, 0, n_steps + p)),
                  pl.BlockSpec(blk, lambda p, i: (i, 0, 2 * n_steps + p)),
                  pl.BlockSpec((bk, bk), lambda p, i: (0, 0)),
                  pl.BlockSpec((1, s, k), lambda p, i: (i, 0, 0))] + [w_spec(r) for r in range(n_rhs)],
        out_specs=(pl.BlockSpec(blk, lambda p, i: (i, 0, p)), pl.BlockSpec((1, s, tn), lambda p, i: (i, 0, p))),
        scratch_shapes=[pltpu.VMEM((k, tn), BF16)],
        compiler_params=_params(2, 60),
        name="sb_attn_gate",
    )(qkv, qkv, qkv, tri, h, *([w_in] * n_rhs))


def _dil_kernel(*refs, n_sub, has_prev, dilation, slopes, n_other):
    refs = list(refs)
    q_ref, kc_ref = refs.pop(0), refs.pop(0)
    kp_ref = refs.pop(0) if has_prev else None
    vc_ref = refs.pop(0)
    vp_ref = refs.pop(0) if has_prev else None
    others = [(refs.pop(0), refs.pop(0)) for _ in range(n_other)]
    o_ref = refs.pop(0)
    lse_ref = None if n_other else refs.pop(0)
    o_slab, lse_slab = refs

    n_keys = 2 * BLOCK if has_prev else BLOCK
    a_idx = lax.broadcasted_iota(jnp.int32, (BLOCK, n_keys), 0)
    b_idx = lax.broadcasted_iota(jnp.int32, (BLOCK, n_keys), 1)
    rel = a_idx + (n_keys - BLOCK) - b_idx
    in_window = (rel >= 0) & (rel <= BLOCK)
    dist = (dilation * rel).astype(F32)
    bias = [(slope * LOG2E) * dist for slope in slopes]
    lane = lax.broadcasted_iota(jnp.int32, (BLOCK, LANES), 1)
    step = jnp.zeros((BLOCK, n_keys), jnp.int32) + pl.program_id(1)
    valid_first = in_window & ((b_idx >= BLOCK) | (step > 0)) if has_prev else in_window

    def operands(r, j, h):
        rows = slice(j * BLOCK, (j + 1) * BLOCK)
        cols = slice(h * HEAD_DIM, (h + 1) * HEAD_DIM)
        q, kk, vv = q_ref[0, r, rows, cols], kc_ref[0, r, rows, cols], vc_ref[0, r, rows, cols]
        if has_prev:
            if j == 0:
                k_prev, v_prev = kp_ref[0, r, :, cols], vp_ref[0, r, :, cols]
            else:
                prev_rows = slice((j - 1) * BLOCK, j * BLOCK)
                k_prev, v_prev = kc_ref[0, r, prev_rows, cols], vc_ref[0, r, prev_rows, cols]
            kk = jnp.concatenate([k_prev, kk], axis=0)
            vv = jnp.concatenate([v_prev, vv], axis=0)
        return q, kk, vv

    blocks = [(r, j, h) for r in range(dilation) for j in range(n_sub) for h in range(DIL_HEADS_PER_GROUP)]
    for c0 in range(0, len(blocks), DIL_CHAINS):
        chunk = blocks[c0:c0 + DIL_CHAINS]
        ops = [operands(*blk) for blk in chunk]
        s = [jnp.where(valid_first if j == 0 else in_window, _dot_nt(q, kk) - bias[h], -jnp.inf)
             for (r, j, h), (q, kk, vv) in zip(chunk, ops)]
        m = [jnp.max(si, axis=-1, keepdims=True) for si in s]
        p = [jnp.exp2(si - mi) for si, mi in zip(s, m)]
        l = [jnp.sum(pi, axis=-1, keepdims=True) for pi in p]
        o = [_dot(pi.astype(BF16), vv) / li for pi, li, (q, kk, vv) in zip(p, l, ops)]
        lse = [mi * LN2 + jnp.log(li) for mi, li in zip(m, l)]
        for (r, j, h), oi in zip(chunk, o):
            o_slab[h, pl.ds(j * BLOCK * dilation + r, BLOCK, stride=dilation), :] = oi
        for g0 in range(0, len(chunk), DIL_HEADS_PER_GROUP):
            r, j, _ = chunk[g0]
            packed = lse[g0 + DIL_HEADS_PER_GROUP - 1]
            for h in reversed(range(DIL_HEADS_PER_GROUP - 1)):
                packed = jnp.where(lane < (h + 1) * LSE_LANES, lse[g0 + h], packed)
            lse_slab[pl.ds(j * BLOCK * dilation + r, BLOCK, stride=dilation), :] = packed

    if not others:
        lse_ref[0] = lse_slab[...]
        for h in range(DIL_HEADS_PER_GROUP):
            o_ref[0, :, h * HEAD_DIM:(h + 1) * HEAD_DIM] = o_slab[h].astype(o_ref.dtype)
        return
    lses = [lse_slab[...]] + [lg_ref[0] for _, lg_ref in others]
    top = functools.reduce(jnp.maximum, lses)
    e = [jnp.exp(lg - top) for lg in lses]
    inv = 1.0 / functools.reduce(lambda a, b: a + b, e)
    weights = [eg * inv for eg in e]
    for h in range(DIL_HEADS_PER_GROUP):
        cols = slice(h * HEAD_DIM, (h + 1) * HEAD_DIM)
        lse_col = slice(h * LSE_LANES, h * LSE_LANES + 1)
        outs = [o_slab[h]] + [og_ref[0, :, cols].astype(F32) for og_ref, _ in others]
        mixed = functools.reduce(lambda a, b: a + b, [wg[:, lse_col] * og for wg, og in zip(weights, outs)])
        o_ref[0, :, cols] = mixed.astype(o_ref.dtype)


def _dil_attention(qkv, batch, group, slopes, n_sub, others=()):
    window, dilation = DIL_GROUPS[group]
    s = qkv.shape[0] // batch
    sub_len = s // dilation
    n_blk = sub_len // BLOCK
    n_sub = min(n_sub, n_blk)
    has_prev = n_blk > 1
    assert window // dilation == BLOCK and sub_len % BLOCK == 0 and n_blk % n_sub == 0
    view = qkv.reshape(batch, dilation, sub_len, 3 * DIL_OUT)
    rows = n_sub * BLOCK

    def cur(col):
        return pl.BlockSpec((1, dilation, rows, DIL_OUT), lambda i, n: (i, 0, n, col))

    def prev(col):
        return pl.BlockSpec((1, dilation, BLOCK, DIL_OUT),
                            lambda i, n: (i, 0, jnp.maximum(n * n_sub - 1, 0), col))

    if has_prev:
        in_specs = [cur(0), cur(1), prev(1), cur(2), prev(2)]
    else:
        in_specs = [cur(0), cur(1), cur(2)]
    args = [view] * len(in_specs)
    tile = pl.BlockSpec((1, dilation * rows, DIL_OUT), lambda i, n: (i, n, 0))
    lse_tile = pl.BlockSpec((1, dilation * rows, LANES), lambda i, n: (i, n, 0))
    for o_g, lse_g in others:
        in_specs += [tile, lse_tile]
        args += [o_g, lse_g]
    o_shape = jax.ShapeDtypeStruct((batch, s, DIL_OUT), BF16)
    lse_shape = jax.ShapeDtypeStruct((batch, s, LANES), F32)
    return pl.pallas_call(
        functools.partial(_dil_kernel, n_sub=n_sub, has_prev=has_prev, dilation=dilation,
                          slopes=slopes, n_other=len(others)),
        out_shape=o_shape if others else (o_shape, lse_shape),
        grid=(batch, n_blk // n_sub),
        in_specs=in_specs,
        out_specs=tile if others else (tile, lse_tile),
        scratch_shapes=[pltpu.VMEM((DIL_HEADS_PER_GROUP, dilation * rows, HEAD_DIM), F32),
                        pltpu.VMEM((dilation * rows, LANES), F32)],
        compiler_params=_params(2, 40),
        name=f"dil_attn_d{dilation}",
    )(*args)


def _post_kernel(osb_ref, odl_ref, gp_ref, bg_ref, x_ref, mod_ref, g2_ref, wsb_hbm, wdl_hbm, wout_hbm,
                 x1_ref, h2_ref, wsb_ref, wdl_ref, wout_ref, stage_ref, sem_ref, *, layer):
    @pl.when(pl.program_id(0) == 0)
    def _():
        for w_hbm, wb_ref in ((wsb_hbm, wsb_ref), (wdl_hbm, wdl_ref), (wout_hbm, wout_ref)):
            _load_weight_bf16(w_hbm, layer, wb_ref, stage_ref, sem_ref)

    d = x_ref.shape[-1]
    chunks = _row_chunks(x_ref.shape[0])
    y_sb = [_dot(osb_ref[c, :], wsb_ref[...]) for c in chunks]
    y_dl = [_dot(odl_ref[c, :], wdl_ref[...]) for c in chunks]
    gates = [_sigmoid(gp_ref[c, :].astype(F32) + bg_ref[...]) for c in chunks]
    mixed = [(g[:, :d] * a + g[:, d:] * b).astype(BF16) for g, a, b in zip(gates, y_sb, y_dl)]
    t = [_dot(mx, wout_ref[...]) for mx in mixed]
    for c, tc in zip(chunks, t):
        x1 = x_ref[c, :] + mod_ref[0, 2:3, :] * tc
        x1_ref[c, :] = x1
        h2 = _rms(x1) * g2_ref[...] * (1.0 + mod_ref[0, 4:5, :]) + mod_ref[0, 3:4, :]
        h2_ref[c, :] = h2.astype(h2_ref.dtype)


def _post(o_sb, o_dl, gate_pre, b_gate, x2d, mod3, g2, w_sb, w_dl, w_out, layer, seq, tm=512):
    m, d = x2d.shape
    weights = (w_sb, w_dl, w_out)
    assert all(w.shape[1] % W_STAGE_ROWS == 0 and w.shape[2] == d for w in weights)

    def row(width):
        return pl.BlockSpec((tm, width), lambda i: (i, 0))

    def whole(arr):
        return pl.BlockSpec(arr.shape, lambda i: (0,) * arr.ndim, pipeline_mode=pl.Buffered(1))

    return pl.pallas_call(
        functools.partial(_post_kernel, layer=layer),
        out_shape=(jax.ShapeDtypeStruct((m, d), F32), jax.ShapeDtypeStruct((m, d), BF16)),
        grid=(m // tm,),
        in_specs=[row(SB_W), row(DIL_OUT), row(2 * d), whole(b_gate), row(d),
                  pl.BlockSpec((1, N_MOD, d), lambda i: ((i * tm) // seq, 0, 0)),
                  whole(g2)] + [pl.BlockSpec(memory_space=pl.ANY)] * len(weights),
        out_specs=(row(d), row(d)),
        scratch_shapes=[pltpu.VMEM(w.shape[1:], BF16) for w in weights]
                       + [pltpu.VMEM((W_STAGE_SLOTS, W_STAGE_ROWS, d), F32),
                          pltpu.SemaphoreType.DMA((W_STAGE_SLOTS,))],
        compiler_params=_params(1, 60),
        name="post",
    )(o_sb, o_dl, gate_pre, b_gate, x2d, mod3, g2, *weights)


def _swiglu_kernel(h_ref, wg_ref, wu_ref, o_ref, wgb_ref, wub_ref):
    @pl.when(pl.program_id(1) == 0)
    def _():
        wgb_ref[...] = wg_ref[...].astype(BF16)
        wub_ref[...] = wu_ref[...].astype(BF16)

    for c in _row_chunks(h_ref.shape[0], 4 * SUB_ROWS):
        g = _dot(h_ref[c, :], wgb_ref[...])
        u = _dot(h_ref[c, :], wub_ref[...])
        o_ref[c, :] = (g * _sigmoid(g) * u).astype(o_ref.dtype)


def _swiglu(h, wg, wu, layer, tm=2048, tn=512):
    m, k = h.shape
    n = wg.shape[-1]
    w_spec = pl.BlockSpec((None, k, tn), lambda j, i: (layer, 0, j))
    return pl.pallas_call(
        _swiglu_kernel,
        out_shape=jax.ShapeDtypeStruct((m, n), BF16),
        grid=(n // tn, m // tm),
        in_specs=[pl.BlockSpec((tm, k), lambda j, i: (i, 0)), w_spec, w_spec],
        out_specs=pl.BlockSpec((tm, tn), lambda j, i: (i, j)),
        scratch_shapes=[pltpu.VMEM((k, tn), BF16), pltpu.VMEM((k, tn), BF16)],
        compiler_params=_params(2, 60),
        name="swiglu",
    )(h, wg, wu)


def _load_weight_bf16(w_hbm, layer, dst_ref, stage_ref, sem_ref):
    n_chunks = dst_ref.shape[0] // W_STAGE_ROWS
    n_slots = stage_ref.shape[0]
    ahead = n_slots - 1
    assert n_chunks >= ahead

    def chunk_copy(c, slot):
        src = w_hbm.at[layer, pl.ds(c * W_STAGE_ROWS, W_STAGE_ROWS), :]
        return pltpu.make_async_copy(src, stage_ref.at[slot], sem_ref.at[slot])

    for c in range(ahead):
        chunk_copy(c, c).start()

    def body(c, carry):
        slot = lax.rem(c, n_slots)

        @pl.when(c + ahead < n_chunks)
        def _():
            chunk_copy(c + ahead, lax.rem(c + ahead, n_slots)).start()

        chunk_copy(c, slot).wait()
        rows = pl.ds(pl.multiple_of(c * W_STAGE_ROWS, W_STAGE_ROWS), W_STAGE_ROWS)
        dst_ref[rows, :] = stage_ref[slot].astype(dst_ref.dtype)
        return carry

    lax.fori_loop(0, n_chunks, body, 0)


def _down_kernel(a_ref, w_hbm, x1_ref, mod_ref, gf_ref, o_ref, wb_ref, stage_ref, sem_ref, *, layer, final_norm):
    @pl.when(pl.program_id(0) == 0)
    def _():
        _load_weight_bf16(w_hbm, layer, wb_ref, stage_ref, sem_ref)

    for c in _row_chunks(a_ref.shape[0]):
        x2 = x1_ref[c, :] + mod_ref[0, 5:6, :] * _dot(a_ref[c, :], wb_ref[...])
        o_ref[c, :] = _rms(x2) * gf_ref[...] if final_norm else x2


def _down(a, w, layer, x1, mod3, g_final, seq, final_norm, tm=512):
    m, f = a.shape
    d = w.shape[-1]
    assert f % W_STAGE_ROWS == 0
    return pl.pallas_call(
        functools.partial(_down_kernel, layer=layer, final_norm=final_norm),
        out_shape=jax.ShapeDtypeStruct((m, d), F32),
        grid=(m // tm,),
        in_specs=[pl.BlockSpec((tm, f), lambda i: (i, 0)),
                  pl.BlockSpec(memory_space=pl.ANY),
                  pl.BlockSpec((tm, d), lambda i: (i, 0)),
                  pl.BlockSpec((1, N_MOD, d), lambda i: ((i * tm) // seq, 0, 0)),
                  pl.BlockSpec((1, d), lambda i: (0, 0))],
        out_specs=pl.BlockSpec((tm, d), lambda i: (i, 0)),
        scratch_shapes=[pltpu.VMEM((f, d), BF16), pltpu.VMEM((W_STAGE_SLOTS, W_STAGE_ROWS, d), F32),
                        pltpu.SemaphoreType.DMA((W_STAGE_SLOTS,))],
        compiler_params=_params(1, 60),
        name="down",
    )(a, w, x1, mod3, g_final)


def _alibi_slopes():
    return tuple(2.0 ** (-8.0 * (i + 1.0) / DIL_HEADS) for i in range(DIL_HEADS))


def kernel(x, c, w_ada, b_ada, g_norm1, g_norm2, g_final, w_in, b_gate, w_proj_sb, w_proj_dil,
           w_out, w_ffn_gate, w_ffn_up, w_ffn_down):
    batch, seq, d = x.shape
    tokens = batch * seq
    n_layers = w_ada.shape[0]
    slopes = _alibi_slopes()
    n_groups = len(DIL_GROUPS)
    sb_tile = 256
    idx = jnp.arange(sb_tile)
    tri = (idx[:, None] > idx[None, :]).astype(BF16)
    c_pad = jnp.pad(c, ((0, ADA_ROWS - batch), (0, 0)))
    perm_dils = tuple(sorted(dil for _, dil in DIL_GROUPS if dil > 1))
    dl_col0 = 3 * SB_W // W_COL
    gate_col0 = dl_col0 + 3 * DIL_W // W_COL
    assert DIL_OUT == W_COL
    group_order = sorted(range(n_groups), key=lambda g: -DIL_GROUPS[g][1])
    assert DIL_GROUPS[group_order[-1]][1] == 1

    for l in range(n_layers):
        mod3 = _ada(c_pad, w_ada[l], b_ada[l][None, :]).reshape(ADA_ROWS, N_MOD, d)

        h_all = _norm1(x, g_norm1[l][None, :], mod3, perm_dils)
        h = h_all[0].reshape(tokens, d)
        h_perm = dict(zip(perm_dils, (hp.reshape(tokens, d) for hp in h_all[1:])))

        sb_tiles = 3 * SB_W // (2 * W_COL)
        qkv_sb = _proj(h, w_in, l, lambda j, r: 2 * j + r, sb_tiles, 2, "proj_sb",
                       query_blocks=[(j, r) for j in range(sb_tiles // 3) for r in range(2)])
        o_sb, gate_pre = _sb_attention_and_gate(qkv_sb.reshape(batch, seq, 3 * SB_W), tri, h_all[0], w_in, l,
                                                gate_col0, 2 * d, sb_tile, sb_tile)
        gate_pre = gate_pre.reshape(tokens, 2 * d)

        partial_groups = []
        for g in group_order:
            dil = DIL_GROUPS[g][1]
            qkv_g = _proj(h_perm.get(dil, h), w_in, l,
                          lambda j, r, g=g: dl_col0 + r * n_groups + g, 1, 3, f"proj_dil_d{dil}",
                          query_blocks=[(0, 0)])
            g_slopes = slopes[g * DIL_HEADS_PER_GROUP:(g + 1) * DIL_HEADS_PER_GROUP]
            if g != group_order[-1]:
                partial_groups.append(_dil_attention(qkv_g, batch, g, g_slopes, n_sub=4))
            else:
                o_dl = _dil_attention(qkv_g, batch, g, g_slopes, n_sub=4, others=tuple(partial_groups))

        x1, h2 = _post(o_sb.reshape(tokens, SB_W), o_dl.reshape(tokens, DIL_OUT), gate_pre,
                       b_gate[l][None, :], x.reshape(tokens, d), mod3, g_norm2[l][None, :],
                       w_proj_sb, w_proj_dil, w_out, l, seq)

        a = _swiglu(h2, w_ffn_gate, w_ffn_up, l)
        x = _down(a, w_ffn_down, l, x1, mod3, g_final[None, :], seq,
                  final_norm=(l == n_layers - 1)).reshape(batch, seq, d)
    return x
```

```python
import functools

import jax
import jax.numpy as jnp
from jax import lax
from jax.experimental import pallas as pl
from jax.experimental.pallas import tpu as pltpu

F32 = jnp.float32
BF16 = jnp.bfloat16

HEAD_DIM = 128
SB_HEADS = 8
DIL_GROUPS = ((128, 1), (512, 4), (2048, 16))
DIL_HEADS_PER_GROUP = 4
DIL_HEADS = DIL_HEADS_PER_GROUP * len(DIL_GROUPS)
BLOCK = 128
SB_W = SB_HEADS * HEAD_DIM
DIL_W = DIL_HEADS * HEAD_DIM
DIL_OUT = DIL_HEADS_PER_GROUP * HEAD_DIM
N_MOD = 6
EPS = 1e-6
LOG2E = 1.4426950408889634
LN2 = 0.6931471805599453
QUERY_SCALE = HEAD_DIM ** -0.5 * LOG2E

LANES = 128
MIB = 1024 * 1024
ADA_ROWS = 8
W_COL = 512
W_STAGE_ROWS = 128
W_STAGE_SLOTS = 4
SUB_ROWS = 256
DIL_CHAINS = 8
LSE_LANES = LANES // DIL_HEADS_PER_GROUP
assert DIL_CHAINS % DIL_HEADS_PER_GROUP == 0


V7X_VMEM_MIB = 64
VMEM_STREAMING_MIB = 40
VMEM_RESIDENT_MIB = V7X_VMEM_MIB - 4


def _params(n_axes, vmem_mib):
    return pltpu.CompilerParams(dimension_semantics=("arbitrary",) * n_axes,
                                vmem_limit_bytes=vmem_mib * MIB)


def _dot(a, b):
    return jnp.dot(a, b, preferred_element_type=F32)


def _dot_nt(a, b):
    return lax.dot_general(a, b, (((1,), (1,)), ((), ())), preferred_element_type=F32)


def _sigmoid(x):
    return 1.0 / (1.0 + jnp.exp(-x))


def _rms(x):
    return x * lax.rsqrt(jnp.mean(x * x, axis=-1, keepdims=True) + EPS)


def _row_chunks(rows, size=SUB_ROWS):
    return [slice(r, r + size) for r in range(0, rows, size)]


def _ada_kernel(c_ref, w_ref, b_ref, o_ref):
    c = c_ref[...]
    s = (c * _sigmoid(c)).astype(BF16)
    o_ref[...] = _dot(s, w_ref[...].astype(BF16)) + b_ref[...]


def _ada(c_pad, w, b, tn=1024):
    d, n = w.shape
    return pl.pallas_call(
        _ada_kernel,
        out_shape=jax.ShapeDtypeStruct((ADA_ROWS, n), F32),
        grid=(n // tn,),
        in_specs=[pl.BlockSpec((ADA_ROWS, d), lambda j: (0, 0)),
                  pl.BlockSpec((d, tn), lambda j: (0, j)),
                  pl.BlockSpec((1, tn), lambda j: (0, j))],
        out_specs=pl.BlockSpec((ADA_ROWS, tn), lambda j: (0, j)),
        compiler_params=_params(1, VMEM_STREAMING_MIB),
        name="ada",
    )(c_pad, w, b)


def _norm1_kernel(x_ref, g_ref, mod_ref, o_ref, *rest, dils):
    n_perm = len(dils)
    perm_refs, slabs = rest[:n_perm], rest[n_perm:]
    ts, d_model = x_ref.shape[1], x_ref.shape[2]
    y = _rms(x_ref[0]) * g_ref[...]
    h = y * (1.0 + mod_ref[0, 1:2, :]) + mod_ref[0, 0:1, :]
    o_ref[0] = h.astype(o_ref.dtype)
    n_slab = d_model // LANES
    for s in range(n_slab):
        slabs[0][s] = h[:, s * LANES:(s + 1) * LANES]
    level = {1: slabs[0]}
    for n, (dil, p_ref) in enumerate(zip(dils, perm_refs)):
        base = max(p for p in level if dil % p == 0)
        step, src = dil // base, level[base]
        keep = slabs[n + 1] if n + 1 < len(slabs) else None
        for r in range(dil):
            r_hi, r_lo = divmod(r, base)
            start = r_lo * (ts // base) + r_hi
            for s in range(n_slab):
                rows = src[s, pl.ds(start, ts // dil, stride=step), :]
                p_ref[0, r, :, s * LANES:(s + 1) * LANES] = rows.astype(p_ref.dtype)
                if keep is not None:
                    keep[s, r * (ts // dil):(r + 1) * (ts // dil), :] = rows
        if keep is not None:
            level[dil] = keep


def _norm1(x, g, mod3, dils, ts=512):
    b, s, d = x.shape
    out_shape = [jax.ShapeDtypeStruct((b, s, d), BF16)]
    out_specs = [pl.BlockSpec((1, ts, d), lambda i, j: (i, j, 0))]
    for dil in dils:
        out_shape.append(jax.ShapeDtypeStruct((b, dil, s // dil, d), BF16))
        out_specs.append(pl.BlockSpec((1, dil, ts // dil, d), lambda i, j: (i, 0, j, 0)))
    slab = pltpu.VMEM((d // LANES, ts, LANES), F32)
    return pl.pallas_call(
        functools.partial(_norm1_kernel, dils=dils),
        out_shape=out_shape,
        grid=(b, s // ts),
        in_specs=[pl.BlockSpec((1, ts, d), lambda i, j: (i, j, 0)),
                  pl.BlockSpec((1, d), lambda i, j: (0, 0)),
                  pl.BlockSpec((1, N_MOD, d), lambda i, j: (i, 0, 0))],
        out_specs=out_specs,
        scratch_shapes=[slab] * max(1, len(dils)),
        compiler_params=_params(2, VMEM_STREAMING_MIB),
        name="norm1",
    )(x, g, mod3)


def _proj_kernel(*refs, n_rhs, query_blocks):
    h_ref, w_refs, o_ref, wb_ref = refs[0], refs[1:1 + n_rhs], refs[1 + n_rhs], refs[2 + n_rhs]

    @pl.when(pl.program_id(1) == 0)
    def _():
        for r, w_ref in enumerate(w_refs):
            wb_ref[:, r * W_COL:(r + 1) * W_COL] = w_ref[...].astype(BF16)

    for c in _row_chunks(h_ref.shape[0], 4 * SUB_ROWS):
        for r in range(n_rhs):
            cols = slice(r * W_COL, (r + 1) * W_COL)
            acc = _dot(h_ref[c, :], wb_ref[:, cols])
            tiles = [j for j, rr in query_blocks if rr == r]
            if tiles:
                is_query = functools.reduce(jnp.logical_or, [pl.program_id(0) == j for j in tiles])
                acc = acc * jnp.where(is_query, QUERY_SCALE, 1.0)
            o_ref[c, cols] = acc.astype(o_ref.dtype)


def _proj(h, w_in, layer, col_fn, n_tiles, n_rhs, name, query_blocks, tm=2048):
    m, k = h.shape
    tn = n_rhs * W_COL
    mode = pl.Buffered(1) if n_tiles == 1 else None

    def w_spec(r):
        return pl.BlockSpec((None, k, W_COL), lambda j, i: (layer, 0, col_fn(j, r)), pipeline_mode=mode)

    return pl.pallas_call(
        functools.partial(_proj_kernel, n_rhs=n_rhs, query_blocks=tuple(query_blocks)),
        out_shape=jax.ShapeDtypeStruct((m, n_tiles * tn), BF16),
        grid=(n_tiles, m // tm),
        in_specs=[pl.BlockSpec((tm, k), lambda j, i: (i, 0))] + [w_spec(r) for r in range(n_rhs)],
        out_specs=pl.BlockSpec((tm, tn), lambda j, i: (i, j)),
        scratch_shapes=[pltpu.VMEM((k, tn), BF16)],
        compiler_params=_params(2, VMEM_RESIDENT_MIB),
        name=name,
    )(h, *([w_in] * n_rhs))


def _sb_kernel(*refs, n_rhs, tq, bk):
    q_ref, k_ref, v_ref, tri_ref, h_ref = refs[:5]
    w_refs, (o_ref, gate_ref, wb_ref) = refs[5:5 + n_rhs], refs[5 + n_rhs:]

    @pl.when(pl.program_id(1) == 0)
    def _():
        for r, w_ref in enumerate(w_refs):
            wb_ref[:, r * W_COL:(r + 1) * W_COL] = w_ref[...].astype(BF16)

    seq = q_ref.shape[1]
    heads = [slice(h * HEAD_DIM, (h + 1) * HEAD_DIM) for h in range(q_ref.shape[2] // HEAD_DIM)]
    tri = tri_ref[...]
    row = lax.broadcasted_iota(jnp.int32, (tq, bk), 0)
    col = lax.broadcasted_iota(jnp.int32, (tq, bk), 1)
    gate_pieces = [(slice(r0, r0 + SUB_ROWS), slice(c0, c0 + SUB_ROWS))
                   for r0 in range(0, seq, SUB_ROWS) for c0 in range(0, n_rhs * W_COL, SUB_ROWS)]
    n_blocks = sum((i + 1) * tq // bk for i in range(seq // tq))
    pieces_per_block = -(-len(gate_pieces) // n_blocks)

    def emit_gate_pieces(count):
        for _ in range(min(count, len(gate_pieces))):
            g_rows, g_cols = gate_pieces.pop(0)
            gate_ref[0, g_rows, g_cols] = _dot(h_ref[0, g_rows, :], wb_ref[:, g_cols]).astype(gate_ref.dtype)

    for i in range(seq // tq):
        n_keys = (i + 1) * tq
        rows = slice(i * tq, (i + 1) * tq)
        z_all = [_dot_nt(q_ref[0, rows, hd], k_ref[0, :n_keys, hd]) for hd in heads]
        run = [jnp.zeros((tq, 1), F32) for _ in heads]
        weights = [[None] * (n_keys // bk) for _ in heads]
        for jb in reversed(range(n_keys // bk)):
            emit_gate_pieces(pieces_per_block)
            on_diagonal = (jb + 1) * bk > i * tq
            keep = col + (jb * bk - i * tq) < row
            z = [za[:, jb * bk:(jb + 1) * bk] for za in z_all]
            sp = [jnp.maximum(zh, 0.0) + jnp.log(1.0 + jnp.exp2(-jnp.abs(zh))) * LOG2E for zh in z]
            own = [zh - sh for zh, sh in zip(z, sp)]
            if on_diagonal:
                sp = [jnp.where(keep, sh, 0.0) for sh in sp]
            sp16 = [sh.astype(BF16) for sh in sp]
            later = [_dot(sh, tri) for sh in sp16]
            a = [jnp.exp2(oh - (rh + lh)) for oh, rh, lh in zip(own, run, later)]
            if on_diagonal:
                a = [jnp.where(keep, ah, 0.0) for ah in a]
            for h, ah in enumerate(a):
                weights[h][jb] = ah.astype(BF16)
            run = [rh + lh[:, 0:1] + sh[:, 0:1].astype(F32) for rh, lh, sh in zip(run, later, sp16)]
        for h, hd in enumerate(heads):
            a_all = weights[h][0] if len(weights[h]) == 1 else jnp.concatenate(weights[h], axis=1)
            o_ref[0, rows, hd] = _dot(a_all, v_ref[0, :n_keys, hd]).astype(o_ref.dtype)
    emit_gate_pieces(len(gate_pieces))


def _sb_attention_and_gate(qkv, tri, h, w_in, layer, gate_col0, gate_width, tq, bk, heads_per_step=2):
    b, s, k = h.shape
    n_steps = SB_HEADS // heads_per_step
    n_rhs = gate_width // (n_steps * W_COL)
    assert n_rhs * n_steps * W_COL == gate_width
    tn = n_rhs * W_COL
    blk = (1, s, heads_per_step * HEAD_DIM)

    def w_spec(r):
        return pl.BlockSpec((None, k, W_COL), lambda p, i: (layer, 0, gate_col0 + p * n_rhs + r),
                            pipeline_mode=pl.Buffered(1))

    return pl.pallas_call(
        functools.partial(_sb_kernel, n_rhs=n_rhs, tq=tq, bk=bk),
        out_shape=(jax.ShapeDtypeStruct((b, s, SB_W), BF16), jax.ShapeDtypeStruct((b, s, gate_width), BF16)),
        grid=(n_steps, b),
        in_specs=[pl.BlockSpec(blk, lambda p, i: (i, 0, p)),
                  pl.BlockSpec(blk, lambda p, i: (i, 0, n_steps + p)),
                  pl.BlockSpec(blk, lambda p, i: (i, 0, 2 * n_steps + p)),
                  pl.BlockSpec((bk, bk), lambda p, i: (0, 0)),
                  pl.BlockSpec((1, s, k), lambda p, i: (i, 0, 0))] + [w_spec(r) for r in range(n_rhs)],
        out_specs=(pl.BlockSpec(blk, lambda p, i: (i, 0, p)), pl.BlockSpec((1, s, tn), lambda p, i: (i, 0, p))),
        scratch_shapes=[pltpu.VMEM((k, tn), BF16)],
        compiler_params=_params(2, VMEM_RESIDENT_MIB),
        name="sb_attn_gate",
    )(qkv, qkv, qkv, tri, h, *([w_in] * n_rhs))


def _dil_kernel(*refs, n_sub, has_prev, dilation, slopes, n_other):
    refs = list(refs)
    q_ref, kc_ref = refs.pop(0), refs.pop(0)
    kp_ref = refs.pop(0) if has_prev else None
    vc_ref = refs.pop(0)
    vp_ref = refs.pop(0) if has_prev else None
    others = [(refs.pop(0), refs.pop(0)) for _ in range(n_other)]
    o_ref = refs.pop(0)
    lse_ref = None if n_other else refs.pop(0)
    o_slab, lse_slab = refs

    n_keys = 2 * BLOCK if has_prev else BLOCK
    a_idx = lax.broadcasted_iota(jnp.int32, (BLOCK, n_keys), 0)
    b_idx = lax.broadcasted_iota(jnp.int32, (BLOCK, n_keys), 1)
    rel = a_idx + (n_keys - BLOCK) - b_idx
    in_window = (rel >= 0) & (rel <= BLOCK)
    dist = (dilation * rel).astype(F32)
    bias = [(slope * LOG2E) * dist for slope in slopes]
    lane = lax.broadcasted_iota(jnp.int32, (BLOCK, LANES), 1)
    step = jnp.zeros((BLOCK, n_keys), jnp.int32) + pl.program_id(1)
    valid_first = in_window & ((b_idx >= BLOCK) | (step > 0)) if has_prev else in_window

    def operands(r, j, h):
        rows = slice(j * BLOCK, (j + 1) * BLOCK)
        cols = slice(h * HEAD_DIM, (h + 1) * HEAD_DIM)
        q, kk, vv = q_ref[0, r, rows, cols], kc_ref[0, r, rows, cols], vc_ref[0, r, rows, cols]
        if has_prev:
            if j == 0:
                k_prev, v_prev = kp_ref[0, r, :, cols], vp_ref[0, r, :, cols]
            else:
                prev_rows = slice((j - 1) * BLOCK, j * BLOCK)
                k_prev, v_prev = kc_ref[0, r, prev_rows, cols], vc_ref[0, r, prev_rows, cols]
            kk = jnp.concatenate([k_prev, kk], axis=0)
            vv = jnp.concatenate([v_prev, vv], axis=0)
        return q, kk, vv

    blocks = [(r, j, h) for r in range(dilation) for j in range(n_sub) for h in range(DIL_HEADS_PER_GROUP)]
    for c0 in range(0, len(blocks), DIL_CHAINS):
        chunk = blocks[c0:c0 + DIL_CHAINS]
        ops = [operands(*blk) for blk in chunk]
        s = [jnp.where(valid_first if j == 0 else in_window, _dot_nt(q, kk) - bias[h], -jnp.inf)
             for (r, j, h), (q, kk, vv) in zip(chunk, ops)]
        m = [jnp.max(si, axis=-1, keepdims=True) for si in s]
        p = [jnp.exp2(si - mi) for si, mi in zip(s, m)]
        l = [jnp.sum(pi, axis=-1, keepdims=True) for pi in p]
        o = [_dot(pi.astype(BF16), vv) / li for pi, li, (q, kk, vv) in zip(p, l, ops)]
        lse = [mi * LN2 + jnp.log(li) for mi, li in zip(m, l)]
        for (r, j, h), oi in zip(chunk, o):
            o_slab[h, pl.ds(j * BLOCK * dilation + r, BLOCK, stride=dilation), :] = oi
        for g0 in range(0, len(chunk), DIL_HEADS_PER_GROUP):
            r, j, _ = chunk[g0]
            packed = lse[g0 + DIL_HEADS_PER_GROUP - 1]
            for h in reversed(range(DIL_HEADS_PER_GROUP - 1)):
                packed = jnp.where(lane < (h + 1) * LSE_LANES, lse[g0 + h], packed)
            lse_slab[pl.ds(j * BLOCK * dilation + r, BLOCK, stride=dilation), :] = packed

    if not others:
        lse_ref[0] = lse_slab[...]
        for h in range(DIL_HEADS_PER_GROUP):
            o_ref[0, :, h * HEAD_DIM:(h + 1) * HEAD_DIM] = o_slab[h].astype(o_ref.dtype)
        return
    lses = [lse_slab[...]] + [lg_ref[0] for _, lg_ref in others]
    top = functools.reduce(jnp.maximum, lses)
    e = [jnp.exp(lg - top) for lg in lses]
    inv = 1.0 / functools.reduce(lambda a, b: a + b, e)
    weights = [eg * inv for eg in e]
    for h in range(DIL_HEADS_PER_GROUP):
        cols = slice(h * HEAD_DIM, (h + 1) * HEAD_DIM)
        lse_col = slice(h * LSE_LANES, h * LSE_LANES + 1)
        outs = [o_slab[h]] + [og_ref[0, :, cols].astype(F32) for og_ref, _ in others]
        mixed = functools.reduce(lambda a, b: a + b, [wg[:, lse_col] * og for wg, og in zip(weights, outs)])
        o_ref[0, :, cols] = mixed.astype(o_ref.dtype)


def _dil_attention(qkv, batch, group, slopes, n_sub, others=()):
    window, dilation = DIL_GROUPS[group]
    s = qkv.shape[0] // batch
    sub_len = s // dilation
    n_blk = sub_len // BLOCK
    n_sub = min(n_sub, n_blk)
    has_prev = n_blk > 1
    assert window // dilation == BLOCK and sub_len % BLOCK == 0 and n_blk % n_sub == 0
    view = qkv.reshape(batch, dilation, sub_len, 3 * DIL_OUT)
    rows = n_sub * BLOCK

    def cur(col):
        return pl.BlockSpec((1, dilation, rows, DIL_OUT), lambda i, n: (i, 0, n, col))

    def prev(col):
        return pl.BlockSpec((1, dilation, BLOCK, DIL_OUT),
                            lambda i, n: (i, 0, jnp.maximum(n * n_sub - 1, 0), col))

    if has_prev:
        in_specs = [cur(0), cur(1), prev(1), cur(2), prev(2)]
    else:
        in_specs = [cur(0), cur(1), cur(2)]
    args = [view] * len(in_specs)
    tile = pl.BlockSpec((1, dilation * rows, DIL_OUT), lambda i, n: (i, n, 0))
    lse_tile = pl.BlockSpec((1, dilation * rows, LANES), lambda i, n: (i, n, 0))
    for o_g, lse_g in others:
        in_specs += [tile, lse_tile]
        args += [o_g, lse_g]
    o_shape = jax.ShapeDtypeStruct((batch, s, DIL_OUT), BF16)
    lse_shape = jax.ShapeDtypeStruct((batch, s, LANES), F32)
    return pl.pallas_call(
        functools.partial(_dil_kernel, n_sub=n_sub, has_prev=has_prev, dilation=dilation,
                          slopes=slopes, n_other=len(others)),
        out_shape=o_shape if others else (o_shape, lse_shape),
        grid=(batch, n_blk // n_sub),
        in_specs=in_specs,
        out_specs=tile if others else (tile, lse_tile),
        scratch_shapes=[pltpu.VMEM((DIL_HEADS_PER_GROUP, dilation * rows, HEAD_DIM), F32),
                        pltpu.VMEM((dilation * rows, LANES), F32)],
        compiler_params=_params(2, VMEM_STREAMING_MIB),
        name=f"dil_attn_d{dilation}",
    )(*args)


def _post_kernel(osb_ref, odl_ref, gp_ref, bg_ref, x_ref, mod_ref, g2_ref, wsb_hbm, wdl_hbm, wout_hbm,
                 x1_ref, h2_ref, wsb_ref, wdl_ref, wout_ref, stage_ref, sem_ref, *, layer):
    @pl.when(pl.program_id(0) == 0)
    def _():
        for w_hbm, wb_ref in ((wsb_hbm, wsb_ref), (wdl_hbm, wdl_ref), (wout_hbm, wout_ref)):
            _load_weight_bf16(w_hbm, layer, wb_ref, stage_ref, sem_ref)

    d = x_ref.shape[-1]
    chunks = _row_chunks(x_ref.shape[0])
    y_sb = [_dot(osb_ref[c, :], wsb_ref[...]) for c in chunks]
    y_dl = [_dot(odl_ref[c, :], wdl_ref[...]) for c in chunks]
    gates = [_sigmoid(gp_ref[c, :].astype(F32) + bg_ref[...]) for c in chunks]
    mixed = [(g[:, :d] * a + g[:, d:] * b).astype(BF16) for g, a, b in zip(gates, y_sb, y_dl)]
    t = [_dot(mx, wout_ref[...]) for mx in mixed]
    for c, tc in zip(chunks, t):
        x1 = x_ref[c, :] + mod_ref[0, 2:3, :] * tc
        x1_ref[c, :] = x1
        h2 = _rms(x1) * g2_ref[...] * (1.0 + mod_ref[0, 4:5, :]) + mod_ref[0, 3:4, :]
        h2_ref[c, :] = h2.astype(h2_ref.dtype)


def _post(o_sb, o_dl, gate_pre, b_gate, x2d, mod3, g2, w_sb, w_dl, w_out, layer, seq, tm=512):
    m, d = x2d.shape
    weights = (w_sb, w_dl, w_out)
    assert all(w.shape[1] % W_STAGE_ROWS == 0 and w.shape[2] == d for w in weights)

    def row(width):
        return pl.BlockSpec((tm, width), lambda i: (i, 0))

    def whole(arr):
        return pl.BlockSpec(arr.shape, lambda i: (0,) * arr.ndim, pipeline_mode=pl.Buffered(1))

    return pl.pallas_call(
        functools.partial(_post_kernel, layer=layer),
        out_shape=(jax.ShapeDtypeStruct((m, d), F32), jax.ShapeDtypeStruct((m, d), BF16)),
        grid=(m // tm,),
        in_specs=[row(SB_W), row(DIL_OUT), row(2 * d), whole(b_gate), row(d),
                  pl.BlockSpec((1, N_MOD, d), lambda i: ((i * tm) // seq, 0, 0)),
                  whole(g2)] + [pl.BlockSpec(memory_space=pl.ANY)] * len(weights),
        out_specs=(row(d), row(d)),
        scratch_shapes=[pltpu.VMEM(w.shape[1:], BF16) for w in weights]
                       + [pltpu.VMEM((W_STAGE_SLOTS, W_STAGE_ROWS, d), F32),
                          pltpu.SemaphoreType.DMA((W_STAGE_SLOTS,))],
        compiler_params=_params(1, VMEM_RESIDENT_MIB),
        name="post",
    )(o_sb, o_dl, gate_pre, b_gate, x2d, mod3, g2, *weights)


def _swiglu_kernel(h_ref, wg_ref, wu_ref, o_ref, wgb_ref, wub_ref):
    @pl.when(pl.program_id(1) == 0)
    def _():
        wgb_ref[...] = wg_ref[...].astype(BF16)
        wub_ref[...] = wu_ref[...].astype(BF16)

    for c in _row_chunks(h_ref.shape[0], 4 * SUB_ROWS):
        g = _dot(h_ref[c, :], wgb_ref[...])
        u = _dot(h_ref[c, :], wub_ref[...])
        o_ref[c, :] = (g * _sigmoid(g) * u).astype(o_ref.dtype)


def _swiglu(h, wg, wu, layer, tm=2048, tn=512):
    m, k = h.shape
    n = wg.shape[-1]
    w_spec = pl.BlockSpec((None, k, tn), lambda j, i: (layer, 0, j))
    return pl.pallas_call(
        _swiglu_kernel,
        out_shape=jax.ShapeDtypeStruct((m, n), BF16),
        grid=(n // tn, m // tm),
        in_specs=[pl.BlockSpec((tm, k), lambda j, i: (i, 0)), w_spec, w_spec],
        out_specs=pl.BlockSpec((tm, tn), lambda j, i: (i, j)),
        scratch_shapes=[pltpu.VMEM((k, tn), BF16), pltpu.VMEM((k, tn), BF16)],
        compiler_params=_params(2, VMEM_RESIDENT_MIB),
        name="swiglu",
    )(h, wg, wu)


def _load_weight_bf16(w_hbm, layer, dst_ref, stage_ref, sem_ref):
    n_chunks = dst_ref.shape[0] // W_STAGE_ROWS
    n_slots = stage_ref.shape[0]
    ahead = min(n_slots - 1, n_chunks)

    def chunk_copy(c, slot):
        src = w_hbm.at[layer, pl.ds(c * W_STAGE_ROWS, W_STAGE_ROWS), :]
        return pltpu.make_async_copy(src, stage_ref.at[slot], sem_ref.at[slot])

    for c in range(ahead):
        chunk_copy(c, c).start()

    def body(c, carry):
        slot = lax.rem(c, n_slots)

        @pl.when(c + ahead < n_chunks)
        def _():
            chunk_copy(c + ahead, lax.rem(c + ahead, n_slots)).start()

        chunk_copy(c, slot).wait()
        rows = pl.ds(pl.multiple_of(c * W_STAGE_ROWS, W_STAGE_ROWS), W_STAGE_ROWS)
        dst_ref[rows, :] = stage_ref[slot].astype(dst_ref.dtype)
        return carry

    lax.fori_loop(0, n_chunks, body, 0)


def _down_kernel(a_ref, w_hbm, x1_ref, mod_ref, gf_ref, o_ref, wb_ref, stage_ref, sem_ref, *, layer, final_norm):
    @pl.when(pl.program_id(0) == 0)
    def _():
        _load_weight_bf16(w_hbm, layer, wb_ref, stage_ref, sem_ref)

    for c in _row_chunks(a_ref.shape[0]):
        x2 = x1_ref[c, :] + mod_ref[0, 5:6, :] * _dot(a_ref[c, :], wb_ref[...])
        o_ref[c, :] = _rms(x2) * gf_ref[...] if final_norm else x2


def _down(a, w, layer, x1, mod3, g_final, seq, final_norm, tm=512):
    m, f = a.shape
    d = w.shape[-1]
    assert f % W_STAGE_ROWS == 0
    return pl.pallas_call(
        functools.partial(_down_kernel, layer=layer, final_norm=final_norm),
        out_shape=jax.ShapeDtypeStruct((m, d), F32),
        grid=(m // tm,),
        in_specs=[pl.BlockSpec((tm, f), lambda i: (i, 0)),
                  pl.BlockSpec(memory_space=pl.ANY),
                  pl.BlockSpec((tm, d), lambda i: (i, 0)),
                  pl.BlockSpec((1, N_MOD, d), lambda i: ((i * tm) // seq, 0, 0)),
                  pl.BlockSpec((1, d), lambda i: (0, 0))],
        out_specs=pl.BlockSpec((tm, d), lambda i: (i, 0)),
        scratch_shapes=[pltpu.VMEM((f, d), BF16), pltpu.VMEM((W_STAGE_SLOTS, W_STAGE_ROWS, d), F32),
                        pltpu.SemaphoreType.DMA((W_STAGE_SLOTS,))],
        compiler_params=_params(1, VMEM_RESIDENT_MIB),
        name="down",
    )(a, w, x1, mod3, g_final)


def _alibi_slopes():
    return tuple(2.0 ** (-8.0 * (i + 1.0) / DIL_HEADS) for i in range(DIL_HEADS))


def kernel(x, c, w_ada, b_ada, g_norm1, g_norm2, g_final, w_in, b_gate, w_proj_sb, w_proj_dil,
           w_out, w_ffn_gate, w_ffn_up, w_ffn_down):
    batch, seq, d = x.shape
    tokens = batch * seq
    n_layers = w_ada.shape[0]
    slopes = _alibi_slopes()
    n_groups = len(DIL_GROUPS)
    sb_tile = 256
    idx = jnp.arange(sb_tile)
    tri = (idx[:, None] > idx[None, :]).astype(BF16)
    c_pad = jnp.pad(c, ((0, ADA_ROWS - batch), (0, 0)))
    perm_dils = tuple(sorted(dil for _, dil in DIL_GROUPS if dil > 1))
    dl_col0 = 3 * SB_W // W_COL
    gate_col0 = dl_col0 + 3 * DIL_W // W_COL
    assert DIL_OUT == W_COL
    group_order = sorted(range(n_groups), key=lambda g: -DIL_GROUPS[g][1])
    assert DIL_GROUPS[group_order[-1]][1] == 1

    for l in range(n_layers):
        mod3 = _ada(c_pad, w_ada[l], b_ada[l][None, :]).reshape(ADA_ROWS, N_MOD, d)

        h_all = _norm1(x, g_norm1[l][None, :], mod3, perm_dils)
        h = h_all[0].reshape(tokens, d)
        h_perm = dict(zip(perm_dils, (hp.reshape(tokens, d) for hp in h_all[1:])))

        sb_tiles = 3 * SB_W // (2 * W_COL)
        qkv_sb = _proj(h, w_in, l, lambda j, r: 2 * j + r, sb_tiles, 2, "proj_sb",
                       query_blocks=[(j, r) for j in range(sb_tiles // 3) for r in range(2)])
        o_sb, gate_pre = _sb_attention_and_gate(qkv_sb.reshape(batch, seq, 3 * SB_W), tri, h_all[0], w_in, l,
                                                gate_col0, 2 * d, sb_tile, sb_tile)
        gate_pre = gate_pre.reshape(tokens, 2 * d)

        partial_groups = []
        for g in group_order:
            dil = DIL_GROUPS[g][1]
            qkv_g = _proj(h_perm.get(dil, h), w_in, l,
                          lambda j, r, g=g: dl_col0 + r * n_groups + g, 1, 3, f"proj_dil_d{dil}",
                          query_blocks=[(0, 0)])
            g_slopes = slopes[g * DIL_HEADS_PER_GROUP:(g + 1) * DIL_HEADS_PER_GROUP]
            if g != group_order[-1]:
                partial_groups.append(_dil_attention(qkv_g, batch, g, g_slopes, n_sub=4))
            else:
                o_dl = _dil_attention(qkv_g, batch, g, g_slopes, n_sub=4, others=tuple(partial_groups))

        x1, h2 = _post(o_sb.reshape(tokens, SB_W), o_dl.reshape(tokens, DIL_OUT), gate_pre,
                       b_gate[l][None, :], x.reshape(tokens, d), mod3, g_norm2[l][None, :],
                       w_proj_sb, w_proj_dil, w_out, l, seq)

        a = _swiglu(h2, w_ffn_gate, w_ffn_up, l)
        x = _down(a, w_ffn_down, l, x1, mod3, g_final[None, :], seq,
                  final_norm=(l == n_layers - 1)).reshape(batch, seq, d)
    return x
```

```python
import functools

import jax
import jax.numpy as jnp
from jax import lax
from jax.experimental import pallas as pl
from jax.experimental.pallas import tpu as pltpu

F32 = jnp.float32
BF16 = jnp.bfloat16

HEAD_DIM = 128
SB_HEADS = 8
DIL_GROUPS = ((128, 1), (512, 4), (2048, 16))
DIL_HEADS_PER_GROUP = 4
DIL_HEADS = DIL_HEADS_PER_GROUP * len(DIL_GROUPS)
BLOCK = 128
SB_W = SB_HEADS * HEAD_DIM
DIL_W = DIL_HEADS * HEAD_DIM
DIL_OUT = DIL_HEADS_PER_GROUP * HEAD_DIM
N_MOD = 6
EPS = 1e-6
LOG2E = 1.4426950408889634
LN2 = 0.6931471805599453
QUERY_SCALE = HEAD_DIM ** -0.5 * LOG2E

LANES = 128
MIB = 1024 * 1024
ADA_ROWS = 8
W_COL = 512
W_STAGE_ROWS = 128
W_STAGE_SLOTS = 4
SUB_ROWS = 256
DIL_CHAINS = 8
LSE_LANES = LANES // DIL_HEADS_PER_GROUP
assert DIL_CHAINS % DIL_HEADS_PER_GROUP == 0


V7X_VMEM_MIB = 64
VMEM_STREAMING_MIB = 40
VMEM_RESIDENT_MIB = V7X_VMEM_MIB - 4


def _params(n_axes, vmem_mib):
    return pltpu.CompilerParams(dimension_semantics=("arbitrary",) * n_axes,
                                vmem_limit_bytes=vmem_mib * MIB)


def _dot(a, b):
    return jnp.dot(a, b, preferred_element_type=F32)


def _dot_nt(a, b):
    return lax.dot_general(a, b, (((1,), (1,)), ((), ())), preferred_element_type=F32)


def _sigmoid(x):
    return 1.0 / (1.0 + jnp.exp(-x))


def _rms(x):
    return x * lax.rsqrt(jnp.mean(x * x, axis=-1, keepdims=True) + EPS)


def _row_chunks(rows, size=SUB_ROWS):
    return [slice(r, r + size) for r in range(0, rows, size)]


def _ada_kernel(c_ref, w_ref, b_ref, o_ref):
    c = c_ref[...]
    s = (c * _sigmoid(c)).astype(BF16)
    o_ref[...] = _dot(s, w_ref[...].astype(BF16)) + b_ref[...]


def _ada(c_pad, w, b, tn=1024):
    d, n = w.shape
    return pl.pallas_call(
        _ada_kernel,
        out_shape=jax.ShapeDtypeStruct((ADA_ROWS, n), F32),
        grid=(n // tn,),
        in_specs=[pl.BlockSpec((ADA_ROWS, d), lambda j: (0, 0)),
                  pl.BlockSpec((d, tn), lambda j: (0, j)),
                  pl.BlockSpec((1, tn), lambda j: (0, j))],
        out_specs=pl.BlockSpec((ADA_ROWS, tn), lambda j: (0, j)),
        compiler_params=_params(1, VMEM_STREAMING_MIB),
        name="ada",
    )(c_pad, w, b)


def _norm1_kernel(x_ref, g_ref, mod_ref, o_ref, *rest, dils):
    n_perm = len(dils)
    perm_refs, slabs = rest[:n_perm], rest[n_perm:]
    ts, d_model = x_ref.shape[1], x_ref.shape[2]
    y = _rms(x_ref[0]) * g_ref[...]
    h = y * (1.0 + mod_ref[0, 1:2, :]) + mod_ref[0, 0:1, :]
    o_ref[0] = h.astype(o_ref.dtype)
    n_slab = d_model // LANES
    for s in range(n_slab):
        slabs[0][s] = h[:, s * LANES:(s + 1) * LANES]
    level = {1: slabs[0]}
    for n, (dil, p_ref) in enumerate(zip(dils, perm_refs)):
        base = max(p for p in level if dil % p == 0)
        step, src = dil // base, level[base]
        keep = slabs[n + 1] if n + 1 < len(slabs) else None
        for r in range(dil):
            r_hi, r_lo = divmod(r, base)
            start = r_lo * (ts // base) + r_hi
            for s in range(n_slab):
                rows = src[s, pl.ds(start, ts // dil, stride=step), :]
                p_ref[0, r, :, s * LANES:(s + 1) * LANES] = rows.astype(p_ref.dtype)
                if keep is not None:
                    keep[s, r * (ts // dil):(r + 1) * (ts // dil), :] = rows
        if keep is not None:
            level[dil] = keep


def _norm1(x, g, mod3, dils, ts=512):
    b, s, d = x.shape
    out_shape = [jax.ShapeDtypeStruct((b, s, d), BF16)]
    out_specs = [pl.BlockSpec((1, ts, d), lambda i, j: (i, j, 0))]
    for dil in dils:
        out_shape.append(jax.ShapeDtypeStruct((b, dil, s // dil, d), BF16))
        out_specs.append(pl.BlockSpec((1, dil, ts // dil, d), lambda i, j: (i, 0, j, 0)))
    slab = pltpu.VMEM((d // LANES, ts, LANES), F32)
    return pl.pallas_call(
        functools.partial(_norm1_kernel, dils=dils),
        out_shape=out_shape,
        grid=(b, s // ts),
        in_specs=[pl.BlockSpec((1, ts, d), lambda i, j: (i, j, 0)),
                  pl.BlockSpec((1, d), lambda i, j: (0, 0)),
                  pl.BlockSpec((1, N_MOD, d), lambda i, j: (i, 0, 0))],
        out_specs=out_specs,
        scratch_shapes=[slab] * max(1, len(dils)),
        compiler_params=_params(2, VMEM_STREAMING_MIB),
        name="norm1",
    )(x, g, mod3)


def _proj_kernel(*refs, n_rhs, query_blocks):
    h_ref, w_refs, o_ref, wb_ref = refs[0], refs[1:1 + n_rhs], refs[1 + n_rhs], refs[2 + n_rhs]

    @pl.when(pl.program_id(1) == 0)
    def _():
        for r, w_ref in enumerate(w_refs):
            wb_ref[:, r * W_COL:(r + 1) * W_COL] = w_ref[...].astype(BF16)

    for c in _row_chunks(h_ref.shape[0], 4 * SUB_ROWS):
        for r in range(n_rhs):
            cols = slice(r * W_COL, (r + 1) * W_COL)
            acc = _dot(h_ref[c, :], wb_ref[:, cols])
            tiles = [j for j, rr in query_blocks if rr == r]
            if tiles:
                is_query = functools.reduce(jnp.logical_or, [pl.program_id(0) == j for j in tiles])
                acc = acc * jnp.where(is_query, QUERY_SCALE, 1.0)
            o_ref[c, cols] = acc.astype(o_ref.dtype)


def _proj(h, w_in, layer, col_fn, n_tiles, n_rhs, name, query_blocks, tm=2048):
    m, k = h.shape
    tn = n_rhs * W_COL
    mode = pl.Buffered(1) if n_tiles == 1 else None

    def w_spec(r):
        return pl.BlockSpec((None, k, W_COL), lambda j, i: (layer, 0, col_fn(j, r)), pipeline_mode=mode)

    return pl.pallas_call(
        functools.partial(_proj_kernel, n_rhs=n_rhs, query_blocks=tuple(query_blocks)),
        out_shape=jax.ShapeDtypeStruct((m, n_tiles * tn), BF16),
        grid=(n_tiles, m // tm),
        in_specs=[pl.BlockSpec((tm, k), lambda j, i: (i, 0))] + [w_spec(r) for r in range(n_rhs)],
        out_specs=pl.BlockSpec((tm, tn), lambda j, i: (i, j)),
        scratch_shapes=[pltpu.VMEM((k, tn), BF16)],
        compiler_params=_params(2, VMEM_RESIDENT_MIB),
        name=name,
    )(h, *([w_in] * n_rhs))


def _sb_kernel(*refs, n_rhs, tq, bk):
    q_ref, k_ref, v_ref, tri_ref, h_ref = refs[:5]
    w_refs, (o_ref, gate_ref, wb_ref) = refs[5:5 + n_rhs], refs[5 + n_rhs:]

    @pl.when(pl.program_id(1) == 0)
    def _():
        for r, w_ref in enumerate(w_refs):
            wb_ref[:, r * W_COL:(r + 1) * W_COL] = w_ref[...].astype(BF16)

    seq = q_ref.shape[1]
    heads = [slice(h * HEAD_DIM, (h + 1) * HEAD_DIM) for h in range(q_ref.shape[2] // HEAD_DIM)]
    tri = tri_ref[...]
    row = lax.broadcasted_iota(jnp.int32, (tq, bk), 0)
    col = lax.broadcasted_iota(jnp.int32, (tq, bk), 1)
    gate_pieces = [(slice(r0, r0 + SUB_ROWS), slice(c0, c0 + SUB_ROWS))
                   for r0 in range(0, seq, SUB_ROWS) for c0 in range(0, n_rhs * W_COL, SUB_ROWS)]
    n_blocks = sum((i + 1) * tq // bk for i in range(seq // tq))
    pieces_per_block = -(-len(gate_pieces) // n_blocks)

    def emit_gate_pieces(count):
        for _ in range(min(count, len(gate_pieces))):
            g_rows, g_cols = gate_pieces.pop(0)
            gate_ref[0, g_rows, g_cols] = _dot(h_ref[0, g_rows, :], wb_ref[:, g_cols]).astype(gate_ref.dtype)

    for i in range(seq // tq):
        n_keys = (i + 1) * tq
        rows = slice(i * tq, (i + 1) * tq)
        z_all = [_dot_nt(q_ref[0, rows, hd], k_ref[0, :n_keys, hd]) for hd in heads]
        run = [jnp.zeros((tq, 1), F32) for _ in heads]
        weights = [[None] * (n_keys // bk) for _ in heads]
        for jb in reversed(range(n_keys // bk)):
            emit_gate_pieces(pieces_per_block)
            on_diagonal = (jb + 1) * bk > i * tq
            keep = col + (jb * bk - i * tq) < row
            z = [za[:, jb * bk:(jb + 1) * bk] for za in z_all]
            sp = [jnp.maximum(zh, 0.0) + jnp.log(1.0 + jnp.exp2(-jnp.abs(zh))) * LOG2E for zh in z]
            own = [zh - sh for zh, sh in zip(z, sp)]
            if on_diagonal:
                sp = [jnp.where(keep, sh, 0.0) for sh in sp]
            sp16 = [sh.astype(BF16) for sh in sp]
            later = [_dot(sh, tri) for sh in sp16]
            a = [jnp.exp2(oh - (rh + lh)) for oh, rh, lh in zip(own, run, later)]
            if on_diagonal:
                a = [jnp.where(keep, ah, 0.0) for ah in a]
            for h, ah in enumerate(a):
                weights[h][jb] = ah.astype(BF16)
            run = [rh + lh[:, 0:1] + sh[:, 0:1].astype(F32) for rh, lh, sh in zip(run, later, sp16)]
        for h, hd in enumerate(heads):
            a_all = weights[h][0] if len(weights[h]) == 1 else jnp.concatenate(weights[h], axis=1)
            o_ref[0, rows, hd] = _dot(a_all, v_ref[0, :n_keys, hd]).astype(o_ref.dtype)
    emit_gate_pieces(len(gate_pieces))


def _sb_attention_and_gate(qkv, tri, h, w_in, layer, gate_col0, gate_width, tq, bk, heads_per_step=2):
    b, s, k = h.shape
    n_steps = SB_HEADS // heads_per_step
    n_rhs = gate_width // (n_steps * W_COL)
    assert n_rhs * n_steps * W_COL == gate_width
    tn = n_rhs * W_COL
    blk = (1, s, heads_per_step * HEAD_DIM)

    def w_spec(r):
        return pl.BlockSpec((None, k, W_COL), lambda p, i: (layer, 0, gate_col0 + p * n_rhs + r),
                            pipeline_mode=pl.Buffered(1))

    return pl.pallas_call(
        functools.partial(_sb_kernel, n_rhs=n_rhs, tq=tq, bk=bk),
        out_shape=(jax.ShapeDtypeStruct((b, s, SB_W), BF16), jax.ShapeDtypeStruct((b, s, gate_width), BF16)),
        grid=(n_steps, b),
        in_specs=[pl.BlockSpec(blk, lambda p, i: (i, 0, p)),
                  pl.BlockSpec(blk, lambda p, i: (i, 0, n_steps + p)),
                  pl.BlockSpec(blk, lambda p, i: (i, 0, 2 * n_steps + p)),
                  pl.BlockSpec((bk, bk), lambda p, i: (0, 0)),
                  pl.BlockSpec((1, s, k), lambda p, i: (i, 0, 0))] + [w_spec(r) for r in range(n_rhs)],
        out_specs=(pl.BlockSpec(blk, lambda p, i: (i, 0, p)), pl.BlockSpec((1, s, tn), lambda p, i: (i, 0, p))),
        scratch_shapes=[pltpu.VMEM((k, tn), BF16)],
        compiler_params=_params(2, VMEM_RESIDENT_MIB),
        name="sb_attn_gate",
    )(qkv, qkv, qkv, tri, h, *([w_in] * n_rhs))


def _dil_kernel(*refs, n_sub, has_prev, dilation, slopes, n_other):
    refs = list(refs)
    q_ref, kc_ref = refs.pop(0), refs.pop(0)
    kp_ref = refs.pop(0) if has_prev else None
    vc_ref = refs.pop(0)
    vp_ref = refs.pop(0) if has_prev else None
    others = [(refs.pop(0), refs.pop(0)) for _ in range(n_other)]
    o_ref = refs.pop(0)
    lse_ref = None if n_other else refs.pop(0)
    o_slab, lse_slab = refs

    n_keys = 2 * BLOCK if has_prev else BLOCK
    a_idx = lax.broadcasted_iota(jnp.int32, (BLOCK, n_keys), 0)
    b_idx = lax.broadcasted_iota(jnp.int32, (BLOCK, n_keys), 1)
    rel = a_idx + (n_keys - BLOCK) - b_idx
    in_window = (rel >= 0) & (rel <= BLOCK)
    dist = (dilation * rel).astype(F32)
    bias = [(slope * LOG2E) * dist for slope in slopes]
    lane = lax.broadcasted_iota(jnp.int32, (BLOCK, LANES), 1)
    step = jnp.zeros((BLOCK, n_keys), jnp.int32) + pl.program_id(1)
    valid_first = in_window & ((b_idx >= BLOCK) | (step > 0)) if has_prev else in_window

    def operands(r, j, h):
        rows = slice(j * BLOCK, (j + 1) * BLOCK)
        cols = slice(h * HEAD_DIM, (h + 1) * HEAD_DIM)
        q, kk, vv = q_ref[0, r, rows, cols], kc_ref[0, r, rows, cols], vc_ref[0, r, rows, cols]
        if has_prev:
            if j == 0:
                k_prev, v_prev = kp_ref[0, r, :, cols], vp_ref[0, r, :, cols]
            else:
                prev_rows = slice((j - 1) * BLOCK, j * BLOCK)
                k_prev, v_prev = kc_ref[0, r, prev_rows, cols], vc_ref[0, r, prev_rows, cols]
            kk = jnp.concatenate([k_prev, kk], axis=0)
            vv = jnp.concatenate([v_prev, vv], axis=0)
        return q, kk, vv

    blocks = [(r, j, h) for r in range(dilation) for j in range(n_sub) for h in range(DIL_HEADS_PER_GROUP)]
    for c0 in range(0, len(blocks), DIL_CHAINS):
        chunk = blocks[c0:c0 + DIL_CHAINS]
        ops = [operands(*blk) for blk in chunk]
        s = [jnp.where(valid_first if j == 0 else in_window, _dot_nt(q, kk) - bias[h], -jnp.inf)
             for (r, j, h), (q, kk, vv) in zip(chunk, ops)]
        m = [jnp.max(si, axis=-1, keepdims=True) for si in s]
        p = [jnp.exp2(si - mi) for si, mi in zip(s, m)]
        l = [jnp.sum(pi, axis=-1, keepdims=True) for pi in p]
        o = [_dot(pi.astype(BF16), vv) / li for pi, li, (q, kk, vv) in zip(p, l, ops)]
        lse = [mi * LN2 + jnp.log(li) for mi, li in zip(m, l)]
        for (r, j, h), oi in zip(chunk, o):
            o_slab[h, pl.ds(j * BLOCK * dilation + r, BLOCK, stride=dilation), :] = oi
        for g0 in range(0, len(chunk), DIL_HEADS_PER_GROUP):
            r, j, _ = chunk[g0]
            packed = lse[g0 + DIL_HEADS_PER_GROUP - 1]
            for h in reversed(range(DIL_HEADS_PER_GROUP - 1)):
                packed = jnp.where(lane < (h + 1) * LSE_LANES, lse[g0 + h], packed)
            lse_slab[pl.ds(j * BLOCK * dilation + r, BLOCK, stride=dilation), :] = packed

    if not others:
        lse_ref[0] = lse_slab[...]
        for h in range(DIL_HEADS_PER_GROUP):
            o_ref[0, :, h * HEAD_DIM:(h + 1) * HEAD_DIM] = o_slab[h].astype(o_ref.dtype)
        return
    lses = [lse_slab[...]] + [lg_ref[0] for _, lg_ref in others]
    top = functools.reduce(jnp.maximum, lses)
    e = [jnp.exp(lg - top) for lg in lses]
    inv = 1.0 / functools.reduce(lambda a, b: a + b, e)
    weights = [eg * inv for eg in e]
    for h in range(DIL_HEADS_PER_GROUP):
        cols = slice(h * HEAD_DIM, (h + 1) * HEAD_DIM)
        lse_col = slice(h * LSE_LANES, h * LSE_LANES + 1)
        outs = [o_slab[h]] + [og_ref[0, :, cols].astype(F32) for og_ref, _ in others]
        mixed = functools.reduce(lambda a, b: a + b, [wg[:, lse_col] * og for wg, og in zip(weights, outs)])
        o_ref[0, :, cols] = mixed.astype(o_ref.dtype)


def _dil_attention(qkv, batch, group, slopes, n_sub, others=()):
    window, dilation = DIL_GROUPS[group]
    s = qkv.shape[0] // batch
    sub_len = s // dilation
    n_blk = sub_len // BLOCK
    n_sub = min(n_sub, n_blk)
    has_prev = n_blk > 1
    assert window // dilation == BLOCK and sub_len % BLOCK == 0 and n_blk % n_sub == 0
    view = qkv.reshape(batch, dilation, sub_len, 3 * DIL_OUT)
    rows = n_sub * BLOCK

    def cur(col):
        return pl.BlockSpec((1, dilation, rows, DIL_OUT), lambda i, n: (i, 0, n, col))

    def prev(col):
        return pl.BlockSpec((1, dilation, BLOCK, DIL_OUT),
                            lambda i, n: (i, 0, jnp.maximum(n * n_sub - 1, 0), col))

    if has_prev:
        in_specs = [cur(0), cur(1), prev(1), cur(2), prev(2)]
    else:
        in_specs = [cur(0), cur(1), cur(2)]
    args = [view] * len(in_specs)
    tile = pl.BlockSpec((1, dilation * rows, DIL_OUT), lambda i, n: (i, n, 0))
    lse_tile = pl.BlockSpec((1, dilation * rows, LANES), lambda i, n: (i, n, 0))
    for o_g, lse_g in others:
        in_specs += [tile, lse_tile]
        args += [o_g, lse_g]
    o_shape = jax.ShapeDtypeStruct((batch, s, DIL_OUT), BF16)
    lse_shape = jax.ShapeDtypeStruct((batch, s, LANES), F32)
    return pl.pallas_call(
        functools.partial(_dil_kernel, n_sub=n_sub, has_prev=has_prev, dilation=dilation,
                          slopes=slopes, n_other=len(others)),
        out_shape=o_shape if others else (o_shape, lse_shape),
        grid=(batch, n_blk // n_sub),
        in_specs=in_specs,
        out_specs=tile if others else (tile, lse_tile),
        scratch_shapes=[pltpu.VMEM((DIL_HEADS_PER_GROUP, dilation * rows, HEAD_DIM), F32),
                        pltpu.VMEM((dilation * rows, LANES), F32)],
        compiler_params=_params(2, VMEM_RESIDENT_MIB),
        name=f"dil_attn_d{dilation}",
    )(*args)


def _post_kernel(osb_ref, odl_ref, gp_ref, bg_ref, x_ref, mod_ref, g2_ref, wsb_hbm, wdl_hbm, wout_hbm,
                 x1_ref, h2_ref, wsb_ref, wdl_ref, wout_ref, stage_ref, sem_ref, *, layer):
    @pl.when(pl.program_id(0) == 0)
    def _():
        for w_hbm, wb_ref in ((wsb_hbm, wsb_ref), (wdl_hbm, wdl_ref), (wout_hbm, wout_ref)):
            _load_weight_bf16(w_hbm, layer, wb_ref, stage_ref, sem_ref)

    d = x_ref.shape[-1]
    chunks = _row_chunks(x_ref.shape[0])
    y_sb = [_dot(osb_ref[c, :], wsb_ref[...]) for c in chunks]
    y_dl = [_dot(odl_ref[c, :], wdl_ref[...]) for c in chunks]
    gates = [_sigmoid(gp_ref[c, :].astype(F32) + bg_ref[...]) for c in chunks]
    mixed = [(g[:, :d] * a + g[:, d:] * b).astype(BF16) for g, a, b in zip(gates, y_sb, y_dl)]
    t = [_dot(mx, wout_ref[...]) for mx in mixed]
    for c, tc in zip(chunks, t):
        x1 = x_ref[c, :] + mod_ref[0, 2:3, :] * tc
        x1_ref[c, :] = x1
        h2 = _rms(x1) * g2_ref[...] * (1.0 + mod_ref[0, 4:5, :]) + mod_ref[0, 3:4, :]
        h2_ref[c, :] = h2.astype(h2_ref.dtype)


def _post(o_sb, o_dl, gate_pre, b_gate, x2d, mod3, g2, w_sb, w_dl, w_out, layer, seq, tm=512):
    m, d = x2d.shape
    weights = (w_sb, w_dl, w_out)
    assert all(w.shape[1] % W_STAGE_ROWS == 0 and w.shape[2] == d for w in weights)

    def row(width):
        return pl.BlockSpec((tm, width), lambda i: (i, 0))

    def whole(arr):
        return pl.BlockSpec(arr.shape, lambda i: (0,) * arr.ndim, pipeline_mode=pl.Buffered(1))

    return pl.pallas_call(
        functools.partial(_post_kernel, layer=layer),
        out_shape=(jax.ShapeDtypeStruct((m, d), F32), jax.ShapeDtypeStruct((m, d), BF16)),
        grid=(m // tm,),
        in_specs=[row(SB_W), row(DIL_OUT), row(2 * d), whole(b_gate), row(d),
                  pl.BlockSpec((1, N_MOD, d), lambda i: ((i * tm) // seq, 0, 0)),
                  whole(g2)] + [pl.BlockSpec(memory_space=pl.ANY)] * len(weights),
        out_specs=(row(d), row(d)),
        scratch_shapes=[pltpu.VMEM(w.shape[1:], BF16) for w in weights]
                       + [pltpu.VMEM((W_STAGE_SLOTS, W_STAGE_ROWS, d), F32),
                          pltpu.SemaphoreType.DMA((W_STAGE_SLOTS,))],
        compiler_params=_params(1, VMEM_RESIDENT_MIB),
        name="post",
    )(o_sb, o_dl, gate_pre, b_gate, x2d, mod3, g2, *weights)


def _swiglu_kernel(h_ref, wg_ref, wu_ref, o_ref, wgb_ref, wub_ref):
    @pl.when(pl.program_id(1) == 0)
    def _():
        wgb_ref[...] = wg_ref[...].astype(BF16)
        wub_ref[...] = wu_ref[...].astype(BF16)

    for c in _row_chunks(h_ref.shape[0], 4 * SUB_ROWS):
        g = _dot(h_ref[c, :], wgb_ref[...])
        u = _dot(h_ref[c, :], wub_ref[...])
        o_ref[c, :] = (g * _sigmoid(g) * u).astype(o_ref.dtype)


def _swiglu(h, wg, wu, layer, tm=2048, tn=512):
    m, k = h.shape
    n = wg.shape[-1]
    w_spec = pl.BlockSpec((None, k, tn), lambda j, i: (layer, 0, j))
    return pl.pallas_call(
        _swiglu_kernel,
        out_shape=jax.ShapeDtypeStruct((m, n), BF16),
        grid=(n // tn, m // tm),
        in_specs=[pl.BlockSpec((tm, k), lambda j, i: (i, 0)), w_spec, w_spec],
        out_specs=pl.BlockSpec((tm, tn), lambda j, i: (i, j)),
        scratch_shapes=[pltpu.VMEM((k, tn), BF16), pltpu.VMEM((k, tn), BF16)],
        compiler_params=_params(2, VMEM_RESIDENT_MIB),
        name="swiglu",
    )(h, wg, wu)


def _load_weight_bf16(w_hbm, layer, dst_ref, stage_ref, sem_ref):
    n_chunks = dst_ref.shape[0] // W_STAGE_ROWS
    n_slots = stage_ref.shape[0]
    ahead = min(n_slots - 1, n_chunks)

    def chunk_copy(c, slot):
        src = w_hbm.at[layer, pl.ds(c * W_STAGE_ROWS, W_STAGE_ROWS), :]
        return pltpu.make_async_copy(src, stage_ref.at[slot], sem_ref.at[slot])

    for c in range(ahead):
        chunk_copy(c, c).start()

    def body(c, carry):
        slot = lax.rem(c, n_slots)

        @pl.when(c + ahead < n_chunks)
        def _():
            chunk_copy(c + ahead, lax.rem(c + ahead, n_slots)).start()

        chunk_copy(c, slot).wait()
        rows = pl.ds(pl.multiple_of(c * W_STAGE_ROWS, W_STAGE_ROWS), W_STAGE_ROWS)
        dst_ref[rows, :] = stage_ref[slot].astype(dst_ref.dtype)
        return carry

    lax.fori_loop(0, n_chunks, body, 0)


def _down_kernel(a_ref, w_hbm, x1_ref, mod_ref, gf_ref, o_ref, wb_ref, stage_ref, sem_ref, *, layer, final_norm):
    @pl.when(pl.program_id(0) == 0)
    def _():
        _load_weight_bf16(w_hbm, layer, wb_ref, stage_ref, sem_ref)

    for c in _row_chunks(a_ref.shape[0]):
        x2 = x1_ref[c, :] + mod_ref[0, 5:6, :] * _dot(a_ref[c, :], wb_ref[...])
        o_ref[c, :] = _rms(x2) * gf_ref[...] if final_norm else x2


def _down(a, w, layer, x1, mod3, g_final, seq, final_norm, tm=512):
    m, f = a.shape
    d = w.shape[-1]
    assert f % W_STAGE_ROWS == 0
    return pl.pallas_call(
        functools.partial(_down_kernel, layer=layer, final_norm=final_norm),
        out_shape=jax.ShapeDtypeStruct((m, d), F32),
        grid=(m // tm,),
        in_specs=[pl.BlockSpec((tm, f), lambda i: (i, 0)),
                  pl.BlockSpec(memory_space=pl.ANY),
                  pl.BlockSpec((tm, d), lambda i: (i, 0)),
                  pl.BlockSpec((1, N_MOD, d), lambda i: ((i * tm) // seq, 0, 0)),
                  pl.BlockSpec((1, d), lambda i: (0, 0))],
        out_specs=pl.BlockSpec((tm, d), lambda i: (i, 0)),
        scratch_shapes=[pltpu.VMEM((f, d), BF16), pltpu.VMEM((W_STAGE_SLOTS, W_STAGE_ROWS, d), F32),
                        pltpu.SemaphoreType.DMA((W_STAGE_SLOTS,))],
        compiler_params=_params(1, VMEM_RESIDENT_MIB),
        name="down",
    )(a, w, x1, mod3, g_final)


def _alibi_slopes():
    return tuple(2.0 ** (-8.0 * (i + 1.0) / DIL_HEADS) for i in range(DIL_HEADS))


def kernel(x, c, w_ada, b_ada, g_norm1, g_norm2, g_final, w_in, b_gate, w_proj_sb, w_proj_dil,
           w_out, w_ffn_gate, w_ffn_up, w_ffn_down):
    batch, seq, d = x.shape
    tokens = batch * seq
    n_layers = w_ada.shape[0]
    slopes = _alibi_slopes()
    n_groups = len(DIL_GROUPS)
    sb_tile = 256
    idx = jnp.arange(sb_tile)
    tri = (idx[:, None] > idx[None, :]).astype(BF16)
    c_pad = jnp.pad(c, ((0, ADA_ROWS - batch), (0, 0)))
    perm_dils = tuple(sorted(dil for _, dil in DIL_GROUPS if dil > 1))
    dl_col0 = 3 * SB_W // W_COL
    gate_col0 = dl_col0 + 3 * DIL_W // W_COL
    assert DIL_OUT == W_COL
    group_order = sorted(range(n_groups), key=lambda g: -DIL_GROUPS[g][1])
    assert DIL_GROUPS[group_order[-1]][1] == 1

    for l in range(n_layers):
        mod3 = _ada(c_pad, w_ada[l], b_ada[l][None, :]).reshape(ADA_ROWS, N_MOD, d)

        h_all = _norm1(x, g_norm1[l][None, :], mod3, perm_dils)
        h = h_all[0].reshape(tokens, d)
        h_perm = dict(zip(perm_dils, (hp.reshape(tokens, d) for hp in h_all[1:])))

        sb_tiles = 3 * SB_W // (2 * W_COL)
        qkv_sb = _proj(h, w_in, l, lambda j, r: 2 * j + r, sb_tiles, 2, "proj_sb",
                       query_blocks=[(j, r) for j in range(sb_tiles // 3) for r in range(2)])
        o_sb, gate_pre = _sb_attention_and_gate(qkv_sb.reshape(batch, seq, 3 * SB_W), tri, h_all[0], w_in, l,
                                                gate_col0, 2 * d, sb_tile, sb_tile)
        gate_pre = gate_pre.reshape(tokens, 2 * d)

        partial_groups = []
        for g in group_order:
            dil = DIL_GROUPS[g][1]
            qkv_g = _proj(h_perm.get(dil, h), w_in, l,
                          lambda j, r, g=g: dl_col0 + r * n_groups + g, 1, 3, f"proj_dil_d{dil}",
                          query_blocks=[(0, 0)])
            g_slopes = slopes[g * DIL_HEADS_PER_GROUP:(g + 1) * DIL_HEADS_PER_GROUP]
            if g != group_order[-1]:
                partial_groups.append(_dil_attention(qkv_g, batch, g, g_slopes, n_sub=4))
            else:
                o_dl = _dil_attention(qkv_g, batch, g, g_slopes, n_sub=4, others=tuple(partial_groups))

        x1, h2 = _post(o_sb.reshape(tokens, SB_W), o_dl.reshape(tokens, DIL_OUT), gate_pre,
                       b_gate[l][None, :], x.reshape(tokens, d), mod3, g_norm2[l][None, :],
                       w_proj_sb, w_proj_dil, w_out, l, seq)

        a = _swiglu(h2, w_ffn_gate, w_ffn_up, l)
        x = _down(a, w_ffn_down, l, x1, mod3, g_final[None, :], seq,
                  final_norm=(l == n_layers - 1)).reshape(batch, seq, d)
    return x
```

```python
import functools

import jax
import jax.numpy as jnp
from jax import lax
from jax.experimental import pallas as pl
from jax.experimental.pallas import tpu as pltpu

F32 = jnp.float32
BF16 = jnp.bfloat16

HEAD_DIM = 128
SB_HEADS = 8
DIL_GROUPS = ((128, 1), (512, 4), (2048, 16))
DIL_HEADS_PER_GROUP = 4
DIL_HEADS = DIL_HEADS_PER_GROUP * len(DIL_GROUPS)
BLOCK = 128
SB_W = SB_HEADS * HEAD_DIM
DIL_W = DIL_HEADS * HEAD_DIM
DIL_OUT = DIL_HEADS_PER_GROUP * HEAD_DIM
N_MOD = 6
EPS = 1e-6
LOG2E = 1.4426950408889634
LN2 = 0.6931471805599453
QUERY_SCALE = HEAD_DIM ** -0.5 * LOG2E

LANES = 128
MIB = 1024 * 1024
ADA_ROWS = 8
W_COL = 512
X_RING_SLOTS = 3
W_STAGE_ROWS = 128
W_STAGE_SLOTS = 4
SUB_ROWS = 256
DIL_CHAINS = 8
LSE_LANES = LANES // DIL_HEADS_PER_GROUP
assert DIL_CHAINS % DIL_HEADS_PER_GROUP == 0


V7X_VMEM_MIB = 64
VMEM_STREAMING_MIB = 40
VMEM_RESIDENT_MIB = V7X_VMEM_MIB - 4


def _params(n_axes, vmem_mib):
    return pltpu.CompilerParams(dimension_semantics=("arbitrary",) * n_axes,
                                vmem_limit_bytes=vmem_mib * MIB)


def _dot(a, b):
    return jnp.dot(a, b, preferred_element_type=F32)


def _dot_nt(a, b):
    return lax.dot_general(a, b, (((1,), (1,)), ((), ())), preferred_element_type=F32)


def _sigmoid(x):
    return 1.0 / (1.0 + jnp.exp(-x))


def _rms(x):
    return x * lax.rsqrt(jnp.mean(x * x, axis=-1, keepdims=True) + EPS)


def _row_chunks(rows, size=SUB_ROWS):
    return [slice(r, r + size) for r in range(0, rows, size)]


def _ada_kernel(c_ref, w_ref, b_ref, o_ref):
    c = c_ref[...]
    s = (c * _sigmoid(c)).astype(BF16)
    o_ref[...] = _dot(s, w_ref[...].astype(BF16)) + b_ref[...]


def _ada(c_pad, w, b, tn=1024):
    d, n = w.shape
    return pl.pallas_call(
        _ada_kernel,
        out_shape=jax.ShapeDtypeStruct((ADA_ROWS, n), F32),
        grid=(n // tn,),
        in_specs=[pl.BlockSpec((ADA_ROWS, d), lambda j: (0, 0)),
                  pl.BlockSpec((d, tn), lambda j: (0, j)),
                  pl.BlockSpec((1, tn), lambda j: (0, j))],
        out_specs=pl.BlockSpec((ADA_ROWS, tn), lambda j: (0, j)),
        compiler_params=_params(1, VMEM_STREAMING_MIB),
        name="ada",
    )(c_pad, w, b)


def _norm1_kernel(x_hbm, g_ref, mod_ref, o_ref, *rest, dils):
    n_perm = len(dils)
    perm_refs, (x_ring, x_sem), slabs = rest[:n_perm], rest[n_perm:n_perm + 2], rest[n_perm + 2:]
    ts, d_model = o_ref.shape[1], o_ref.shape[2]
    n_j = pl.num_programs(1)
    n_steps = pl.num_programs(0) * n_j
    step = pl.program_id(0) * n_j + pl.program_id(1)

    def tile_copy(t):
        src = x_hbm.at[t // n_j, pl.ds(pl.multiple_of(lax.rem(t, n_j) * ts, ts), ts), :]
        slot = lax.rem(t, X_RING_SLOTS)
        return pltpu.make_async_copy(src, x_ring.at[slot], x_sem.at[slot])

    @pl.when(step == 0)
    def _():
        for t in range(X_RING_SLOTS - 1):
            tile_copy(t).start()

    @pl.when(step + X_RING_SLOTS - 1 < n_steps)
    def _():
        tile_copy(step + X_RING_SLOTS - 1).start()

    tile_copy(step).wait()
    y = _rms(x_ring[lax.rem(step, X_RING_SLOTS)]) * g_ref[...]
    h = y * (1.0 + mod_ref[0, 1:2, :]) + mod_ref[0, 0:1, :]
    o_ref[0] = h.astype(o_ref.dtype)
    n_slab = d_model // LANES
    for s in range(n_slab):
        slabs[0][s] = h[:, s * LANES:(s + 1) * LANES]
    level = {1: slabs[0]}
    for n, (dil, p_ref) in enumerate(zip(dils, perm_refs)):
        base = max(p for p in level if dil % p == 0)
        step, src = dil // base, level[base]
        keep = slabs[n + 1] if n + 1 < len(slabs) else None
        for r in range(dil):
            r_hi, r_lo = divmod(r, base)
            start = r_lo * (ts // base) + r_hi
            for s in range(n_slab):
                rows = src[s, pl.ds(start, ts // dil, stride=step), :]
                p_ref[0, r, :, s * LANES:(s + 1) * LANES] = rows.astype(p_ref.dtype)
                if keep is not None:
                    keep[s, r * (ts // dil):(r + 1) * (ts // dil), :] = rows
        if keep is not None:
            level[dil] = keep


def _norm1(x, g, mod3, dils, ts=512):
    b, s, d = x.shape
    out_shape = [jax.ShapeDtypeStruct((b, s, d), BF16)]
    out_specs = [pl.BlockSpec((1, ts, d), lambda i, j: (i, j, 0))]
    for dil in dils:
        out_shape.append(jax.ShapeDtypeStruct((b, dil, s // dil, d), BF16))
        out_specs.append(pl.BlockSpec((1, dil, ts // dil, d), lambda i, j: (i, 0, j, 0)))
    slab = pltpu.VMEM((d // LANES, ts, LANES), F32)
    assert b * (s // ts) >= X_RING_SLOTS - 1
    return pl.pallas_call(
        functools.partial(_norm1_kernel, dils=dils),
        out_shape=out_shape,
        grid=(b, s // ts),
        in_specs=[pl.BlockSpec(memory_space=pl.ANY),
                  pl.BlockSpec((1, d), lambda i, j: (0, 0)),
                  pl.BlockSpec((1, N_MOD, d), lambda i, j: (i, 0, 0))],
        out_specs=out_specs,
        scratch_shapes=[pltpu.VMEM((X_RING_SLOTS, ts, d), F32), pltpu.SemaphoreType.DMA((X_RING_SLOTS,))]
                       + [slab] * max(1, len(dils)),
        compiler_params=_params(2, VMEM_STREAMING_MIB),
        name="norm1",
    )(x, g, mod3)


def _proj_kernel(*refs, n_rhs, query_blocks):
    h_ref, w_refs, o_ref, wb_ref = refs[0], refs[1:1 + n_rhs], refs[1 + n_rhs], refs[2 + n_rhs]

    @pl.when(pl.program_id(1) == 0)
    def _():
        for r, w_ref in enumerate(w_refs):
            wb_ref[:, r * W_COL:(r + 1) * W_COL] = w_ref[...].astype(BF16)

    for c in _row_chunks(h_ref.shape[0], 4 * SUB_ROWS):
        for r in range(n_rhs):
            cols = slice(r * W_COL, (r + 1) * W_COL)
            acc = _dot(h_ref[c, :], wb_ref[:, cols])
            tiles = [j for j, rr in query_blocks if rr == r]
            if tiles:
                is_query = functools.reduce(jnp.logical_or, [pl.program_id(0) == j for j in tiles])
                acc = acc * jnp.where(is_query, QUERY_SCALE, 1.0)
            o_ref[c, cols] = acc.astype(o_ref.dtype)


def _proj(h, w_in, layer, col_fn, n_tiles, n_rhs, name, query_blocks, tm=2048):
    m, k = h.shape
    tn = n_rhs * W_COL
    mode = pl.Buffered(1) if n_tiles == 1 else None

    def w_spec(r):
        return pl.BlockSpec((None, k, W_COL), lambda j, i: (layer, 0, col_fn(j, r)), pipeline_mode=mode)

    return pl.pallas_call(
        functools.partial(_proj_kernel, n_rhs=n_rhs, query_blocks=tuple(query_blocks)),
        out_shape=jax.ShapeDtypeStruct((m, n_tiles * tn), BF16),
        grid=(n_tiles, m // tm),
        in_specs=[pl.BlockSpec((tm, k), lambda j, i: (i, 0))] + [w_spec(r) for r in range(n_rhs)],
        out_specs=pl.BlockSpec((tm, tn), lambda j, i: (i, j)),
        scratch_shapes=[pltpu.VMEM((k, tn), BF16)],
        compiler_params=_params(2, VMEM_RESIDENT_MIB),
        name=name,
    )(h, *([w_in] * n_rhs))


def _sb_kernel(*refs, n_rhs, tq, bk):
    q_ref, k_ref, v_ref, tri_ref, h_ref = refs[:5]
    w_refs, (o_ref, gate_ref, wb_ref) = refs[5:5 + n_rhs], refs[5 + n_rhs:]

    @pl.when(pl.program_id(1) == 0)
    def _():
        for r, w_ref in enumerate(w_refs):
            wb_ref[:, r * W_COL:(r + 1) * W_COL] = w_ref[...].astype(BF16)

    seq = q_ref.shape[1]
    heads = [slice(h * HEAD_DIM, (h + 1) * HEAD_DIM) for h in range(q_ref.shape[2] // HEAD_DIM)]
    tri = tri_ref[...]
    row = lax.broadcasted_iota(jnp.int32, (tq, bk), 0)
    col = lax.broadcasted_iota(jnp.int32, (tq, bk), 1)
    gate_pieces = [(slice(r0, r0 + SUB_ROWS), slice(c0, c0 + SUB_ROWS))
                   for r0 in range(0, seq, SUB_ROWS) for c0 in range(0, n_rhs * W_COL, SUB_ROWS)]
    n_blocks = sum((i + 1) * tq // bk for i in range(seq // tq))
    pieces_per_block = -(-len(gate_pieces) // n_blocks)

    def emit_gate_pieces(count):
        for _ in range(min(count, len(gate_pieces))):
            g_rows, g_cols = gate_pieces.pop(0)
            gate_ref[0, g_rows, g_cols] = _dot(h_ref[0, g_rows, :], wb_ref[:, g_cols]).astype(gate_ref.dtype)

    for i in range(seq // tq):
        n_keys = (i + 1) * tq
        rows = slice(i * tq, (i + 1) * tq)
        z_all = [_dot_nt(q_ref[0, rows, hd], k_ref[0, :n_keys, hd]) for hd in heads]
        run = [jnp.zeros((tq, 1), F32) for _ in heads]
        weights = [[None] * (n_keys // bk) for _ in heads]
        for jb in reversed(range(n_keys // bk)):
            emit_gate_pieces(pieces_per_block)
            on_diagonal = (jb + 1) * bk > i * tq
            keep = col + (jb * bk - i * tq) < row
            z = [za[:, jb * bk:(jb + 1) * bk] for za in z_all]
            sp = [jnp.maximum(zh, 0.0) + jnp.log(1.0 + jnp.exp2(-jnp.abs(zh))) * LOG2E for zh in z]
            own = [zh - sh for zh, sh in zip(z, sp)]
            if on_diagonal:
                sp = [jnp.where(keep, sh, 0.0) for sh in sp]
            sp16 = [sh.astype(BF16) for sh in sp]
            later = [_dot(sh, tri) for sh in sp16]
            a = [jnp.exp2(oh - (rh + lh)) for oh, rh, lh in zip(own, run, later)]
            if on_diagonal:
                a = [jnp.where(keep, ah, 0.0) for ah in a]
            for h, ah in enumerate(a):
                weights[h][jb] = ah.astype(BF16)
            run = [rh + lh[:, 0:1] + sh[:, 0:1].astype(F32) for rh, lh, sh in zip(run, later, sp16)]
        for h, hd in enumerate(heads):
            a_all = weights[h][0] if len(weights[h]) == 1 else jnp.concatenate(weights[h], axis=1)
            o_ref[0, rows, hd] = _dot(a_all, v_ref[0, :n_keys, hd]).astype(o_ref.dtype)
    emit_gate_pieces(len(gate_pieces))


def _sb_attention_and_gate(qkv, tri, h, w_in, layer, gate_col0, gate_width, tq, bk, heads_per_step=2):
    b, s, k = h.shape
    n_steps = SB_HEADS // heads_per_step
    n_rhs = gate_width // (n_steps * W_COL)
    assert n_rhs * n_steps * W_COL == gate_width
    tn = n_rhs * W_COL
    blk = (1, s, heads_per_step * HEAD_DIM)

    def w_spec(r):
        return pl.BlockSpec((None, k, W_COL), lambda p, i: (layer, 0, gate_col0 + p * n_rhs + r),
                            pipeline_mode=pl.Buffered(1))

    return pl.pallas_call(
        functools.partial(_sb_kernel, n_rhs=n_rhs, tq=tq, bk=bk),
        out_shape=(jax.ShapeDtypeStruct((b, s, SB_W), BF16), jax.ShapeDtypeStruct((b, s, gate_width), BF16)),
        grid=(n_steps, b),
        in_specs=[pl.BlockSpec(blk, lambda p, i: (i, 0, p)),
                  pl.BlockSpec(blk, lambda p, i: (i, 0, n_steps + p)),
                  pl.BlockSpec(blk, lambda p, i: (i, 0, 2 * n_steps + p)),
                  pl.BlockSpec((bk, bk), lambda p, i: (0, 0)),
                  pl.BlockSpec((1, s, k), lambda p, i: (i, 0, 0))] + [w_spec(r) for r in range(n_rhs)],
        out_specs=(pl.BlockSpec(blk, lambda p, i: (i, 0, p)), pl.BlockSpec((1, s, tn), lambda p, i: (i, 0, p))),
        scratch_shapes=[pltpu.VMEM((k, tn), BF16)],
        compiler_params=_params(2, VMEM_RESIDENT_MIB),
        name="sb_attn_gate",
    )(qkv, qkv, qkv, tri, h, *([w_in] * n_rhs))


def _dil_kernel(*refs, n_sub, has_prev, dilation, slopes, n_other):
    refs = list(refs)
    q_ref, kc_ref = refs.pop(0), refs.pop(0)
    kp_ref = refs.pop(0) if has_prev else None
    vc_ref = refs.pop(0)
    vp_ref = refs.pop(0) if has_prev else None
    others = [(refs.pop(0), refs.pop(0)) for _ in range(n_other)]
    o_ref = refs.pop(0)
    lse_ref = None if n_other else refs.pop(0)
    o_slab, lse_slab = refs

    n_keys = 2 * BLOCK if has_prev else BLOCK
    a_idx = lax.broadcasted_iota(jnp.int32, (BLOCK, n_keys), 0)
    b_idx = lax.broadcasted_iota(jnp.int32, (BLOCK, n_keys), 1)
    rel = a_idx + (n_keys - BLOCK) - b_idx
    in_window = (rel >= 0) & (rel <= BLOCK)
    dist = (dilation * rel).astype(F32)
    bias = [(slope * LOG2E) * dist for slope in slopes]
    lane = lax.broadcasted_iota(jnp.int32, (BLOCK, LANES), 1)
    step = jnp.zeros((BLOCK, n_keys), jnp.int32) + pl.program_id(1)
    valid_first = in_window & ((b_idx >= BLOCK) | (step > 0)) if has_prev else in_window

    def operands(r, j, h):
        rows = slice(j * BLOCK, (j + 1) * BLOCK)
        cols = slice(h * HEAD_DIM, (h + 1) * HEAD_DIM)
        q, kk, vv = q_ref[0, r, rows, cols], kc_ref[0, r, rows, cols], vc_ref[0, r, rows, cols]
        if has_prev:
            if j == 0:
                k_prev, v_prev = kp_ref[0, r, :, cols], vp_ref[0, r, :, cols]
            else:
                prev_rows = slice((j - 1) * BLOCK, j * BLOCK)
                k_prev, v_prev = kc_ref[0, r, prev_rows, cols], vc_ref[0, r, prev_rows, cols]
            kk = jnp.concatenate([k_prev, kk], axis=0)
            vv = jnp.concatenate([v_prev, vv], axis=0)
        return q, kk, vv

    blocks = [(r, j, h) for r in range(dilation) for j in range(n_sub) for h in range(DIL_HEADS_PER_GROUP)]
    for c0 in range(0, len(blocks), DIL_CHAINS):
        chunk = blocks[c0:c0 + DIL_CHAINS]
        ops = [operands(*blk) for blk in chunk]
        s = [jnp.where(valid_first if j == 0 else in_window, _dot_nt(q, kk) - bias[h], -jnp.inf)
             for (r, j, h), (q, kk, vv) in zip(chunk, ops)]
        m = [jnp.max(si, axis=-1, keepdims=True) for si in s]
        p = [jnp.exp2(si - mi) for si, mi in zip(s, m)]
        l = [jnp.sum(pi, axis=-1, keepdims=True) for pi in p]
        o = [_dot(pi.astype(BF16), vv) / li for pi, li, (q, kk, vv) in zip(p, l, ops)]
        lse = [mi * LN2 + jnp.log(li) for mi, li in zip(m, l)]
        for (r, j, h), oi in zip(chunk, o):
            o_slab[h, pl.ds(j * BLOCK * dilation + r, BLOCK, stride=dilation), :] = oi
        for g0 in range(0, len(chunk), DIL_HEADS_PER_GROUP):
            r, j, _ = chunk[g0]
            packed = lse[g0 + DIL_HEADS_PER_GROUP - 1]
            for h in reversed(range(DIL_HEADS_PER_GROUP - 1)):
                packed = jnp.where(lane < (h + 1) * LSE_LANES, lse[g0 + h], packed)
            lse_slab[pl.ds(j * BLOCK * dilation + r, BLOCK, stride=dilation), :] = packed

    if not others:
        lse_ref[0] = lse_slab[...]
        for h in range(DIL_HEADS_PER_GROUP):
            o_ref[0, :, h * HEAD_DIM:(h + 1) * HEAD_DIM] = o_slab[h].astype(o_ref.dtype)
        return
    lses = [lse_slab[...]] + [lg_ref[0] for _, lg_ref in others]
    top = functools.reduce(jnp.maximum, lses)
    e = [jnp.exp(lg - top) for lg in lses]
    inv = 1.0 / functools.reduce(lambda a, b: a + b, e)
    weights = [eg * inv for eg in e]
    for h in range(DIL_HEADS_PER_GROUP):
        cols = slice(h * HEAD_DIM, (h + 1) * HEAD_DIM)
        lse_col = slice(h * LSE_LANES, h * LSE_LANES + 1)
        outs = [o_slab[h]] + [og_ref[0, :, cols].astype(F32) for og_ref, _ in others]
        mixed = functools.reduce(lambda a, b: a + b, [wg[:, lse_col] * og for wg, og in zip(weights, outs)])
        o_ref[0, :, cols] = mixed.astype(o_ref.dtype)


def _dil_attention(qkv, batch, group, slopes, n_sub, others=()):
    window, dilation = DIL_GROUPS[group]
    s = qkv.shape[0] // batch
    sub_len = s // dilation
    n_blk = sub_len // BLOCK
    n_sub = min(n_sub, n_blk)
    has_prev = n_blk > 1
    assert window // dilation == BLOCK and sub_len % BLOCK == 0 and n_blk % n_sub == 0
    view = qkv.reshape(batch, dilation, sub_len, 3 * DIL_OUT)
    rows = n_sub * BLOCK

    def cur(col):
        return pl.BlockSpec((1, dilation, rows, DIL_OUT), lambda i, n: (i, 0, n, col))

    def prev(col):
        return pl.BlockSpec((1, dilation, BLOCK, DIL_OUT),
                            lambda i, n: (i, 0, jnp.maximum(n * n_sub - 1, 0), col))

    if has_prev:
        in_specs = [cur(0), cur(1), prev(1), cur(2), prev(2)]
    else:
        in_specs = [cur(0), cur(1), cur(2)]
    args = [view] * len(in_specs)
    tile = pl.BlockSpec((1, dilation * rows, DIL_OUT), lambda i, n: (i, n, 0))
    lse_tile = pl.BlockSpec((1, dilation * rows, LANES), lambda i, n: (i, n, 0))
    for o_g, lse_g in others:
        in_specs += [tile, lse_tile]
        args += [o_g, lse_g]
    o_shape = jax.ShapeDtypeStruct((batch, s, DIL_OUT), BF16)
    lse_shape = jax.ShapeDtypeStruct((batch, s, LANES), F32)
    return pl.pallas_call(
        functools.partial(_dil_kernel, n_sub=n_sub, has_prev=has_prev, dilation=dilation,
                          slopes=slopes, n_other=len(others)),
        out_shape=o_shape if others else (o_shape, lse_shape),
        grid=(batch, n_blk // n_sub),
        in_specs=in_specs,
        out_specs=tile if others else (tile, lse_tile),
        scratch_shapes=[pltpu.VMEM((DIL_HEADS_PER_GROUP, dilation * rows, HEAD_DIM), F32),
                        pltpu.VMEM((dilation * rows, LANES), F32)],
        compiler_params=_params(2, VMEM_RESIDENT_MIB),
        name=f"dil_attn_d{dilation}",
    )(*args)


def _post_kernel(osb_ref, odl_ref, gp_ref, bg_ref, x_ref, mod_ref, g2_ref, wsb_hbm, wdl_hbm, wout_hbm,
                 x1_ref, h2_ref, wsb_ref, wdl_ref, wout_ref, stage_ref, sem_ref, *, layer):
    @pl.when(pl.program_id(0) == 0)
    def _():
        for w_hbm, wb_ref in ((wsb_hbm, wsb_ref), (wdl_hbm, wdl_ref), (wout_hbm, wout_ref)):
            _load_weight_bf16(w_hbm, layer, wb_ref, stage_ref, sem_ref)

    d = x_ref.shape[-1]
    chunks = _row_chunks(x_ref.shape[0])
    y_sb = [_dot(osb_ref[c, :], wsb_ref[...]) for c in chunks]
    y_dl = [_dot(odl_ref[c, :], wdl_ref[...]) for c in chunks]
    gates = [_sigmoid(gp_ref[c, :].astype(F32) + bg_ref[...]) for c in chunks]
    mixed = [(g[:, :d] * a + g[:, d:] * b).astype(BF16) for g, a, b in zip(gates, y_sb, y_dl)]
    t = [_dot(mx, wout_ref[...]) for mx in mixed]
    for c, tc in zip(chunks, t):
        x1 = x_ref[c, :] + mod_ref[0, 2:3, :] * tc
        x1_ref[c, :] = x1
        h2 = _rms(x1) * g2_ref[...] * (1.0 + mod_ref[0, 4:5, :]) + mod_ref[0, 3:4, :]
        h2_ref[c, :] = h2.astype(h2_ref.dtype)


def _post(o_sb, o_dl, gate_pre, b_gate, x2d, mod3, g2, w_sb, w_dl, w_out, layer, seq, tm=512):
    m, d = x2d.shape
    weights = (w_sb, w_dl, w_out)
    assert all(w.shape[1] % W_STAGE_ROWS == 0 and w.shape[2] == d for w in weights)

    def row(width):
        return pl.BlockSpec((tm, width), lambda i: (i, 0))

    def whole(arr):
        return pl.BlockSpec(arr.shape, lambda i: (0,) * arr.ndim, pipeline_mode=pl.Buffered(1))

    return pl.pallas_call(
        functools.partial(_post_kernel, layer=layer),
        out_shape=(jax.ShapeDtypeStruct((m, d), F32), jax.ShapeDtypeStruct((m, d), BF16)),
        grid=(m // tm,),
        in_specs=[row(SB_W), row(DIL_OUT), row(2 * d), whole(b_gate), row(d),
                  pl.BlockSpec((1, N_MOD, d), lambda i: ((i * tm) // seq, 0, 0)),
                  whole(g2)] + [pl.BlockSpec(memory_space=pl.ANY)] * len(weights),
        out_specs=(row(d), row(d)),
        scratch_shapes=[pltpu.VMEM(w.shape[1:], BF16) for w in weights]
                       + [pltpu.VMEM((W_STAGE_SLOTS, W_STAGE_ROWS, d), F32),
                          pltpu.SemaphoreType.DMA((W_STAGE_SLOTS,))],
        compiler_params=_params(1, VMEM_RESIDENT_MIB),
        name="post",
    )(o_sb, o_dl, gate_pre, b_gate, x2d, mod3, g2, *weights)


def _swiglu_kernel(h_ref, wg_ref, wu_ref, o_ref, wgb_ref, wub_ref):
    @pl.when(pl.program_id(1) == 0)
    def _():
        wgb_ref[...] = wg_ref[...].astype(BF16)
        wub_ref[...] = wu_ref[...].astype(BF16)

    for c in _row_chunks(h_ref.shape[0], 4 * SUB_ROWS):
        g = _dot(h_ref[c, :], wgb_ref[...])
        u = _dot(h_ref[c, :], wub_ref[...])
        o_ref[c, :] = (g * _sigmoid(g) * u).astype(o_ref.dtype)


def _swiglu(h, wg, wu, layer, tm=2048, tn=512):
    m, k = h.shape
    n = wg.shape[-1]
    w_spec = pl.BlockSpec((None, k, tn), lambda j, i: (layer, 0, j))
    return pl.pallas_call(
        _swiglu_kernel,
        out_shape=jax.ShapeDtypeStruct((m, n), BF16),
        grid=(n // tn, m // tm),
        in_specs=[pl.BlockSpec((tm, k), lambda j, i: (i, 0)), w_spec, w_spec],
        out_specs=pl.BlockSpec((tm, tn), lambda j, i: (i, j)),
        scratch_shapes=[pltpu.VMEM((k, tn), BF16), pltpu.VMEM((k, tn), BF16)],
        compiler_params=_params(2, VMEM_RESIDENT_MIB),
        name="swiglu",
    )(h, wg, wu)


def _load_weight_bf16(w_hbm, layer, dst_ref, stage_ref, sem_ref):
    n_chunks = dst_ref.shape[0] // W_STAGE_ROWS
    n_slots = stage_ref.shape[0]
    ahead = min(n_slots - 1, n_chunks)

    def chunk_copy(c, slot):
        src = w_hbm.at[layer, pl.ds(c * W_STAGE_ROWS, W_STAGE_ROWS), :]
        return pltpu.make_async_copy(src, stage_ref.at[slot], sem_ref.at[slot])

    for c in range(ahead):
        chunk_copy(c, c).start()

    def body(c, carry):
        slot = lax.rem(c, n_slots)

        @pl.when(c + ahead < n_chunks)
        def _():
            chunk_copy(c + ahead, lax.rem(c + ahead, n_slots)).start()

        chunk_copy(c, slot).wait()
        rows = pl.ds(pl.multiple_of(c * W_STAGE_ROWS, W_STAGE_ROWS), W_STAGE_ROWS)
        dst_ref[rows, :] = stage_ref[slot].astype(dst_ref.dtype)
        return carry

    lax.fori_loop(0, n_chunks, body, 0)


def _down_kernel(a_ref, w_hbm, x1_ref, mod_ref, gf_ref, o_ref, wb_ref, stage_ref, sem_ref, *, layer, final_norm):
    @pl.when(pl.program_id(0) == 0)
    def _():
        _load_weight_bf16(w_hbm, layer, wb_ref, stage_ref, sem_ref)

    for c in _row_chunks(a_ref.shape[0]):
        x2 = x1_ref[c, :] + mod_ref[0, 5:6, :] * _dot(a_ref[c, :], wb_ref[...])
        o_ref[c, :] = _rms(x2) * gf_ref[...] if final_norm else x2


def _down(a, w, layer, x1, mod3, g_final, seq, final_norm, tm=512):
    m, f = a.shape
    d = w.shape[-1]
    assert f % W_STAGE_ROWS == 0
    return pl.pallas_call(
        functools.partial(_down_kernel, layer=layer, final_norm=final_norm),
        out_shape=jax.ShapeDtypeStruct((m, d), F32),
        grid=(m // tm,),
        in_specs=[pl.BlockSpec((tm, f), lambda i: (i, 0)),
                  pl.BlockSpec(memory_space=pl.ANY),
                  pl.BlockSpec((tm, d), lambda i: (i, 0)),
                  pl.BlockSpec((1, N_MOD, d), lambda i: ((i * tm) // seq, 0, 0)),
                  pl.BlockSpec((1, d), lambda i: (0, 0))],
        out_specs=pl.BlockSpec((tm, d), lambda i: (i, 0)),
        scratch_shapes=[pltpu.VMEM((f, d), BF16), pltpu.VMEM((W_STAGE_SLOTS, W_STAGE_ROWS, d), F32),
                        pltpu.SemaphoreType.DMA((W_STAGE_SLOTS,))],
        compiler_params=_params(1, VMEM_RESIDENT_MIB),
        name="down",
    )(a, w, x1, mod3, g_final)


def _alibi_slopes():
    return tuple(2.0 ** (-8.0 * (i + 1.0) / DIL_HEADS) for i in range(DIL_HEADS))


def kernel(x, c, w_ada, b_ada, g_norm1, g_norm2, g_final, w_in, b_gate, w_proj_sb, w_proj_dil,
           w_out, w_ffn_gate, w_ffn_up, w_ffn_down):
    batch, seq, d = x.shape
    tokens = batch * seq
    n_layers = w_ada.shape[0]
    slopes = _alibi_slopes()
    n_groups = len(DIL_GROUPS)
    sb_tile = 256
    idx = jnp.arange(sb_tile)
    tri = (idx[:, None] > idx[None, :]).astype(BF16)
    c_pad = jnp.pad(c, ((0, ADA_ROWS - batch), (0, 0)))
    perm_dils = tuple(sorted(dil for _, dil in DIL_GROUPS if dil > 1))
    dl_col0 = 3 * SB_W // W_COL
    gate_col0 = dl_col0 + 3 * DIL_W // W_COL
    assert DIL_OUT == W_COL
    group_order = sorted(range(n_groups), key=lambda g: -DIL_GROUPS[g][1])
    assert DIL_GROUPS[group_order[-1]][1] == 1

    for l in range(n_layers):
        mod3 = _ada(c_pad, w_ada[l], b_ada[l][None, :]).reshape(ADA_ROWS, N_MOD, d)

        h_all = _norm1(x, g_norm1[l][None, :], mod3, perm_dils)
        h = h_all[0].reshape(tokens, d)
        h_perm = dict(zip(perm_dils, (hp.reshape(tokens, d) for hp in h_all[1:])))

        sb_tiles = 3 * SB_W // (2 * W_COL)
        qkv_sb = _proj(h, w_in, l, lambda j, r: 2 * j + r, sb_tiles, 2, "proj_sb",
                       query_blocks=[(j, r) for j in range(sb_tiles // 3) for r in range(2)])
        o_sb, gate_pre = _sb_attention_and_gate(qkv_sb.reshape(batch, seq, 3 * SB_W), tri, h_all[0], w_in, l,
                                                gate_col0, 2 * d, sb_tile, sb_tile)
        gate_pre = gate_pre.reshape(tokens, 2 * d)

        partial_groups = []
        for g in group_order:
            dil = DIL_GROUPS[g][1]
            qkv_g = _proj(h_perm.get(dil, h), w_in, l,
                          lambda j, r, g=g: dl_col0 + r * n_groups + g, 1, 3, f"proj_dil_d{dil}",
                          query_blocks=[(0, 0)])
            g_slopes = slopes[g * DIL_HEADS_PER_GROUP:(g + 1) * DIL_HEADS_PER_GROUP]
            if g != group_order[-1]:
                partial_groups.append(_dil_attention(qkv_g, batch, g, g_slopes, n_sub=4))
            else:
                o_dl = _dil_attention(qkv_g, batch, g, g_slopes, n_sub=4, others=tuple(partial_groups))

        x1, h2 = _post(o_sb.reshape(tokens, SB_W), o_dl.reshape(tokens, DIL_OUT), gate_pre,
                       b_gate[l][None, :], x.reshape(tokens, d), mod3, g_norm2[l][None, :],
                       w_proj_sb, w_proj_dil, w_out, l, seq)

        a = _swiglu(h2, w_ffn_gate, w_ffn_up, l)
        x = _down(a, w_ffn_down, l, x1, mod3, g_final[None, :], seq,
                  final_norm=(l == n_layers - 1)).reshape(batch, seq, d)
    return x
```

```python
import functools

import jax
import jax.numpy as jnp
from jax import lax
from jax.experimental import pallas as pl
from jax.experimental.pallas import tpu as pltpu

F32 = jnp.float32
BF16 = jnp.bfloat16

HEAD_DIM = 128
SB_HEADS = 8
DIL_GROUPS = ((128, 1), (512, 4), (2048, 16))
DIL_HEADS_PER_GROUP = 4
DIL_HEADS = DIL_HEADS_PER_GROUP * len(DIL_GROUPS)
BLOCK = 128
SB_W = SB_HEADS * HEAD_DIM
DIL_W = DIL_HEADS * HEAD_DIM
DIL_OUT = DIL_HEADS_PER_GROUP * HEAD_DIM
N_MOD = 6
EPS = 1e-6
LOG2E = 1.4426950408889634
LN2 = 0.6931471805599453
QUERY_SCALE = HEAD_DIM ** -0.5 * LOG2E

LANES = 128
MIB = 1024 * 1024
ADA_ROWS = 8
W_COL = 512
X_RING_SLOTS = 3
W_STAGE_ROWS = 128
W_STAGE_SLOTS = 4
SUB_ROWS = 256
DIL_CHAINS = 8
LSE_LANES = LANES // DIL_HEADS_PER_GROUP
assert DIL_CHAINS % DIL_HEADS_PER_GROUP == 0


V7X_VMEM_MIB = 64
VMEM_STREAMING_MIB = 40
VMEM_RESIDENT_MIB = V7X_VMEM_MIB - 4


def _params(n_axes, vmem_mib):
    return pltpu.CompilerParams(dimension_semantics=("arbitrary",) * n_axes,
                                vmem_limit_bytes=vmem_mib * MIB)


def _dot(a, b):
    return jnp.dot(a, b, preferred_element_type=F32)


def _dot_nt(a, b):
    return lax.dot_general(a, b, (((1,), (1,)), ((), ())), preferred_element_type=F32)


def _sigmoid(x):
    return 1.0 / (1.0 + jnp.exp(-x))


def _rms(x):
    return x * lax.rsqrt(jnp.mean(x * x, axis=-1, keepdims=True) + EPS)


def _row_chunks(rows, size=SUB_ROWS):
    return [slice(r, r + size) for r in range(0, rows, size)]


def _ada_kernel(c_ref, w_ref, b_ref, o_ref):
    c = c_ref[...]
    s = (c * _sigmoid(c)).astype(BF16)
    o_ref[...] = _dot(s, w_ref[...].astype(BF16)) + b_ref[...]


def _ada(c_pad, w, b, tn=1024):
    d, n = w.shape
    return pl.pallas_call(
        _ada_kernel,
        out_shape=jax.ShapeDtypeStruct((ADA_ROWS, n), F32),
        grid=(n // tn,),
        in_specs=[pl.BlockSpec((ADA_ROWS, d), lambda j: (0, 0)),
                  pl.BlockSpec((d, tn), lambda j: (0, j)),
                  pl.BlockSpec((1, tn), lambda j: (0, j))],
        out_specs=pl.BlockSpec((ADA_ROWS, tn), lambda j: (0, j)),
        compiler_params=_params(1, VMEM_STREAMING_MIB),
        name="ada",
    )(c_pad, w, b)


def _norm1_kernel(x_hbm, g_ref, mod_ref, o_ref, *rest, dils):
    n_perm = len(dils)
    perm_refs, (x_ring, x_sem), slabs = rest[:n_perm], rest[n_perm:n_perm + 2], rest[n_perm + 2:]
    ts, d_model = o_ref.shape[1], o_ref.shape[2]
    n_j = pl.num_programs(1)
    n_steps = pl.num_programs(0) * n_j
    step = pl.program_id(0) * n_j + pl.program_id(1)

    def tile_copy(t):
        src = x_hbm.at[t // n_j, pl.ds(pl.multiple_of(lax.rem(t, n_j) * ts, ts), ts), :]
        slot = lax.rem(t, X_RING_SLOTS)
        return pltpu.make_async_copy(src, x_ring.at[slot], x_sem.at[slot])

    @pl.when(step == 0)
    def _():
        for t in range(X_RING_SLOTS - 1):
            tile_copy(t).start()

    @pl.when(step + X_RING_SLOTS - 1 < n_steps)
    def _():
        tile_copy(step + X_RING_SLOTS - 1).start()

    tile_copy(step).wait()
    y = _rms(x_ring[lax.rem(step, X_RING_SLOTS)]) * g_ref[...]
    h = y * (1.0 + mod_ref[0, 1:2, :]) + mod_ref[0, 0:1, :]
    o_ref[0] = h.astype(o_ref.dtype)
    n_slab = d_model // LANES
    for s in range(n_slab):
        slabs[0][s] = h[:, s * LANES:(s + 1) * LANES]
    level = {1: slabs[0]}
    for n, (dil, p_ref) in enumerate(zip(dils, perm_refs)):
        base = max(p for p in level if dil % p == 0)
        step, src = dil // base, level[base]
        keep = slabs[n + 1] if n + 1 < len(slabs) else None
        for r in range(dil):
            r_hi, r_lo = divmod(r, base)
            start = r_lo * (ts // base) + r_hi
            for s in range(n_slab):
                rows = src[s, pl.ds(start, ts // dil, stride=step), :]
                p_ref[0, r, :, s * LANES:(s + 1) * LANES] = rows.astype(p_ref.dtype)
                if keep is not None:
                    keep[s, r * (ts // dil):(r + 1) * (ts // dil), :] = rows
        if keep is not None:
            level[dil] = keep


def _norm1(x, g, mod3, dils, ts=512):
    b, s, d = x.shape
    out_shape = [jax.ShapeDtypeStruct((b, s, d), BF16)]
    out_specs = [pl.BlockSpec((1, ts, d), lambda i, j: (i, j, 0))]
    for dil in dils:
        out_shape.append(jax.ShapeDtypeStruct((b, dil, s // dil, d), BF16))
        out_specs.append(pl.BlockSpec((1, dil, ts // dil, d), lambda i, j: (i, 0, j, 0)))
    slab = pltpu.VMEM((d // LANES, ts, LANES), F32)
    assert b * (s // ts) >= X_RING_SLOTS - 1
    return pl.pallas_call(
        functools.partial(_norm1_kernel, dils=dils),
        out_shape=out_shape,
        grid=(b, s // ts),
        in_specs=[pl.BlockSpec(memory_space=pl.ANY),
                  pl.BlockSpec((1, d), lambda i, j: (0, 0)),
                  pl.BlockSpec((1, N_MOD, d), lambda i, j: (i, 0, 0))],
        out_specs=out_specs,
        scratch_shapes=[pltpu.VMEM((X_RING_SLOTS, ts, d), F32), pltpu.SemaphoreType.DMA((X_RING_SLOTS,))]
                       + [slab] * max(1, len(dils)),
        compiler_params=_params(2, VMEM_STREAMING_MIB),
        name="norm1",
    )(x, g, mod3)


def _proj_kernel(*refs, n_rhs, query_blocks):
    h_ref, w_refs, o_ref, wb_ref = refs[0], refs[1:1 + n_rhs], refs[1 + n_rhs], refs[2 + n_rhs]

    @pl.when(pl.program_id(1) == 0)
    def _():
        for r, w_ref in enumerate(w_refs):
            wb_ref[:, r * W_COL:(r + 1) * W_COL] = w_ref[...].astype(BF16)

    for c in _row_chunks(h_ref.shape[0], 4 * SUB_ROWS):
        for r in range(n_rhs):
            cols = slice(r * W_COL, (r + 1) * W_COL)
            acc = _dot(h_ref[c, :], wb_ref[:, cols])
            tiles = [j for j, rr in query_blocks if rr == r]
            if tiles:
                is_query = functools.reduce(jnp.logical_or, [pl.program_id(0) == j for j in tiles])
                acc = acc * jnp.where(is_query, QUERY_SCALE, 1.0)
            o_ref[c, cols] = acc.astype(o_ref.dtype)


def _proj(h, w_in, layer, col_fn, n_tiles, n_rhs, name, query_blocks, tm=2048):
    m, k = h.shape
    tn = n_rhs * W_COL
    mode = pl.Buffered(1) if n_tiles == 1 else None

    def w_spec(r):
        return pl.BlockSpec((None, k, W_COL), lambda j, i: (layer, 0, col_fn(j, r)), pipeline_mode=mode)

    return pl.pallas_call(
        functools.partial(_proj_kernel, n_rhs=n_rhs, query_blocks=tuple(query_blocks)),
        out_shape=jax.ShapeDtypeStruct((m, n_tiles * tn), BF16),
        grid=(n_tiles, m // tm),
        in_specs=[pl.BlockSpec((tm, k), lambda j, i: (i, 0))] + [w_spec(r) for r in range(n_rhs)],
        out_specs=pl.BlockSpec((tm, tn), lambda j, i: (i, j)),
        scratch_shapes=[pltpu.VMEM((k, tn), BF16)],
        compiler_params=_params(2, VMEM_RESIDENT_MIB),
        name=name,
    )(h, *([w_in] * n_rhs))


def _sb_kernel(*refs, n_rhs, tq, bk):
    q_ref, k_ref, v_ref, tri_ref, h_ref = refs[:5]
    w_refs, (o_ref, gate_ref, wb_ref) = refs[5:5 + n_rhs], refs[5 + n_rhs:]

    @pl.when(pl.program_id(1) == 0)
    def _():
        for r, w_ref in enumerate(w_refs):
            wb_ref[:, r * W_COL:(r + 1) * W_COL] = w_ref[...].astype(BF16)

    seq = q_ref.shape[1]
    heads = [slice(h * HEAD_DIM, (h + 1) * HEAD_DIM) for h in range(q_ref.shape[2] // HEAD_DIM)]
    tri = tri_ref[...]
    row = lax.broadcasted_iota(jnp.int32, (tq, bk), 0)
    col = lax.broadcasted_iota(jnp.int32, (tq, bk), 1)
    gate_pieces = [(slice(r0, r0 + SUB_ROWS), slice(c0, c0 + SUB_ROWS))
                   for r0 in range(0, seq, SUB_ROWS) for c0 in range(0, n_rhs * W_COL, SUB_ROWS)]
    n_blocks = sum((i + 1) * tq // bk for i in range(seq // tq))
    pieces_per_block = -(-len(gate_pieces) // n_blocks)

    def emit_gate_pieces(count):
        for _ in range(min(count, len(gate_pieces))):
            g_rows, g_cols = gate_pieces.pop(0)
            gate_ref[0, g_rows, g_cols] = _dot(h_ref[0, g_rows, :], wb_ref[:, g_cols]).astype(gate_ref.dtype)

    for i in range(seq // tq):
        n_keys = (i + 1) * tq
        rows = slice(i * tq, (i + 1) * tq)
        z_all = [_dot_nt(q_ref[0, rows, hd], k_ref[0, :n_keys, hd]) for hd in heads]
        run = [jnp.zeros((tq, 1), F32) for _ in heads]
        weights = [[None] * (n_keys // bk) for _ in heads]
        for jb in reversed(range(n_keys // bk)):
            emit_gate_pieces(pieces_per_block)
            on_diagonal = (jb + 1) * bk > i * tq
            keep = col + (jb * bk - i * tq) < row
            z = [za[:, jb * bk:(jb + 1) * bk] for za in z_all]
            sp = [jnp.maximum(zh, 0.0) + jnp.log(1.0 + jnp.exp2(-jnp.abs(zh))) * LOG2E for zh in z]
            own = [zh - sh for zh, sh in zip(z, sp)]
            if on_diagonal:
                sp = [jnp.where(keep, sh, 0.0) for sh in sp]
            sp16 = [sh.astype(BF16) for sh in sp]
            later = [_dot(sh, tri) for sh in sp16]
            a = [jnp.exp2(oh - (rh + lh)) for oh, rh, lh in zip(own, run, later)]
            if on_diagonal:
                a = [jnp.where(keep, ah, 0.0) for ah in a]
            for h, ah in enumerate(a):
                weights[h][jb] = ah.astype(BF16)
            run = [rh + lh[:, 0:1] + sh[:, 0:1].astype(F32) for rh, lh, sh in zip(run, later, sp16)]
        for h, hd in enumerate(heads):
            a_all = weights[h][0] if len(weights[h]) == 1 else jnp.concatenate(weights[h], axis=1)
            o_ref[0, rows, hd] = _dot(a_all, v_ref[0, :n_keys, hd]).astype(o_ref.dtype)
    emit_gate_pieces(len(gate_pieces))


def _sb_attention_and_gate(qkv, tri, h, w_in, layer, gate_col0, gate_width, tq, bk, heads_per_step=2):
    b, s, k = h.shape
    n_steps = SB_HEADS // heads_per_step
    n_rhs = gate_width // (n_steps * W_COL)
    assert n_rhs * n_steps * W_COL == gate_width
    tn = n_rhs * W_COL
    blk = (1, s, heads_per_step * HEAD_DIM)

    def w_spec(r):
        return pl.BlockSpec((None, k, W_COL), lambda p, i: (layer, 0, gate_col0 + p * n_rhs + r),
                            pipeline_mode=pl.Buffered(1))

    return pl.pallas_call(
        functools.partial(_sb_kernel, n_rhs=n_rhs, tq=tq, bk=bk),
        out_shape=(jax.ShapeDtypeStruct((b, s, SB_W), BF16), jax.ShapeDtypeStruct((b, s, gate_width), BF16)),
        grid=(n_steps, b),
        in_specs=[pl.BlockSpec(blk, lambda p, i: (i, 0, p)),
                  pl.BlockSpec(blk, lambda p, i: (i, 0, n_steps + p)),
                  pl.BlockSpec(blk, lambda p, i: (i, 0, 2 * n_steps + p)),
                  pl.BlockSpec((bk, bk), lambda p, i: (0, 0)),
                  pl.BlockSpec((1, s, k), lambda p, i: (i, 0, 0))] + [w_spec(r) for r in range(n_rhs)],
        out_specs=(pl.BlockSpec(blk, lambda p, i: (i, 0, p)), pl.BlockSpec((1, s, tn), lambda p, i: (i, 0, p))),
        scratch_shapes=[pltpu.VMEM((k, tn), BF16)],
        compiler_params=_params(2, VMEM_RESIDENT_MIB),
        name="sb_attn_gate",
    )(qkv, qkv, qkv, tri, h, *([w_in] * n_rhs))


def _dil_kernel(*refs, n_sub, has_prev, dilation, slopes, n_other):
    refs = list(refs)
    q_ref, kc_ref = refs.pop(0), refs.pop(0)
    kp_ref = refs.pop(0) if has_prev else None
    vc_ref = refs.pop(0)
    vp_ref = refs.pop(0) if has_prev else None
    others = [(refs.pop(0), refs.pop(0)) for _ in range(n_other)]
    o_ref = refs.pop(0)
    lse_ref = None if n_other else refs.pop(0)
    o_slab, lse_slab = refs

    n_keys = 2 * BLOCK if has_prev else BLOCK
    a_idx = lax.broadcasted_iota(jnp.int32, (BLOCK, n_keys), 0)
    b_idx = lax.broadcasted_iota(jnp.int32, (BLOCK, n_keys), 1)
    rel = a_idx + (n_keys - BLOCK) - b_idx
    in_window = (rel >= 0) & (rel <= BLOCK)
    dist = (dilation * rel).astype(F32)
    bias = [(slope * LOG2E) * dist for slope in slopes]
    lane = lax.broadcasted_iota(jnp.int32, (BLOCK, LANES), 1)
    step = jnp.zeros((BLOCK, n_keys), jnp.int32) + pl.program_id(1)
    valid_first = in_window & ((b_idx >= BLOCK) | (step > 0)) if has_prev else in_window

    def operands(r, j, h):
        rows = slice(j * BLOCK, (j + 1) * BLOCK)
        cols = slice(h * HEAD_DIM, (h + 1) * HEAD_DIM)
        q, kk, vv = q_ref[0, r, rows, cols], kc_ref[0, r, rows, cols], vc_ref[0, r, rows, cols]
        if has_prev:
            if j == 0:
                k_prev, v_prev = kp_ref[0, r, :, cols], vp_ref[0, r, :, cols]
            else:
                prev_rows = slice((j - 1) * BLOCK, j * BLOCK)
                k_prev, v_prev = kc_ref[0, r, prev_rows, cols], vc_ref[0, r, prev_rows, cols]
            kk = jnp.concatenate([k_prev, kk], axis=0)
            vv = jnp.concatenate([v_prev, vv], axis=0)
        return q, kk, vv

    blocks = [(r, j, h) for r in range(dilation) for j in range(n_sub) for h in range(DIL_HEADS_PER_GROUP)]
    for c0 in range(0, len(blocks), DIL_CHAINS):
        chunk = blocks[c0:c0 + DIL_CHAINS]
        ops = [operands(*blk) for blk in chunk]
        s = [jnp.where(valid_first if j == 0 else in_window, _dot_nt(q, kk) - bias[h], -jnp.inf)
             for (r, j, h), (q, kk, vv) in zip(chunk, ops)]
        m = [jnp.max(si, axis=-1, keepdims=True) for si in s]
        p = [jnp.exp2(si - mi) for si, mi in zip(s, m)]
        l = [jnp.sum(pi, axis=-1, keepdims=True) for pi in p]
        o = [_dot(pi.astype(BF16), vv) / li for pi, li, (q, kk, vv) in zip(p, l, ops)]
        lse = [mi * LN2 + jnp.log(li) for mi, li in zip(m, l)]
        for (r, j, h), oi in zip(chunk, o):
            o_slab[h, pl.ds(j * BLOCK * dilation + r, BLOCK, stride=dilation), :] = oi
        for g0 in range(0, len(chunk), DIL_HEADS_PER_GROUP):
            r, j, _ = chunk[g0]
            packed = lse[g0 + DIL_HEADS_PER_GROUP - 1]
            for h in reversed(range(DIL_HEADS_PER_GROUP - 1)):
                packed = jnp.where(lane < (h + 1) * LSE_LANES, lse[g0 + h], packed)
            lse_slab[pl.ds(j * BLOCK * dilation + r, BLOCK, stride=dilation), :] = packed

    if not others:
        lse_ref[0] = lse_slab[...]
        for h in range(DIL_HEADS_PER_GROUP):
            o_ref[0, :, h * HEAD_DIM:(h + 1) * HEAD_DIM] = o_slab[h].astype(o_ref.dtype)
        return
    lses = [lse_slab[...]] + [lg_ref[0] for _, lg_ref in others]
    top = functools.reduce(jnp.maximum, lses)
    e = [jnp.exp(lg - top) for lg in lses]
    inv = 1.0 / functools.reduce(lambda a, b: a + b, e)
    weights = [eg * inv for eg in e]
    for h in range(DIL_HEADS_PER_GROUP):
        cols = slice(h * HEAD_DIM, (h + 1) * HEAD_DIM)
        lse_col = slice(h * LSE_LANES, h * LSE_LANES + 1)
        outs = [o_slab[h]] + [og_ref[0, :, cols].astype(F32) for og_ref, _ in others]
        mixed = functools.reduce(lambda a, b: a + b, [wg[:, lse_col] * og for wg, og in zip(weights, outs)])
        o_ref[0, :, cols] = mixed.astype(o_ref.dtype)


def _dil_attention(qkv, batch, group, slopes, n_sub, others=()):
    window, dilation = DIL_GROUPS[group]
    s = qkv.shape[0] // batch
    sub_len = s // dilation
    n_blk = sub_len // BLOCK
    n_sub = min(n_sub, n_blk)
    has_prev = n_blk > 1
    assert window // dilation == BLOCK and sub_len % BLOCK == 0 and n_blk % n_sub == 0
    view = qkv.reshape(batch, dilation, sub_len, 3 * DIL_OUT)
    rows = n_sub * BLOCK

    def cur(col):
        return pl.BlockSpec((1, dilation, rows, DIL_OUT), lambda i, n: (i, 0, n, col))

    def prev(col):
        return pl.BlockSpec((1, dilation, BLOCK, DIL_OUT),
                            lambda i, n: (i, 0, jnp.maximum(n * n_sub - 1, 0), col))

    if has_prev:
        in_specs = [cur(0), cur(1), prev(1), cur(2), prev(2)]
    else:
        in_specs = [cur(0), cur(1), cur(2)]
    args = [view] * len(in_specs)
    tile = pl.BlockSpec((1, dilation * rows, DIL_OUT), lambda i, n: (i, n, 0))
    lse_tile = pl.BlockSpec((1, dilation * rows, LANES), lambda i, n: (i, n, 0))
    for o_g, lse_g in others:
        in_specs += [tile, lse_tile]
        args += [o_g, lse_g]
    o_shape = jax.ShapeDtypeStruct((batch, s, DIL_OUT), BF16)
    lse_shape = jax.ShapeDtypeStruct((batch, s, LANES), F32)
    return pl.pallas_call(
        functools.partial(_dil_kernel, n_sub=n_sub, has_prev=has_prev, dilation=dilation,
                          slopes=slopes, n_other=len(others)),
        out_shape=o_shape if others else (o_shape, lse_shape),
        grid=(batch, n_blk // n_sub),
        in_specs=in_specs,
        out_specs=tile if others else (tile, lse_tile),
        scratch_shapes=[pltpu.VMEM((DIL_HEADS_PER_GROUP, dilation * rows, HEAD_DIM), F32),
                        pltpu.VMEM((dilation * rows, LANES), F32)],
        compiler_params=_params(2, VMEM_RESIDENT_MIB),
        name=f"dil_attn_d{dilation}",
    )(*args)


def _post_kernel(osb_ref, odl_ref, gp_ref, bg_ref, x_ref, mod_ref, g2_ref, wsb_hbm, wdl_hbm, wout_hbm,
                 x1_ref, h2_ref, wsb_ref, wdl_ref, wout_ref, stage_ref, sem_ref, *, layer):
    @pl.when(pl.program_id(0) == 0)
    def _():
        for w_hbm, wb_ref in ((wsb_hbm, wsb_ref), (wdl_hbm, wdl_ref), (wout_hbm, wout_ref)):
            _load_weight_bf16(w_hbm, layer, wb_ref, stage_ref, sem_ref)

    d = x_ref.shape[-1]
    chunks = _row_chunks(x_ref.shape[0])
    y_sb = [_dot(osb_ref[c, :], wsb_ref[...]) for c in chunks]
    y_dl = [_dot(odl_ref[c, :], wdl_ref[...]) for c in chunks]
    gates = [_sigmoid(gp_ref[c, :].astype(F32) + bg_ref[...]) for c in chunks]
    mixed = [(g[:, :d] * a + g[:, d:] * b).astype(BF16) for g, a, b in zip(gates, y_sb, y_dl)]
    t = [_dot(mx, wout_ref[...]) for mx in mixed]
    for c, tc in zip(chunks, t):
        x1 = x_ref[c, :] + mod_ref[0, 2:3, :] * tc
        x1_ref[c, :] = x1
        h2 = _rms(x1) * g2_ref[...] * (1.0 + mod_ref[0, 4:5, :]) + mod_ref[0, 3:4, :]
        h2_ref[c, :] = h2.astype(h2_ref.dtype)


def _post(o_sb, o_dl, gate_pre, b_gate, x2d, mod3, g2, w_sb, w_dl, w_out, layer, seq, tm=512):
    m, d = x2d.shape
    weights = (w_sb, w_dl, w_out)
    assert all(w.shape[1] % W_STAGE_ROWS == 0 and w.shape[2] == d for w in weights)

    def row(width):
        return pl.BlockSpec((tm, width), lambda i: (i, 0))

    def whole(arr):
        return pl.BlockSpec(arr.shape, lambda i: (0,) * arr.ndim, pipeline_mode=pl.Buffered(1))

    return pl.pallas_call(
        functools.partial(_post_kernel, layer=layer),
        out_shape=(jax.ShapeDtypeStruct((m, d), F32), jax.ShapeDtypeStruct((m, d), BF16)),
        grid=(m // tm,),
        in_specs=[row(SB_W), row(DIL_OUT), row(2 * d), whole(b_gate), row(d),
                  pl.BlockSpec((1, N_MOD, d), lambda i: ((i * tm) // seq, 0, 0)),
                  whole(g2)] + [pl.BlockSpec(memory_space=pl.ANY)] * len(weights),
        out_specs=(row(d), row(d)),
        scratch_shapes=[pltpu.VMEM(w.shape[1:], BF16) for w in weights]
                       + [pltpu.VMEM((W_STAGE_SLOTS, W_STAGE_ROWS, d), F32),
                          pltpu.SemaphoreType.DMA((W_STAGE_SLOTS,))],
        compiler_params=_params(1, VMEM_RESIDENT_MIB),
        name="post",
    )(o_sb, o_dl, gate_pre, b_gate, x2d, mod3, g2, *weights)


def _swiglu_kernel(h_ref, wg_ref, wu_ref, o_ref, wgb_ref, wub_ref):
    @pl.when(pl.program_id(1) == 0)
    def _():
        wgb_ref[...] = wg_ref[...].astype(BF16)
        wub_ref[...] = wu_ref[...].astype(BF16)

    for c in _row_chunks(h_ref.shape[0], 4 * SUB_ROWS):
        g = _dot(h_ref[c, :], wgb_ref[...])
        u = _dot(h_ref[c, :], wub_ref[...])
        o_ref[c, :] = (g * _sigmoid(g) * u).astype(o_ref.dtype)


def _swiglu(h, wg, wu, layer, tm=2048, tn=512):
    m, k = h.shape
    n = wg.shape[-1]
    w_spec = pl.BlockSpec((None, k, tn), lambda j, i: (layer, 0, j))
    return pl.pallas_call(
        _swiglu_kernel,
        out_shape=jax.ShapeDtypeStruct((m, n), BF16),
        grid=(n // tn, m // tm),
        in_specs=[pl.BlockSpec((tm, k), lambda j, i: (i, 0)), w_spec, w_spec],
        out_specs=pl.BlockSpec((tm, tn), lambda j, i: (i, j)),
        scratch_shapes=[pltpu.VMEM((k, tn), BF16), pltpu.VMEM((k, tn), BF16)],
        compiler_params=_params(2, VMEM_RESIDENT_MIB),
        name="swiglu",
    )(h, wg, wu)


def _load_weight_bf16(w_hbm, layer, dst_ref, stage_ref, sem_ref):
    n_chunks = dst_ref.shape[0] // W_STAGE_ROWS
    n_slots = stage_ref.shape[0]
    ahead = min(n_slots - 1, n_chunks)

    def chunk_copy(c, slot):
        src = w_hbm.at[layer, pl.ds(c * W_STAGE_ROWS, W_STAGE_ROWS), :]
        return pltpu.make_async_copy(src, stage_ref.at[slot], sem_ref.at[slot])

    for c in range(ahead):
        chunk_copy(c, c).start()

    def body(c, carry):
        slot = lax.rem(c, n_slots)

        @pl.when(c + ahead < n_chunks)
        def _():
            chunk_copy(c + ahead, lax.rem(c + ahead, n_slots)).start()

        chunk_copy(c, slot).wait()
        rows = pl.ds(pl.multiple_of(c * W_STAGE_ROWS, W_STAGE_ROWS), W_STAGE_ROWS)
        dst_ref[rows, :] = stage_ref[slot].astype(dst_ref.dtype)
        return carry

    lax.fori_loop(0, n_chunks, body, 0)


def _down_kernel(a_ref, w_hbm, x1_ref, mod_ref, gf_ref, o_ref, wb_ref, stage_ref, sem_ref, *, layer, final_norm):
    @pl.when(pl.program_id(0) == 0)
    def _():
        _load_weight_bf16(w_hbm, layer, wb_ref, stage_ref, sem_ref)

    for c in _row_chunks(a_ref.shape[0]):
        x2 = x1_ref[c, :] + mod_ref[0, 5:6, :] * _dot(a_ref[c, :], wb_ref[...])
        o_ref[c, :] = _rms(x2) * gf_ref[...] if final_norm else x2


def _down(a, w, layer, x1, mod3, g_final, seq, final_norm, tm=512):
    m, f = a.shape
    d = w.shape[-1]
    assert f % W_STAGE_ROWS == 0
    return pl.pallas_call(
        functools.partial(_down_kernel, layer=layer, final_norm=final_norm),
        out_shape=jax.ShapeDtypeStruct((m, d), F32),
        grid=(m // tm,),
        in_specs=[pl.BlockSpec((tm, f), lambda i: (i, 0)),
                  pl.BlockSpec(memory_space=pl.ANY),
                  pl.BlockSpec((tm, d), lambda i: (i, 0)),
                  pl.BlockSpec((1, N_MOD, d), lambda i: ((i * tm) // seq, 0, 0)),
                  pl.BlockSpec((1, d), lambda i: (0, 0))],
        out_specs=pl.BlockSpec((tm, d), lambda i: (i, 0)),
        scratch_shapes=[pltpu.VMEM((f, d), BF16), pltpu.VMEM((W_STAGE_SLOTS, W_STAGE_ROWS, d), F32),
                        pltpu.SemaphoreType.DMA((W_STAGE_SLOTS,))],
        compiler_params=_params(1, VMEM_RESIDENT_MIB),
        name="down",
    )(a, w, x1, mod3, g_final)


def _alibi_slopes():
    return tuple(2.0 ** (-8.0 * (i + 1.0) / DIL_HEADS) for i in range(DIL_HEADS))


def kernel(x, c, w_ada, b_ada, g_norm1, g_norm2, g_final, w_in, b_gate, w_proj_sb, w_proj_dil,
           w_out, w_ffn_gate, w_ffn_up, w_ffn_down):
    batch, seq, d = x.shape
    tokens = batch * seq
    n_layers = w_ada.shape[0]
    slopes = _alibi_slopes()
    n_groups = len(DIL_GROUPS)
    sb_tile = 256
    idx = jnp.arange(sb_tile)
    tri = (idx[:, None] > idx[None, :]).astype(BF16)
    c_pad = jnp.pad(c, ((0, ADA_ROWS - batch), (0, 0)))
    perm_dils = tuple(sorted(dil for _, dil in DIL_GROUPS if dil > 1))
    dl_col0 = 3 * SB_W // W_COL
    gate_col0 = dl_col0 + 3 * DIL_W // W_COL
    assert DIL_OUT == W_COL
    group_order = sorted(range(n_groups), key=lambda g: -DIL_GROUPS[g][1])
    assert DIL_GROUPS[group_order[-1]][1] == 1

    for l in range(n_layers):
        mod3 = _ada(c_pad, w_ada[l], b_ada[l][None, :]).reshape(ADA_ROWS, N_MOD, d)

        h_all = _norm1(x, g_norm1[l][None, :], mod3, perm_dils)
        h = h_all[0].reshape(tokens, d)
        h_perm = dict(zip(perm_dils, (hp.reshape(tokens, d) for hp in h_all[1:])))

        sb_tiles = 3 * SB_W // (2 * W_COL)
        qkv_sb = _proj(h, w_in, l, lambda j, r: 2 * j + r, sb_tiles, 2, "proj_sb",
                       query_blocks=[(j, r) for j in range(sb_tiles // 3) for r in range(2)])
        o_sb, gate_pre = _sb_attention_and_gate(qkv_sb.reshape(batch, seq, 3 * SB_W), tri, h_all[0], w_in, l,
                                                gate_col0, 2 * d, sb_tile, sb_tile)
        gate_pre = gate_pre.reshape(tokens, 2 * d)

        partial_groups = []
        for g in group_order:
            dil = DIL_GROUPS[g][1]
            qkv_g = _proj(h_perm.get(dil, h), w_in, l,
                          lambda j, r, g=g: dl_col0 + r * n_groups + g, 1, 3, f"proj_dil_d{dil}",
                          query_blocks=[(0, 0)])
            g_slopes = slopes[g * DIL_HEADS_PER_GROUP:(g + 1) * DIL_HEADS_PER_GROUP]
            if g != group_order[-1]:
                partial_groups.append(_dil_attention(qkv_g, batch, g, g_slopes, n_sub=4))
            else:
                o_dl = _dil_attention(qkv_g, batch, g, g_slopes, n_sub=8, others=tuple(partial_groups))

        x1, h2 = _post(o_sb.reshape(tokens, SB_W), o_dl.reshape(tokens, DIL_OUT), gate_pre,
                       b_gate[l][None, :], x.reshape(tokens, d), mod3, g_norm2[l][None, :],
                       w_proj_sb, w_proj_dil, w_out, l, seq)

        a = _swiglu(h2, w_ffn_gate, w_ffn_up, l)
        x = _down(a, w_ffn_down, l, x1, mod3, g_final[None, :], seq,
                  final_norm=(l == n_layers - 1)).reshape(batch, seq, d)
    return x
```
